```python
import math, functools
import jax, jax.numpy as jnp
from jax import lax
import numpy as np

D_MODEL = 1024
BATCH = 4
SEQ = 4096
DEPTH = 2
DEC_BATCH = 128
DEC_SEQ = 8
PAST_LEN = 2048
PAGE_SIZE = 128

HEAD_DIM = 64
A_GROUPS = 4
A_WIDTH = A_GROUPS * HEAD_DIM
CHUNK = 128
B_HEADS = 8
B_WIDTH = B_HEADS * HEAD_DIM
IDX_HEADS = 8
IDX_DIM = 64
TOPK_MAX = 256
Q_BLOCK = 128
NUM_BUCKETS = 32
MAX_DISTANCE = 128
C_HEADS = 4
C_WIDTH = C_HEADS * HEAD_DIM
RET_CHUNK = 128
ROPE_BASE = 10000.0
MIX_WIDTH = A_WIDTH + B_WIDTH + C_WIDTH
IN_SPLITS = (A_WIDTH, A_WIDTH, B_WIDTH, B_WIDTH, B_WIDTH, IDX_HEADS * IDX_DIM, IDX_DIM, IDX_HEADS,
             C_WIDTH, C_WIDTH, C_WIDTH, C_WIDTH)
IN_WIDTH = sum(IN_SPLITS)
N_GROUPS = 4
EXPERTS_PER_GROUP = 8
N_EXPERTS = N_GROUPS * EXPERTS_PER_GROUP
TOP_K_EXPERTS = 2
EXPERT_FF = 256
ALPHA = (2 * DEPTH) ** 0.25
BETA = (8 * DEPTH) ** -0.25
LN_EPS = 1e-5

kernel_name = "hymba_gmlp_dsa_retnet_hmoe_step"


def layer_norm(x, gain, bias):
    xf = x.astype(jnp.float32)
    mu = jnp.mean(xf, axis=-1, keepdims=True)
    var = jnp.mean(jnp.square(xf - mu), axis=-1, keepdims=True)
    return ((xf - mu) * lax.rsqrt(var + LN_EPS)).astype(x.dtype) * gain + bias


def rotary(x, pos):
    half = x.shape[-1] // 2
    inv = ROPE_BASE ** (-jnp.arange(half, dtype=jnp.float32) / half)
    ang = pos.astype(jnp.float32)[:, None] * inv[None, :]
    cos = jnp.cos(ang)[None, :, None, :].astype(x.dtype)
    sin = jnp.sin(ang)[None, :, None, :].astype(x.dtype)
    x1, x2 = x[..., :half], x[..., half:]
    return jnp.concatenate([x1 * cos - x2 * sin, x1 * sin + x2 * cos], axis=-1)


def t5_bucket(dist):
    n = jnp.maximum(dist, 0)
    max_exact = NUM_BUCKETS // 2
    nf = jnp.maximum(n, 1).astype(jnp.float32)
    large = max_exact + (jnp.log(nf / max_exact) / math.log(MAX_DISTANCE / max_exact)
                         * (NUM_BUCKETS - max_exact)).astype(jnp.int32)
    large = jnp.minimum(large, NUM_BUCKETS - 1)
    return jnp.where(n < max_exact, n, large)


def chunk_spatial_mix(v, ws, bs):
    B, L = v.shape[:2]
    n = -(-L // CHUNK)
    vp = jnp.pad(v, ((0, 0), (0, n * CHUNK - L), (0, 0), (0, 0))).reshape(B, n, CHUNK, A_GROUPS, HEAD_DIM)
    wm = ws * jnp.tril(jnp.ones((CHUNK, CHUNK), ws.dtype))
    out = jnp.einsum('gij,bnjgd->bnigd', wm, vp) + bs.T[None, None, :, :, None]
    return out.reshape(B, n * CHUNK, A_GROUPS, HEAD_DIM)[:, :L]


def retention(q, k, v, s0):
    B, L = q.shape[:2]
    c = RET_CHUNK if L % RET_CHUNK == 0 else L
    n = L // c
    log_g = jnp.log(1.0 - 2.0 ** (-5.0 - jnp.arange(C_HEADS, dtype=jnp.float32)))
    i = jnp.arange(c, dtype=jnp.float32)
    diff = i[:, None] - i[None, :]
    dmat = jnp.where(diff >= 0, jnp.exp(log_g[:, None, None] * jnp.maximum(diff, 0.0)), 0.0).astype(q.dtype)
    q_dec = jnp.exp(log_g[:, None] * (i + 1.0)).astype(q.dtype)[None, :, :, None]
    k_dec = jnp.exp(log_g[:, None] * (c - 1.0 - i)).astype(q.dtype)[None, :, :, None]
    s_dec = jnp.exp(log_g * c).astype(q.dtype)[None, :, None, None]

    def to_chunks(t):
        return t.reshape(B, n, c, C_HEADS, HEAD_DIM).transpose(1, 0, 3, 2, 4)

    def step(s, qkv):
        qi, ki, vi = qkv
        att = jnp.einsum('bhid,bhjd->bhij', qi, ki) * dmat
        o = jnp.einsum('bhij,bhjd->bhid', att, vi) + jnp.einsum('bhid,bhde->bhie', qi, s) * q_dec
        s_new = s * s_dec + jnp.einsum('bhjd,bhje->bhde', ki * k_dec, vi)
        return s_new, o

    s_fin, oc = lax.scan(step, s0, (to_chunks(q), to_chunks(k), to_chunks(v)))
    return oc.transpose(1, 0, 3, 2, 4).reshape(B, L, C_HEADS, HEAD_DIM), s_fin


def dsa_select_attend(q, iq, iw, ik_all, qpos, gather_kv, rel_bias, topk):
    L = ik_all.shape[1]
    kpos = jnp.arange(L, dtype=jnp.int32)
    s = jnp.einsum('bthe,ble->bthl', iq.astype(jnp.float32), ik_all.astype(jnp.float32)) * IDX_DIM ** -0.5
    score = jnp.einsum('bthl,bth->btl', jax.nn.relu(s), iw.astype(jnp.float32)) * IDX_HEADS ** -0.5
    score = jnp.where(kpos[None, None, :] <= qpos[None, :, None], score, -jnp.inf)
    _, sel = lax.top_k(score, topk)
    k_sel, v_sel = gather_kv(sel)
    valid = sel <= qpos[None, :, None]
    bias = rel_bias[t5_bucket(qpos[None, :, None] - sel)]
    logits = (jnp.einsum('bthd,btkhd->bthk', q, k_sel).astype(jnp.float32) * HEAD_DIM ** -0.5
              + bias.transpose(0, 1, 3, 2).astype(jnp.float32))
    logits = jnp.where(valid[:, :, None, :], logits, -jnp.inf)
    p = jax.nn.softmax(logits, axis=-1).astype(v_sel.dtype)
    return jnp.einsum('bthk,btkhd->bthd', p, v_sel)


def dsa_prompt(q, k, v, iq, iw, ik, rel_bias):
    B, L = q.shape[:2]
    topk = min(TOPK_MAX, L // 4)
    nblk = L // Q_BLOCK

    def gather_kv(sel):
        take = jax.vmap(lambda t, s: t[s])
        return take(k, sel), take(v, sel)

    def block(bi):
        st = bi * Q_BLOCK
        sl = lambda t: lax.dynamic_slice_in_dim(t, st, Q_BLOCK, axis=1)
        qpos = st + jnp.arange(Q_BLOCK, dtype=jnp.int32)
        return dsa_select_attend(sl(q), sl(iq), sl(iw), ik, qpos, gather_kv, rel_bias, topk)

    o = lax.map(block, jnp.arange(nblk, dtype=jnp.int32))
    return o.transpose(1, 0, 2, 3, 4).reshape(B, L, B_HEADS, HEAD_DIM)


def dsa_sample(q, k, v, iq, iw, ik, cache_k_l, cache_v_l, cache_ik_l, page_table, rel_bias):
    Bd, T = q.shape[:2]
    L = PAST_LEN + T
    topk = min(TOPK_MAX, L // 4)
    ik_past = cache_ik_l[page_table].reshape(Bd, PAST_LEN, IDX_DIM)
    ik_all = jnp.concatenate([ik_past, ik], axis=1)
    qpos = PAST_LEN + jnp.arange(T, dtype=jnp.int32)

    def gather_kv(sel):
        in_past = (sel < PAST_LEN)[..., None, None]
        p = jnp.minimum(sel, PAST_LEN - 1)
        phys = page_table[jnp.arange(Bd)[:, None, None], p // PAGE_SIZE]
        off = p % PAGE_SIZE
        cur = jnp.clip(sel - PAST_LEN, 0, T - 1)
        take = jax.vmap(lambda t, s: t[s])
        k_sel = jnp.where(in_past, cache_k_l[phys, off], take(k, cur))
        v_sel = jnp.where(in_past, cache_v_l[phys, off], take(v, cur))
        return k_sel, v_sel

    return dsa_select_attend(q, iq, iw, ik_all, qpos, gather_kv, rel_bias, topk)


def hier_moe(x, rg_w, rg_b, re_w, re_b, e_gate, e_up, e_down):
    B, L, D = x.shape
    xt = x.reshape(B * L, D)
    g_logits = (xt @ rg_w).astype(jnp.float32) + rg_b.astype(jnp.float32)
    g_sel = jnp.argmax(g_logits, axis=-1)
    p_group = jnp.take_along_axis(jax.nn.softmax(g_logits, axis=-1), g_sel[:, None], axis=-1)
    e_logits = ((xt @ re_w).astype(jnp.float32) + re_b.astype(jnp.float32)).reshape(-1, N_GROUPS, EXPERTS_PER_GROUP)
    e_in = jnp.take_along_axis(e_logits, g_sel[:, None, None], axis=1)[:, 0]
    top_v, top_i = lax.top_k(e_in, TOP_K_EXPERTS)
    gates = jax.nn.softmax(top_v, axis=-1) * p_group
    expert_id = g_sel[:, None] * EXPERTS_PER_GROUP + top_i
    combine = jnp.sum(jax.nn.one_hot(expert_id, N_EXPERTS, dtype=jnp.float32) * gates[..., None], axis=1)
    h = jax.nn.silu(jnp.einsum('nd,edf->nef', xt, e_gate)) * jnp.einsum('nd,edf->nef', xt, e_up)
    h = h * combine.astype(x.dtype)[:, :, None]
    return jnp.einsum('nef,efd->nd', h, e_down).reshape(B, L, D)


def layer_forward(x, pos, attend, s0, w_in, w_out, gv_gain, gv_bias, ws, bs, ret_gain, ret_bias,
                  ln1_g, ln1_b, ln2_g, ln2_b, rg_w, rg_b, re_w, re_b, e_gate, e_up, e_down):
    B, L, _ = x.shape
    offs = np.cumsum(IN_SPLITS)[:-1].tolist()
    a_u, a_v, q, k, v, iq, ik, iw, cq, ck, cv, cg = jnp.split(x @ w_in, offs, axis=-1)
    heads = lambda t, h: t.reshape(B, L, h, -1)
    a_v = layer_norm(a_v, gv_gain, gv_bias)
    a_out = a_u * chunk_spatial_mix(heads(a_v, A_GROUPS), ws, bs).reshape(B, L, A_WIDTH)
    k = heads(k, B_HEADS)
    v = heads(v, B_HEADS)
    b_out = attend(heads(q, B_HEADS), k, v, heads(iq, IDX_HEADS), iw, ik).reshape(B, L, B_WIDTH)
    r, s_fin = retention(rotary(heads(cq, C_HEADS), pos),
                         rotary(heads(ck, C_HEADS), pos) * HEAD_DIM ** -0.5,
                         heads(cv, C_HEADS), s0)
    c_out = jax.nn.silu(cg) * layer_norm(r, ret_gain, ret_bias).reshape(B, L, C_WIDTH)
    mix = jnp.concatenate([a_out, b_out, c_out], axis=-1) @ w_out
    x = layer_norm(ALPHA * x + mix, ln1_g, ln1_b)
    x = layer_norm(ALPHA * x + hier_moe(x, rg_w, rg_b, re_w, re_b, e_gate, e_up, e_down), ln2_g, ln2_b)
    last = ((L - 1) // CHUNK) * CHUNK
    return x, (k, v, ik, s_fin, a_v[:, last:])


def setup_inputs(seed: int = 0) -> dict:
    key = jax.random.key(seed)
    ks = jax.random.split(key, 32)
    f32 = jnp.float32
    n_pages = PAST_LEN // PAGE_SIZE
    n_phys = (5 * DEC_BATCH * n_pages) // 4
    nrm = lambda k, shape, scale: jax.random.normal(k, shape, f32) * scale
    page_table = jax.random.permutation(ks[6], n_phys)[:DEC_BATCH * n_pages].reshape(DEC_BATCH, n_pages).astype(jnp.int32)
    return {
        "x_prompt": nrm(ks[0], (BATCH, SEQ, D_MODEL), 1.0),
        "x_sample": nrm(ks[1], (DEC_BATCH, DEC_SEQ, D_MODEL), 1.0),
        "cache_k": nrm(ks[2], (DEPTH, n_phys, PAGE_SIZE, B_HEADS, HEAD_DIM), 1.0),
        "cache_v": nrm(ks[3], (DEPTH, n_phys, PAGE_SIZE, B_HEADS, HEAD_DIM), 1.0),
        "cache_idx_k": nrm(ks[4], (DEPTH, n_phys, PAGE_SIZE, IDX_DIM), 1.0),
        "state_ret": nrm(ks[5], (DEPTH, DEC_BATCH, C_HEADS, HEAD_DIM, HEAD_DIM), 1.0),
        "page_table": page_table,
        "w_in": nrm(ks[7], (DEPTH, D_MODEL, IN_WIDTH), D_MODEL ** -0.5),
        "w_out": nrm(ks[8], (DEPTH, MIX_WIDTH, D_MODEL), MIX_WIDTH ** -0.5 * BETA),
        "gmlp_v_gain": 1.0 + nrm(ks[9], (DEPTH, A_WIDTH), 0.02),
        "gmlp_v_bias": nrm(ks[10], (DEPTH, A_WIDTH), 0.02),
        "gmlp_ws": nrm(ks[11], (DEPTH, A_GROUPS, CHUNK, CHUNK), CHUNK ** -0.5),
        "gmlp_bs": 1.0 + nrm(ks[12], (DEPTH, A_GROUPS, CHUNK), 0.02),
        "rel_bias": nrm(ks[13], (NUM_BUCKETS, B_HEADS), 0.5),
        "ret_gn_gain": 1.0 + nrm(ks[14], (DEPTH, C_HEADS, HEAD_DIM), 0.02),
        "ret_gn_bias": nrm(ks[15], (DEPTH, C_HEADS, HEAD_DIM), 0.02),
        "ln1_gain": 1.0 + nrm(ks[16], (DEPTH, D_MODEL), 0.02),
        "ln1_bias": nrm(ks[17], (DEPTH, D_MODEL), 0.02),
        "ln2_gain": 1.0 + nrm(ks[18], (DEPTH, D_MODEL), 0.02),
        "ln2_bias": nrm(ks[19], (DEPTH, D_MODEL), 0.02),
        "router_group_w": nrm(ks[20], (DEPTH, D_MODEL, N_GROUPS), D_MODEL ** -0.5),
        "router_group_b": nrm(ks[21], (DEPTH, N_GROUPS), 0.01),
        "router_expert_w": nrm(ks[22], (DEPTH, D_MODEL, N_EXPERTS), D_MODEL ** -0.5),
        "router_expert_b": nrm(ks[23], (DEPTH, N_EXPERTS), 0.01),
        "expert_w_gate": nrm(ks[24], (DEPTH, N_EXPERTS, D_MODEL, EXPERT_FF), D_MODEL ** -0.5),
        "expert_w_up": nrm(ks[25], (DEPTH, N_EXPERTS, D_MODEL, EXPERT_FF), D_MODEL ** -0.5),
        "expert_w_down": nrm(ks[26], (DEPTH, N_EXPERTS, EXPERT_FF, D_MODEL), EXPERT_FF ** -0.5 * BETA),
    }


def reference(x_prompt, x_sample, cache_k, cache_v, cache_idx_k, state_ret, page_table,
              w_in, w_out, gmlp_v_gain, gmlp_v_bias, gmlp_ws, gmlp_bs, rel_bias, ret_gn_gain, ret_gn_bias,
              ln1_gain, ln1_bias, ln2_gain, ln2_bias, router_group_w, router_group_b,
              router_expert_w, router_expert_b, expert_w_gate, expert_w_up, expert_w_down):
    pos_p = jnp.arange(SEQ, dtype=jnp.int32)
    pos_s = PAST_LEN + jnp.arange(DEC_SEQ, dtype=jnp.int32)
    s0_p = jnp.zeros((BATCH, C_HEADS, HEAD_DIM, HEAD_DIM), x_prompt.dtype)
    xp, xs = x_prompt, x_sample
    st_p, st_s = [], []
    attn_p = functools.partial(dsa_prompt, rel_bias=rel_bias)
    for l in range(DEPTH):
        lw = (w_in[l], w_out[l], gmlp_v_gain[l], gmlp_v_bias[l], gmlp_ws[l], gmlp_bs[l],
              ret_gn_gain[l], ret_gn_bias[l], ln1_gain[l], ln1_bias[l], ln2_gain[l], ln2_bias[l],
              router_group_w[l], router_group_b[l], router_expert_w[l], router_expert_b[l],
              expert_w_gate[l], expert_w_up[l], expert_w_down[l])
        attn_s = functools.partial(dsa_sample, cache_k_l=cache_k[l], cache_v_l=cache_v[l],
                                   cache_ik_l=cache_idx_k[l], page_table=page_table, rel_bias=rel_bias)
        xp, sp = layer_forward(xp, pos_p, attn_p, s0_p, *lw)
        xs, ss = layer_forward(xs, pos_s, attn_s, state_ret[l], *lw)
        st_p.append(sp)
        st_s.append(ss)
    stk = lambda sts, i: jnp.stack([s[i] for s in sts], axis=0)
    return (xp, xs,
            stk(st_p, 0), stk(st_p, 1), stk(st_p, 2), stk(st_p, 3), stk(st_p, 4),
            stk(st_s, 0), stk(st_s, 1), stk(st_s, 2), stk(st_s, 3), stk(st_s, 4))
```

```python
import functools
import math

import jax
import jax.numpy as jnp
from jax import lax
from jax.experimental import pallas as pl
from jax.experimental.pallas import tpu as pltpu

D_MODEL = 1024
HEAD_DIM = 64
A_GROUPS = 4
A_WIDTH = 256
CHUNK = 128
B_HEADS = 8
B_WIDTH = 512
IDX_HEADS = 8
IDX_DIM = 64
TOPK_MAX = 256
Q_BLOCK = 128
NUM_BUCKETS = 32
MAX_DISTANCE = 128
C_HEADS = 4
C_WIDTH = 256
RET_CHUNK = 128
ROPE_BASE = 10000.0
PAGE_SIZE = 128
N_GROUPS = 4
EXPERTS_PER_GROUP = 8
N_EXPERTS = 32
EXPERT_FF = 256
DEPTH = 2
ALPHA = (2 * DEPTH) ** 0.25
LN_EPS = 1e-5

F32 = jnp.float32
BF16 = jnp.bfloat16
I32 = jnp.int32
LANES = 128
MASK_NEG = -1e30
INT_MIN = -2 ** 31

COL_AU, COL_AV, COL_Q, COL_K, COL_V, COL_IQ, COL_IK, COL_IW = 0, 256, 512, 1024, 1536, 2048, 2560, 2688
COL_CQ, COL_CK, COL_CV, COL_CG = 2816, 3072, 3328, 3584
H_WIDTH = 3840


def _cparams(sem, vmem_mb=None):
    kw = dict(dimension_semantics=sem)
    if vmem_mb is not None:
        kw["vmem_limit_bytes"] = vmem_mb << 20
    return pltpu.CompilerParams(**kw)


def _nt_dot(a, b):
    return lax.dot_general(a, b, (((1,), (1,)), ((), ())), preferred_element_type=F32)


def _dot(a, b):
    return jnp.dot(a, b, preferred_element_type=F32)


def _layer_norm_rows(x, gain, bias):
    mu = jnp.mean(x, axis=-1, keepdims=True)
    xc = x - mu
    var = jnp.mean(xc * xc, axis=-1, keepdims=True)
    return xc * lax.rsqrt(var + LN_EPS) * gain + bias


def _silu(x):
    return x * (1.0 / (1.0 + jnp.exp(-x)))


def _proj_kernel(x_ref, w_ref, o_ref):
    o_ref[...] = _dot(x_ref[...].astype(BF16), w_ref[...])


def _proj(x2, w_pad):
    n = x2.shape[0]
    tm = 1024 if n > 1024 else 512
    tn = 768
    return pl.pallas_call(
        _proj_kernel,
        grid=(n // tm, H_WIDTH // tn),
        in_specs=[pl.BlockSpec((tm, D_MODEL), lambda i, j: (i, 0)),
                  pl.BlockSpec((D_MODEL, tn), lambda i, j: (0, j))],
        out_specs=pl.BlockSpec((tm, tn), lambda i, j: (i, j)),
        out_shape=jax.ShapeDtypeStruct((n, H_WIDTH), F32),
        compiler_params=_cparams(("arbitrary", "arbitrary"), 40),
        name="in_proj",
    )(x2, w_pad)


def _bias_kernel(rb_ref, d_ref, o_ref):
    n = jnp.maximum(d_ref[...], 0)
    max_exact = NUM_BUCKETS // 2
    nf = jnp.maximum(n, 1).astype(F32)
    large = max_exact + (jnp.log(nf / max_exact) / math.log(MAX_DISTANCE / max_exact)
                         * (NUM_BUCKETS - max_exact)).astype(I32)
    large = jnp.minimum(large, NUM_BUCKETS - 1)
    bucket = jnp.where(n < max_exact, n, large)
    for h in range(B_HEADS):
        acc = jnp.zeros(bucket.shape, F32)
        for bk in range(NUM_BUCKETS):
            acc = jnp.where(bucket == bk, rb_ref[bk * B_HEADS + h], acc)
        o_ref[h] = acc


def _bias_tables(dist, rel_bias):
    r = dist.shape[0]
    return pl.pallas_call(
        _bias_kernel,
        in_specs=[pl.BlockSpec(memory_space=pltpu.SMEM),
                  pl.BlockSpec((r, LANES), lambda: (0, 0))],
        out_specs=pl.BlockSpec((B_HEADS, r, LANES), lambda: (0, 0, 0)),
        out_shape=jax.ShapeDtypeStruct((B_HEADS, r, LANES), F32),
        name="bias_tables",
    )(rel_bias.reshape(-1), dist)


def _gmlp_kernel(u_ref, v_ref, g_ref, b_ref, ws_ref, bsb_ref, o_ref, vn_ref, *, nchunk):
    r = lax.broadcasted_iota(I32, (CHUNK, CHUNK), 0)
    c = lax.broadcasted_iota(I32, (CHUNK, CHUNK), 1)
    grp = lax.broadcasted_iota(I32, (CHUNK, A_WIDTH), 1) // HEAD_DIM
    wts = [jnp.where(r >= c, ws_ref[g], 0.0).astype(BF16) for g in range(A_GROUPS)]
    gain, bias, bsb = g_ref[...], b_ref[...], bsb_ref[...]
    for ci in range(nchunk):
        sl = pl.ds(ci * CHUNK, CHUNK)
        vn = _layer_norm_rows(v_ref[sl, :], gain, bias)
        vn_ref[sl, :] = vn
        vb = vn.astype(BF16)
        mixed = bsb
        for g in range(A_GROUPS):
            mixed = mixed + jnp.where(grp == g, _dot(wts[g], vb), 0.0)
        o_ref[sl, :] = (u_ref[sl, :] * mixed).astype(BF16)


def _gmlp(h2, gain, bias, ws, bsb):
    n = h2.shape[0]
    tm = min(n, 1024)
    blk = lambda col: pl.BlockSpec((tm, A_WIDTH), lambda i: (i, col // A_WIDTH))
    full2 = lambda a: pl.BlockSpec(a.shape, lambda i: (0, 0))
    return pl.pallas_call(
        functools.partial(_gmlp_kernel, nchunk=tm // CHUNK),
        grid=(n // tm,),
        in_specs=[blk(COL_AU), blk(COL_AV), full2(gain), full2(bias),
                  pl.BlockSpec(ws.shape, lambda i: (0, 0, 0)), full2(bsb)],
        out_specs=[pl.BlockSpec((tm, A_WIDTH), lambda i: (i, 0)),
                   pl.BlockSpec((tm, A_WIDTH), lambda i: (i, 0))],
        out_shape=[jax.ShapeDtypeStruct((n, A_WIDTH), BF16),
                   jax.ShapeDtypeStruct((n, A_WIDTH), F32)],
        compiler_params=_cparams(("arbitrary",)),
        name="gmlp",
    )(h2, h2, gain, bias, ws, bsb)


def _ret_tables(c):
    log_g = jnp.log(1.0 - 2.0 ** (-5.0 - jnp.arange(C_HEADS, dtype=F32)))
    i = jnp.arange(c, dtype=F32)
    diff = i[:, None] - i[None, :]
    dmat = jnp.where(diff >= 0, jnp.exp(log_g[:, None, None] * jnp.maximum(diff, 0.0)), 0.0)
    q_dec = jnp.exp(log_g[:, None] * (i + 1.0))
    k_dec = jnp.exp(log_g[:, None] * (c - 1.0 - i))
    s_dec = jnp.exp(log_g * c)
    qd = jnp.repeat(q_dec.T, HEAD_DIM, axis=1)
    kd = jnp.repeat(k_dec.T, HEAD_DIM, axis=1)
    hid = jnp.arange(C_WIDTH) // HEAD_DIM
    same = hid[:, None] == hid[None, :]
    sd = jnp.where(same, s_dec[hid][:, None], 0.0)
    return dmat, qd, kd, sd, same.astype(F32)


def _rope_tables(pos):
    half = HEAD_DIM // 2
    inv = ROPE_BASE ** (-jnp.arange(half, dtype=F32) / half)
    ang = pos.astype(F32)[:, None] * inv[None, :]
    cos, sin = jnp.cos(ang), jnp.sin(ang)
    cosf = jnp.tile(jnp.concatenate([cos, cos], axis=1), (1, C_HEADS))
    sins = jnp.tile(jnp.concatenate([-sin, sin], axis=1), (1, C_HEADS))
    return cosf, sins


def _ret_kernel(q_ref, k_ref, v_ref, g_ref, cos_ref, sin_ref, qd_ref, kd_ref, dm_ref, sd_ref, bd_ref,
                gg_ref, gb_ref, s0_ref, o_ref, sf_ref, s_scr, *, bt, c):
    ci = pl.program_id(1)

    @pl.when(ci == 0)
    def _():
        s_scr[...] = s0_ref[...]

    lane = lax.broadcasted_iota(I32, (c, C_WIDTH), 1)
    hid = lane // HEAD_DIM
    first = (lane % HEAD_DIM) < (HEAD_DIM // 2)
    cosf, sins = cos_ref[...], sin_ref[...]
    half = HEAD_DIM // 2

    def rot(x):
        partner = jnp.where(first, pltpu.roll(x, C_WIDTH - half, 1), pltpu.roll(x, half, 1))
        return x * cosf + partner * sins

    def seg_mean(x):
        out = jnp.zeros_like(x)
        for h in range(C_HEADS):
            hm = hid == h
            s = jnp.sum(jnp.where(hm, x, 0.0), axis=1, keepdims=True) * (1.0 / HEAD_DIM)
            out = jnp.where(hm, s, out)
        return out

    for bb in range(bt):
        q = rot(q_ref[bb])
        k = rot(k_ref[bb]) * (HEAD_DIM ** -0.5)
        v = v_ref[bb]
        kb, vb = k.astype(BF16), v.astype(BF16)
        s_old = s_scr[bb]
        o = _dot(q.astype(BF16), s_old.astype(BF16)) * qd_ref[...]
        for h in range(C_HEADS):
            hm = hid == h
            att = _nt_dot(jnp.where(hm, q, 0.0).astype(BF16), kb) * dm_ref[h]
            o = o + jnp.where(hm, _dot(att.astype(BF16), vb), 0.0)
        kdt = jnp.transpose(k * kd_ref[...]).astype(BF16)
        s_scr[bb] = s_old * sd_ref[...] + bd_ref[...] * _dot(kdt, vb)
        mu = seg_mean(o)
        oc = o - mu
        var = seg_mean(oc * oc)
        normed = oc * lax.rsqrt(var + LN_EPS) * gg_ref[...] + gb_ref[...]
        o_ref[bb] = (_silu(g_ref[bb]) * normed).astype(BF16)

    @pl.when(ci == pl.num_programs(1) - 1)
    def _():
        sf_ref[...] = s_scr[...]


def _retention(h3, cosf, sins, gn_gain, gn_bias, s0e, c, bt):
    b, l, _ = h3.shape
    dmat, qd, kd, sd, bd = _ret_tables(c)
    blk = lambda col: pl.BlockSpec((bt, c, C_WIDTH), lambda i, j: (i, j, col // C_WIDTH))
    const2 = lambda a: pl.BlockSpec(a.shape, lambda i, j: (0, 0))
    pos_blk = pl.BlockSpec((c, C_WIDTH), lambda i, j: (j, 0))
    st_blk = pl.BlockSpec((bt, C_WIDTH, C_WIDTH), lambda i, j: (i, 0, 0))
    return pl.pallas_call(
        functools.partial(_ret_kernel, bt=bt, c=c),
        grid=(b // bt, l // c),
        in_specs=[blk(COL_CQ), blk(COL_CK), blk(COL_CV), blk(COL_CG), pos_blk, pos_blk,
                  const2(qd), const2(kd), pl.BlockSpec(dmat.shape, lambda i, j: (0, 0, 0)),
                  const2(sd), const2(bd), const2(gn_gain), const2(gn_bias), st_blk],
        out_specs=[pl.BlockSpec((bt, c, C_WIDTH), lambda i, j: (i, j, 0)), st_blk],
        out_shape=[jax.ShapeDtypeStruct((b, l, C_WIDTH), BF16),
                   jax.ShapeDtypeStruct((b, C_WIDTH, C_WIDTH), F32)],
        scratch_shapes=[pltpu.VMEM((bt, C_WIDTH, C_WIDTH), F32)],
        compiler_params=_cparams(("arbitrary", "arbitrary")),
        name="retention",
    )(h3, h3, h3, h3, cosf, sins, qd, kd, dmat, sd, bd, gn_gain, gn_bias, s0e)


def _state_embed(s):
    b = s.shape[0]
    eye = jnp.eye(C_HEADS, dtype=s.dtype)
    return (s[:, :, :, None, :] * eye[None, :, None, :, None]).reshape(b, C_WIDTH, C_WIDTH)


def _state_extract(se):
    b = se.shape[0]
    s5 = se.reshape(b, C_HEADS, HEAD_DIM, C_HEADS, HEAD_DIM)
    return jnp.stack([s5[:, h, :, h, :] for h in range(C_HEADS)], axis=1)


def _sortable(score):
    bits = pltpu.bitcast(score, I32)
    return bits ^ ((bits >> 31) & 0x7FFFFFFF)


def _dsa_kernel(iq_ref, iw_ref, ik_ref, q_ref, k_ref, vt_ref, tt_ref, o_ref,
                sc_scr, m_scr, l_scr, acc_scr, j_scr, *, topk):
    qb = pl.program_id(1)
    ntile = qb + 1
    row = lax.broadcasted_iota(I32, (Q_BLOCK, Q_BLOCK), 0)
    col = lax.broadcasted_iota(I32, (Q_BLOCK, Q_BLOCK), 1)
    qpos = qb * Q_BLOCK + col
    iq_all = iq_ref[0].reshape(IDX_HEADS * Q_BLOCK, IDX_DIM)
    iw = iw_ref[0]
    int_min = jnp.int32(INT_MIN)
    score_scale = IDX_HEADS ** -0.5

    def tile_rows(j):
        return pl.ds(pl.multiple_of(j * Q_BLOCK, Q_BLOCK), Q_BLOCK)

    def score_tile(j, carry):
        s = _nt_dot(ik_ref[0, tile_rows(j), :], iq_all)
        acc = jnp.zeros((Q_BLOCK, Q_BLOCK), F32)
        for h in range(IDX_HEADS):
            acc = acc + jnp.maximum(s[:, h * Q_BLOCK:(h + 1) * Q_BLOCK], 0.0) * iw[h:h + 1, :]
        score = (acc * (IDX_DIM ** -0.5)) * score_scale
        kint = jnp.where(j * Q_BLOCK + row <= qpos, _sortable(score), int_min)
        sc_scr[tile_rows(j), :] = kint
        return carry

    lax.fori_loop(0, ntile, score_tile, 0)

    def count(pred):
        def body(j, acc):
            m = jnp.where(pred(sc_scr[tile_rows(j), :], j), 1.0, 0.0)
            part = m[0:8]
            for i in range(1, Q_BLOCK // 8):
                part = part + m[i * 8:(i + 1) * 8]
            return acc + part
        acc = lax.fori_loop(0, ntile, body, jnp.zeros((8, Q_BLOCK), F32))
        return jnp.sum(acc, axis=0, keepdims=True)

    def bit_body(i, ans_u):
        cand_u = ans_u | lax.shift_left(jnp.int32(1), 31 - i)
        cand_s = cand_u ^ int_min
        cnt = count(lambda t, j: t >= cand_s)
        return jnp.where(cnt >= topk, cand_u, ans_u)

    thr = lax.fori_loop(0, 32, bit_body, jnp.zeros((1, Q_BLOCK), I32)) ^ int_min
    need = topk - count(lambda t, j: t > thr)
    n_ge = count(lambda t, j: t >= thr)

    j_scr[...] = jnp.full(j_scr.shape, 1 << 20, I32)
    has_tie = jnp.max(jnp.where((n_ge > topk) & (thr != int_min), 1.0, 0.0)) > 0.0

    @pl.when(has_tie)
    def _():
        def jbit(i, jc):
            cand = jc | lax.shift_left(jnp.int32(1), 12 - i)
            cnt = count(lambda t, j: (t == thr) & (j * Q_BLOCK + row < cand))
            return jnp.where(cnt <= need, cand, jc)
        jc = lax.fori_loop(0, 13, jbit, jnp.zeros((1, Q_BLOCK), I32))
        j_scr[...] = jnp.broadcast_to(jc, j_scr.shape)

    j_cut = j_scr[0:1, :]

    m_scr[...] = jnp.full(m_scr.shape, MASK_NEG, F32)
    l_scr[...] = jnp.zeros(l_scr.shape, F32)
    acc_scr[...] = jnp.zeros(acc_scr.shape, F32)

    def attend_tile(j, carry):
        t = sc_scr[tile_rows(j), :]
        kidx = j * Q_BLOCK + row
        sel = ((t > thr) | ((t == thr) & (kidx < j_cut))) & (kidx <= qpos)
        madd = jnp.where(sel, 0.0, MASK_NEG)
        off = jnp.minimum(qb - j, 2)
        for h in range(B_HEADS):
            s = _nt_dot(k_ref[0, h, tile_rows(j), :], q_ref[0, h])
            s = s * (HEAD_DIM ** -0.5) + tt_ref[h, off] + madd
            m_prev = m_scr[h:h + 1, :]
            m_new = jnp.maximum(m_prev, jnp.max(s, axis=0, keepdims=True))
            alpha = jnp.exp(m_prev - m_new)
            p = jnp.exp(s - m_new)
            l_scr[h:h + 1, :] = l_scr[h:h + 1, :] * alpha + jnp.sum(p, axis=0, keepdims=True)
            m_scr[h:h + 1, :] = m_new
            vt = vt_ref[0, h, :, tile_rows(j)]
            acc_scr[h] = acc_scr[h] * alpha + _dot(vt, p.astype(BF16))
        return carry

    lax.fori_loop(0, ntile, attend_tile, 0)
    for h in range(B_HEADS):
        o_ref[0, h * HEAD_DIM:(h + 1) * HEAD_DIM, :] = (acc_scr[h] / l_scr[h:h + 1, :]).astype(BF16)


def _dsa_prompt(iqh, iwt, ikb, qh, kh, vt, tt, topk):
    b, _, l, _ = qh.shape
    qblk = pl.BlockSpec((1, B_HEADS, Q_BLOCK, HEAD_DIM), lambda i, j: (i, 0, j, 0))
    return pl.pallas_call(
        functools.partial(_dsa_kernel, topk=topk),
        grid=(b, l // Q_BLOCK),
        in_specs=[qblk,
                  pl.BlockSpec((1, IDX_HEADS, Q_BLOCK), lambda i, j: (i, 0, j)),
                  pl.BlockSpec((1, l, IDX_DIM), lambda i, j: (i, 0, 0)),
                  qblk,
                  pl.BlockSpec((1, B_HEADS, l, HEAD_DIM), lambda i, j: (i, 0, 0, 0)),
                  pl.BlockSpec((1, B_HEADS, HEAD_DIM, l), lambda i, j: (i, 0, 0, 0)),
                  pl.BlockSpec(tt.shape, lambda i, j: (0, 0, 0, 0))],
        out_specs=pl.BlockSpec((1, B_WIDTH, Q_BLOCK), lambda i, j: (i, 0, j)),
        out_shape=jax.ShapeDtypeStruct((b, B_WIDTH, l), BF16),
        scratch_shapes=[pltpu.VMEM((l, Q_BLOCK), I32),
                        pltpu.VMEM((B_HEADS, Q_BLOCK), F32),
                        pltpu.VMEM((B_HEADS, Q_BLOCK), F32),
                        pltpu.VMEM((B_HEADS, HEAD_DIM, Q_BLOCK), F32),
                        pltpu.VMEM((8, Q_BLOCK), I32)],
        compiler_params=_cparams(("arbitrary", "arbitrary"), 56),
        name="dsa_prompt",
    )(iqh, iwt, ikb, qh, kh, vt, tt)


def _sidx_kernel(pt_ref, iq_ref, iw_ref, *rest, npages):
    pages, ikn_ref, o_ref = rest[:npages], rest[npages], rest[npages + 1]
    iq, iw = iq_ref[0], iw_ref[0]
    nq = iq.shape[0] // IDX_HEADS
    for j in range(npages + 1):
        keys = (pages[j][0] if j < npages else ikn_ref[0]).astype(BF16)
        r = jnp.maximum(_nt_dot(iq, keys), 0.0) * iw
        acc = r[0:nq]
        for h in range(1, IDX_HEADS):
            acc = acc + r[h * nq:(h + 1) * nq]
        o_ref[0, :, j * PAGE_SIZE:(j + 1) * PAGE_SIZE] = (acc * (IDX_DIM ** -0.5)) * (IDX_HEADS ** -0.5)


def _sample_scores(pt_flat, iqs, iws, cache_ik, ikn, npages):
    bd, hq, _ = iqs.shape
    nq = hq // IDX_HEADS
    width = (npages + 1) * PAGE_SIZE
    page_spec = lambda p: pl.BlockSpec((1, PAGE_SIZE, IDX_DIM), lambda i, pt, p=p: (pt[i * npages + p], 0, 0))
    per_b = lambda a: pl.BlockSpec((1,) + a.shape[1:], lambda i, pt: (i, 0, 0))
    return pl.pallas_call(
        functools.partial(_sidx_kernel, npages=npages),
        grid_spec=pltpu.PrefetchScalarGridSpec(
            num_scalar_prefetch=1, grid=(bd,),
            in_specs=[per_b(iqs), per_b(iws)] + [page_spec(p) for p in range(npages)] + [per_b(ikn)],
            out_specs=pl.BlockSpec((1, nq, width), lambda i, pt: (i, 0, 0))),
        out_shape=jax.ShapeDtypeStruct((bd, nq, width), F32),
        compiler_params=_cparams(("arbitrary",)),
        name="sample_scores",
    )(pt_flat, iqs, iws, *([cache_ik] * npages), ikn)


def _sthr_kernel(s_ref, o_ref, k_scr, *, topk, past, nq):
    tr, width = s_ref.shape
    colw = lax.broadcasted_iota(I32, (tr, width), 1)
    rowq = lax.broadcasted_iota(I32, (tr, width), 0) % nq
    vis = colw <= past + rowq
    int_min = jnp.int32(INT_MIN)
    k_scr[...] = jnp.where(vis, _sortable(s_ref[...]), int_min)

    def count(pred):
        return jnp.sum(jnp.where(pred(k_scr[...]), 1.0, 0.0), axis=1, keepdims=True)

    def bit_body(i, ans_u):
        cand_u = ans_u | lax.shift_left(jnp.int32(1), 31 - i)
        cand_s = cand_u ^ int_min
        return jnp.where(count(lambda t: t >= cand_s) >= topk, cand_u, ans_u)

    thr = lax.fori_loop(0, 32, bit_body, jnp.zeros((tr, 1), I32)) ^ int_min
    need = topk - count(lambda t: t > thr)

    def jbit(i, jc):
        cand = jc | lax.shift_left(jnp.int32(1), 12 - i)
        cnt = count(lambda t: (t == thr) & (colw < cand))
        return jnp.where(cnt <= need, cand, jc)

    j_cut = lax.fori_loop(0, 13, jbit, jnp.zeros((tr, 1), I32))
    t = k_scr[...]
    sel = ((t > thr) | ((t == thr) & (colw < j_cut))) & vis
    o_ref[...] = jnp.where(sel, 0.0, MASK_NEG)


def _sample_mask(scores2, topk, past, nq):
    r, width = scores2.shape
    tr = min(r, 128)
    return pl.pallas_call(
        functools.partial(_sthr_kernel, topk=topk, past=past, nq=nq),
        grid=(r // tr,),
        in_specs=[pl.BlockSpec((tr, width), lambda i: (i, 0))],
        out_specs=pl.BlockSpec((tr, width), lambda i: (i, 0)),
        out_shape=jax.ShapeDtypeStruct((r, width), F32),
        scratch_shapes=[pltpu.VMEM((tr, width), I32)],
        compiler_params=_cparams(("arbitrary",)),
        name="sample_mask",
    )(scores2)


def _sattn_kernel(pt_ref, q_ref, m_ref, bfar_ref, bnear_ref, *rest, npages, nq):
    kpages, vpages = rest[:npages], rest[npages:2 * npages]
    kn_ref, vn_ref, o_ref, s_scr = rest[2 * npages:2 * npages + 4]
    q = q_ref[0]
    mx = jnp.full((B_HEADS * nq, PAGE_SIZE), MASK_NEG, F32)
    for j in range(npages + 1):
        kp = (kpages[j][0] if j < npages else kn_ref[0]).astype(BF16)
        cols = slice(j * PAGE_SIZE, (j + 1) * PAGE_SIZE)
        bias = bfar_ref[...] if j < npages - 1 else bnear_ref[:, (j - npages + 1) * PAGE_SIZE:(j - npages + 2) * PAGE_SIZE]
        madd = jnp.concatenate([m_ref[0, :, cols]] * B_HEADS, axis=0)
        s = _nt_dot(q, kp) * (HEAD_DIM ** -0.5) + bias + madd
        s_scr[:, cols] = s
        mx = jnp.maximum(mx, s)
    mrow = jnp.max(mx, axis=1, keepdims=True)
    lsum = jnp.zeros((B_HEADS * nq, PAGE_SIZE), F32)
    acc = jnp.zeros((B_HEADS * nq, B_WIDTH), F32)
    for j in range(npages + 1):
        vp = (vpages[j][0] if j < npages else vn_ref[0]).astype(BF16)
        p = jnp.exp(s_scr[:, j * PAGE_SIZE:(j + 1) * PAGE_SIZE] - mrow)
        lsum = lsum + p
        acc = acc + _dot(p.astype(BF16), vp)
    acc = acc / jnp.sum(lsum, axis=1, keepdims=True)
    head = lax.broadcasted_iota(I32, (nq, B_WIDTH), 1) // HEAD_DIM
    out = jnp.zeros((nq, B_WIDTH), F32)
    for h in range(B_HEADS):
        out = jnp.where(head == h, acc[h * nq:(h + 1) * nq], out)
    o_ref[0] = out.astype(BF16)


def _sample_attend(pt_flat, qbd, madd3, bfar, bnear, cache_k, cache_v, kn, vn, npages):
    bd, hq, _ = qbd.shape
    nq = hq // B_HEADS
    width = (npages + 1) * PAGE_SIZE
    page_spec = lambda p: pl.BlockSpec((1, PAGE_SIZE, B_WIDTH), lambda i, pt, p=p: (pt[i * npages + p], 0, 0))
    per_b = lambda a: pl.BlockSpec((1,) + a.shape[1:], lambda i, pt: (i, 0, 0))
    const2 = lambda a: pl.BlockSpec(a.shape, lambda i, pt: (0, 0))
    pages = [page_spec(p) for p in range(npages)]
    return pl.pallas_call(
        functools.partial(_sattn_kernel, npages=npages, nq=nq),
        grid_spec=pltpu.PrefetchScalarGridSpec(
            num_scalar_prefetch=1, grid=(bd,),
            in_specs=[per_b(qbd), per_b(madd3), const2(bfar), const2(bnear)] + pages + pages + [per_b(kn), per_b(vn)],
            out_specs=pl.BlockSpec((1, nq, B_WIDTH), lambda i, pt: (i, 0, 0)),
            scratch_shapes=[pltpu.VMEM((hq, width), F32)]),
        out_shape=jax.ShapeDtypeStruct((bd, nq, B_WIDTH), BF16),
        compiler_params=_cparams(("arbitrary",), 48),
        name="sample_attend",
    )(pt_flat, qbd, madd3, bfar, bnear, *([cache_k] * npages), *([cache_v] * npages), kn, vn)


def _outproj_kernel(a_ref, b_ref, c_ref, wa_ref, wb_ref, wc_ref, x_ref, g_ref, bb_ref, o_ref):
    mix = _dot(a_ref[...], wa_ref[...]) + _dot(b_ref[...], wb_ref[...]) + _dot(c_ref[...], wc_ref[...])
    o_ref[...] = _layer_norm_rows(ALPHA * x_ref[...] + mix, g_ref[...], bb_ref[...])


def _outproj_ln(a, b, c, wa, wb, wc, x2, gain, bias):
    n = x2.shape[0]
    tm = 512
    rows = lambda a_: pl.BlockSpec((tm, a_.shape[1]), lambda i: (i, 0))
    full2 = lambda a_: pl.BlockSpec(a_.shape, lambda i: (0, 0))
    return pl.pallas_call(
        _outproj_kernel,
        grid=(n // tm,),
        in_specs=[rows(a), rows(b), rows(c), full2(wa), full2(wb), full2(wc), rows(x2), full2(gain), full2(bias)],
        out_specs=pl.BlockSpec((tm, D_MODEL), lambda i: (i, 0)),
        out_shape=jax.ShapeDtypeStruct((n, D_MODEL), F32),
        compiler_params=_cparams(("arbitrary",), 40),
        name="outproj_ln1",
    )(a, b, c, wa, wb, wc, x2, gain, bias)


def _router_kernel(x_ref, wh_ref, wl_ref, b_ref, o_ref):
    x = x_ref[...]
    xh = x.astype(BF16)
    xl = (x - xh.astype(F32)).astype(BF16)
    logits = _dot(xh, wh_ref[...]) + _dot(xh, wl_ref[...]) + _dot(xl, wh_ref[...]) + b_ref[...]
    lane = lax.broadcasted_iota(I32, logits.shape, 1).astype(F32)
    neg = -jnp.inf
    first_lane = lambda m: jnp.min(jnp.where(m, lane, float(LANES)), axis=1, keepdims=True)
    is_g = (lane >= N_EXPERTS) & (lane < N_EXPERTS + N_GROUPS)
    gmax = jnp.max(jnp.where(is_g, logits, neg), axis=1, keepdims=True)
    g_sel = first_lane(is_g & (logits == gmax)) - N_EXPERTS
    p_group = 1.0 / jnp.sum(jnp.where(is_g, jnp.exp(logits - gmax), 0.0), axis=1, keepdims=True)
    in_g = jnp.floor(lane * (1.0 / EXPERTS_PER_GROUP)) == g_sel
    ev = jnp.where(in_g, logits, neg)
    v1 = jnp.max(ev, axis=1, keepdims=True)
    i1 = first_lane(in_g & (logits == v1))
    ev2 = jnp.where(lane == i1, neg, ev)
    v2 = jnp.max(ev2, axis=1, keepdims=True)
    i2 = first_lane(in_g & (lane != i1) & (logits == v2))
    e2 = jnp.exp(v2 - v1)
    den = 1.0 + e2
    g1 = (1.0 / den) * p_group
    g2 = (e2 / den) * p_group
    o_ref[...] = jnp.where(lane == i1, g1, jnp.where(lane == i2, g2, 0.0))


def _router(x2, wh, wl, bias):
    n = x2.shape[0]
    tm = 512
    full2 = lambda a_: pl.BlockSpec(a_.shape, lambda i: (0, 0))
    return pl.pallas_call(
        _router_kernel,
        grid=(n // tm,),
        in_specs=[pl.BlockSpec((tm, D_MODEL), lambda i: (i, 0)), full2(wh), full2(wl), full2(bias)],
        out_specs=pl.BlockSpec((tm, LANES), lambda i: (i, 0)),
        out_shape=jax.ShapeDtypeStruct((n, LANES), F32),
        compiler_params=_cparams(("arbitrary",)),
        name="router",
    )(x2, wh, wl, bias)


def _moe_kernel(x_ref, cmb_ref, wgu_ref, wd_ref, g_ref, b_ref, o_ref, xb_scr, acc_scr):
    e = pl.program_id(1)

    @pl.when(e == 0)
    def _():
        xb_scr[...] = x_ref[...].astype(BF16)
        acc_scr[...] = jnp.zeros(acc_scr.shape, F32)

    gu = _dot(xb_scr[...], wgu_ref[0])
    cmb = cmb_ref[...]
    lane = lax.broadcasted_iota(I32, cmb.shape, 1)
    ce = jnp.sum(jnp.where(lane == e, cmb, 0.0), axis=1, keepdims=True)
    hid = _silu(gu[:, :EXPERT_FF]) * gu[:, EXPERT_FF:] * ce
    acc_scr[...] += _dot(hid.astype(BF16), wd_ref[0])

    @pl.when(e == pl.num_programs(1) - 1)
    def _():
        o_ref[...] = _layer_norm_rows(ALPHA * x_ref[...] + acc_scr[...], g_ref[...], b_ref[...])


def _moe_ln(x2, cmb, wgu, wd, gain, bias):
    n = x2.shape[0]
    tm = 1024 if n > 1024 else 512
    full2 = lambda a_: pl.BlockSpec(a_.shape, lambda i, e: (0, 0))
    return pl.pallas_call(
        _moe_kernel,
        grid=(n // tm, N_EXPERTS),
        in_specs=[pl.BlockSpec((tm, D_MODEL), lambda i, e: (i, 0)),
                  pl.BlockSpec((tm, LANES), lambda i, e: (i, 0)),
                  pl.BlockSpec((1, D_MODEL, 2 * EXPERT_FF), lambda i, e: (e, 0, 0)),
                  pl.BlockSpec((1, EXPERT_FF, D_MODEL), lambda i, e: (e, 0, 0)),
                  full2(gain), full2(bias)],
        out_specs=pl.BlockSpec((tm, D_MODEL), lambda i, e: (i, 0)),
        out_shape=jax.ShapeDtypeStruct((n, D_MODEL), F32),
        scratch_shapes=[pltpu.VMEM((tm, D_MODEL), BF16), pltpu.VMEM((tm, D_MODEL), F32)],
        compiler_params=_cparams(("arbitrary", "arbitrary"), 48),
        name="moe_ln2",
    )(x2, cmb, wgu, wd, gain, bias)


def _layer_weights(l, w_in, w_out, gv_gain, gv_bias, ws, bs, ret_gain, ret_bias, ln1_g, ln1_b, ln2_g, ln2_b,
                   rg_w, rg_b, re_w, re_b, e_gate, e_up, e_down):
    w = w_in[l]
    z = lambda k: jnp.zeros((D_MODEL, k), w.dtype)
    ik_end = 2560 + IDX_DIM
    iw_end = ik_end + IDX_HEADS
    w_pad = jnp.concatenate([w[:, :ik_end], z(COL_IW - COL_IK - IDX_DIM), w[:, ik_end:iw_end],
                             z(COL_CQ - COL_IW - IDX_HEADS), w[:, iw_end:]], axis=1).astype(BF16)
    wo = w_out[l].astype(BF16)
    wr = jnp.concatenate([re_w[l], rg_w[l], jnp.zeros((D_MODEL, LANES - N_EXPERTS - N_GROUPS), F32)], axis=1)
    wrh = wr.astype(BF16)
    wrl = (wr - wrh.astype(F32)).astype(BF16)
    br = jnp.concatenate([re_b[l], rg_b[l], jnp.zeros((LANES - N_EXPERTS - N_GROUPS,), F32)])[None, :]
    row = lambda a: a[l].reshape(1, -1)
    return dict(
        w_pad=w_pad, wa=wo[:A_WIDTH], wb=wo[A_WIDTH:A_WIDTH + B_WIDTH], wc=wo[A_WIDTH + B_WIDTH:],
        gv_gain=row(gv_gain), gv_bias=row(gv_bias), ws=ws[l], bs=bs[l],
        ret_gain=row(ret_gain), ret_bias=row(ret_bias),
        ln1_g=row(ln1_g), ln1_b=row(ln1_b), ln2_g=row(ln2_g), ln2_b=row(ln2_b),
        wrh=wrh, wrl=wrl, br=br,
        wgu=jnp.concatenate([e_gate[l], e_up[l]], axis=2).astype(BF16), wd=e_down[l].astype(BF16))


def _channel_mix(x2, a_out, b_out, c_out, lw):
    x1 = _outproj_ln(a_out, b_out, c_out, lw["wa"], lw["wb"], lw["wc"], x2, lw["ln1_g"], lw["ln1_b"])
    cmb = _router(x1, lw["wrh"], lw["wrl"], lw["br"])
    return _moe_ln(x1, cmb, lw["wgu"], lw["wd"], lw["ln2_g"], lw["ln2_b"])


def _heads(t2, b, l, nh):
    return t2.reshape(b, l, nh, t2.shape[-1] // nh)


def _prompt_layer(x3, lw, tt, cosf, sins):
    b, l, _ = x3.shape
    x2 = x3.reshape(b * l, D_MODEL)
    h2 = _proj(x2, lw["w_pad"])
    col = lambda c0, wdt: h2[:, c0:c0 + wdt]
    bsb = jnp.repeat(lw["bs"].T, HEAD_DIM, axis=1)
    a_out, a_vn = _gmlp(h2, lw["gv_gain"], lw["gv_bias"], lw["ws"], bsb)
    k4 = _heads(col(COL_K, B_WIDTH), b, l, B_HEADS)
    v4 = _heads(col(COL_V, B_WIDTH), b, l, B_HEADS)
    ik3 = col(COL_IK, IDX_DIM).reshape(b, l, IDX_DIM)
    hm = lambda t4: jnp.transpose(t4, (0, 2, 1, 3)).astype(BF16)
    qh = hm(_heads(col(COL_Q, B_WIDTH), b, l, B_HEADS))
    iqh = hm(_heads(col(COL_IQ, IDX_HEADS * IDX_DIM), b, l, IDX_HEADS))
    iwt = jnp.transpose(col(COL_IW, IDX_HEADS).reshape(b, l, IDX_HEADS), (0, 2, 1))
    vt = jnp.transpose(v4, (0, 2, 3, 1)).astype(BF16)
    topk = min(TOPK_MAX, l // 4)
    bt_out = _dsa_prompt(iqh, iwt, ik3.astype(BF16), qh, hm(k4), vt, tt, topk)
    b_out = jnp.transpose(bt_out, (0, 2, 1)).reshape(b * l, B_WIDTH)
    s0e = jnp.zeros((b, C_WIDTH, C_WIDTH), F32)
    c_out, s_fin = _retention(h2.reshape(b, l, H_WIDTH), cosf, sins, lw["ret_gain"], lw["ret_bias"], s0e,
                              RET_CHUNK, 1)
    y = _channel_mix(x2, a_out, b_out, c_out.reshape(b * l, C_WIDTH), lw)
    last = ((l - 1) // CHUNK) * CHUNK
    state = (k4, v4, ik3, _state_extract(s_fin), a_vn.reshape(b, l, A_WIDTH)[:, last:])
    return y.reshape(b, l, D_MODEL), state


def _sample_layer(x3, lw, cache_k_l, cache_v_l, cache_ik_l, state_l, pt_flat, npages, bfar, bnear, cosf, sins):
    bd, t, _ = x3.shape
    n = bd * t
    past = npages * PAGE_SIZE
    x2 = x3.reshape(n, D_MODEL)
    h2 = _proj(x2, lw["w_pad"])
    col = lambda c0, wdt: h2[:, c0:c0 + wdt]
    rep = CHUNK // t
    eye = jnp.eye(rep, dtype=F32)
    ws_t = lw["ws"][:, :t, :t]
    ws_bd = (eye[None, :, None, :, None] * ws_t[:, None, :, None, :]).reshape(A_GROUPS, CHUNK, CHUNK)
    bsb = jnp.repeat(jnp.tile(lw["bs"][:, :t], (1, rep)).T, HEAD_DIM, axis=1)
    a_out, a_vn = _gmlp(h2, lw["gv_gain"], lw["gv_bias"], ws_bd, bsb)
    k4 = _heads(col(COL_K, B_WIDTH), bd, t, B_HEADS)
    v4 = _heads(col(COL_V, B_WIDTH), bd, t, B_HEADS)
    ik3 = col(COL_IK, IDX_DIM).reshape(bd, t, IDX_DIM)
    pad_rows = lambda a3: jnp.pad(a3, ((0, 0), (0, PAGE_SIZE - t), (0, 0)))
    iq4 = _heads(col(COL_IQ, IDX_HEADS * IDX_DIM), bd, t, IDX_HEADS)
    iqs = jnp.transpose(iq4, (0, 2, 1, 3)).reshape(bd, IDX_HEADS * t, IDX_DIM).astype(BF16)
    iw3 = jnp.transpose(col(COL_IW, IDX_HEADS).reshape(bd, t, IDX_HEADS), (0, 2, 1))
    iws = jnp.broadcast_to(iw3.reshape(bd, IDX_HEADS * t, 1), (bd, IDX_HEADS * t, LANES))
    scores = _sample_scores(pt_flat, iqs, iws, cache_ik_l, pad_rows(ik3), npages)
    width = (npages + 1) * PAGE_SIZE
    topk = min(TOPK_MAX, (past + t) // 4)
    madd = _sample_mask(scores.reshape(n, width), topk, past, t).reshape(bd, t, width)
    q4 = _heads(col(COL_Q, B_WIDTH), bd, t, B_HEADS)
    eye_h = jnp.eye(B_HEADS, dtype=F32)
    qbd = (jnp.transpose(q4, (0, 2, 1, 3))[:, :, :, None, :] * eye_h[None, :, None, :, None])
    qbd = qbd.reshape(bd, B_HEADS * t, B_WIDTH).astype(BF16)
    b_out = _sample_attend(pt_flat, qbd, madd, bfar, bnear,
                           cache_k_l.reshape(-1, PAGE_SIZE, B_WIDTH), cache_v_l.reshape(-1, PAGE_SIZE, B_WIDTH),
                           pad_rows(k4.reshape(bd, t, B_WIDTH)), pad_rows(v4.reshape(bd, t, B_WIDTH)), npages)
    c_out, s_fin = _retention(h2.reshape(bd, t, H_WIDTH), cosf, sins, lw["ret_gain"], lw["ret_bias"],
                              _state_embed(state_l), t, 8)
    y = _channel_mix(x2, a_out, b_out.reshape(n, B_WIDTH), c_out.reshape(n, C_WIDTH), lw)
    state = (k4, v4, ik3, _state_extract(s_fin), a_vn.reshape(bd, t, A_WIDTH))
    return y.reshape(bd, t, D_MODEL), state


def _distance_tables(rel_bias, t, past):
    r = jnp.arange(Q_BLOCK, dtype=I32)
    d_prompt = jnp.concatenate([off * Q_BLOCK + r[None, :] - r[:, None] for off in range(3)], axis=0)
    qpos = past + jnp.arange(t, dtype=I32)
    near0 = past - PAGE_SIZE
    d_near = [qpos[:, None] - (near0 + half * PAGE_SIZE + r[None, :]) for half in range(2)]
    d_far = jnp.full((t, LANES), MAX_DISTANCE * 2, I32)
    tables = _bias_tables(jnp.concatenate([d_prompt] + d_near + [d_far], axis=0), rel_bias)
    n0 = 3 * Q_BLOCK
    tt = tables[:, :n0].reshape(B_HEADS, 3, Q_BLOCK, Q_BLOCK)
    bnear = jnp.concatenate([tables[:, n0:n0 + t], tables[:, n0 + t:n0 + 2 * t]], axis=2).reshape(B_HEADS * t, 2 * LANES)
    bfar = tables[:, n0 + 2 * t:n0 + 3 * t].reshape(B_HEADS * t, LANES)
    return tt, bnear, bfar


def kernel(x_prompt, x_sample, cache_k, cache_v, cache_idx_k, state_ret, page_table, w_in, w_out, gmlp_v_gain,
           gmlp_v_bias, gmlp_ws, gmlp_bs, rel_bias, ret_gn_gain, ret_gn_bias, ln1_gain, ln1_bias, ln2_gain, ln2_bias,
           router_group_w, router_group_b, router_expert_w, router_expert_b, expert_w_gate, expert_w_up,
           expert_w_down):
    depth = w_in.shape[0]
    seq = x_prompt.shape[1]
    bd, t, _ = x_sample.shape
    npages = page_table.shape[1]
    past = npages * PAGE_SIZE
    pt_flat = page_table.reshape(-1).astype(I32)
    tt, bnear, bfar = _distance_tables(rel_bias, t, past)
    cos_p, sin_p = _rope_tables(jnp.arange(seq, dtype=I32))
    cos_s, sin_s = _rope_tables(past + jnp.arange(t, dtype=I32))
    xp, xs = x_prompt, x_sample
    st_p, st_s = [], []
    for l in range(depth):
        lw = _layer_weights(l, w_in, w_out, gmlp_v_gain, gmlp_v_bias, gmlp_ws, gmlp_bs, ret_gn_gain, ret_gn_bias,
                            ln1_gain, ln1_bias, ln2_gain, ln2_bias, router_group_w, router_group_b,
                            router_expert_w, router_expert_b, expert_w_gate, expert_w_up, expert_w_down)
        xp, sp = _prompt_layer(xp, lw, tt, cos_p, sin_p)
        xs, ss = _sample_layer(xs, lw, cache_k[l], cache_v[l], cache_idx_k[l], state_ret[l], pt_flat, npages,
                               bfar, bnear, cos_s, sin_s)
        st_p.append(sp)
        st_s.append(ss)
    stk = lambda sts, i: jnp.stack([s[i] for s in sts], axis=0)
    return (xp, xs,
            stk(st_p, 0), stk(st_p, 1), stk(st_p, 2), stk(st_p, 3), stk(st_p, 4),
            stk(st_s, 0), stk(st_s, 1), stk(st_s, 2), stk(st_s, 3), stk(st_s, 4))
```

```python
import functools
import math

import jax
import jax.numpy as jnp
from jax import lax
from jax.experimental import pallas as pl
from jax.experimental.pallas import tpu as pltpu

D_MODEL = 1024
HEAD_DIM = 64
A_GROUPS = 4
A_WIDTH = 256
CHUNK = 128
B_HEADS = 8
B_WIDTH = 512
IDX_HEADS = 8
IDX_DIM = 64
TOPK_MAX = 256
Q_BLOCK = 128
KEY_TILE = 256
NUM_BUCKETS = 32
MAX_DISTANCE = 128
C_HEADS = 4
C_WIDTH = 256
RET_CHUNK = 128
ROPE_BASE = 10000.0
PAGE_SIZE = 128
N_GROUPS = 4
EXPERTS_PER_GROUP = 8
N_EXPERTS = 32
EXPERT_FF = 256
DEPTH = 2
ALPHA = (2 * DEPTH) ** 0.25
LN_EPS = 1e-5

F32 = jnp.float32
BF16 = jnp.bfloat16
I32 = jnp.int32
LANES = 128
MASK_NEG = -1e30
INT_MIN = -2 ** 31

COL_AU, COL_AV, COL_Q, COL_K, COL_V, COL_IQ, COL_IK, COL_IW = 0, 256, 512, 1024, 1536, 2048, 2560, 2688
COL_CQ, COL_CK, COL_CV, COL_CG = 2816, 3072, 3328, 3584
H_WIDTH = 3840


def _cparams(sem, vmem_mb=None):
    kw = dict(dimension_semantics=sem)
    if vmem_mb is not None:
        kw["vmem_limit_bytes"] = vmem_mb << 20
    return pltpu.CompilerParams(**kw)


def _nt_dot(a, b):
    return lax.dot_general(a, b, (((1,), (1,)), ((), ())), preferred_element_type=F32)


def _dot(a, b):
    return jnp.dot(a, b, preferred_element_type=F32)


def _layer_norm_rows(x, gain, bias):
    mu = jnp.mean(x, axis=-1, keepdims=True)
    xc = x - mu
    var = jnp.mean(xc * xc, axis=-1, keepdims=True)
    return xc * lax.rsqrt(var + LN_EPS) * gain + bias


def _silu(x):
    return x * (1.0 / (1.0 + jnp.exp(-x)))


def _proj_kernel(x_ref, w_ref, o_ref):
    o_ref[...] = _dot(x_ref[...].astype(BF16), w_ref[...])


def _proj(x2, w_pad):
    n = x2.shape[0]
    tm = 1024 if n > 1024 else 512
    tn = 768
    return pl.pallas_call(
        _proj_kernel,
        grid=(n // tm, H_WIDTH // tn),
        in_specs=[pl.BlockSpec((tm, D_MODEL), lambda i, j: (i, 0)),
                  pl.BlockSpec((D_MODEL, tn), lambda i, j: (0, j))],
        out_specs=pl.BlockSpec((tm, tn), lambda i, j: (i, j)),
        out_shape=jax.ShapeDtypeStruct((n, H_WIDTH), F32),
        compiler_params=_cparams(("arbitrary", "arbitrary"), 40),
        name="in_proj",
    )(x2, w_pad)


def _bias_kernel(rb_ref, d_ref, o_ref):
    n = jnp.maximum(d_ref[...], 0)
    max_exact = NUM_BUCKETS // 2
    nf = jnp.maximum(n, 1).astype(F32)
    large = max_exact + (jnp.log(nf / max_exact) / math.log(MAX_DISTANCE / max_exact)
                         * (NUM_BUCKETS - max_exact)).astype(I32)
    large = jnp.minimum(large, NUM_BUCKETS - 1)
    bucket = jnp.where(n < max_exact, n, large)
    for h in range(B_HEADS):
        acc = jnp.zeros(bucket.shape, F32)
        for bk in range(NUM_BUCKETS):
            acc = jnp.where(bucket == bk, rb_ref[bk * B_HEADS + h], acc)
        o_ref[h] = acc


def _bias_tables(dist, rel_bias):
    r = dist.shape[0]
    return pl.pallas_call(
        _bias_kernel,
        in_specs=[pl.BlockSpec(memory_space=pltpu.SMEM),
                  pl.BlockSpec((r, LANES), lambda: (0, 0))],
        out_specs=pl.BlockSpec((B_HEADS, r, LANES), lambda: (0, 0, 0)),
        out_shape=jax.ShapeDtypeStruct((B_HEADS, r, LANES), F32),
        name="bias_tables",
    )(rel_bias.reshape(-1), dist)


def _gmlp_kernel(u_ref, v_ref, g_ref, b_ref, ws_ref, bsb_ref, o_ref, vn_ref, *, nchunk):
    r = lax.broadcasted_iota(I32, (CHUNK, CHUNK), 0)
    c = lax.broadcasted_iota(I32, (CHUNK, CHUNK), 1)
    grp = lax.broadcasted_iota(I32, (CHUNK, A_WIDTH), 1) // HEAD_DIM
    wts = [jnp.where(r >= c, ws_ref[g], 0.0).astype(BF16) for g in range(A_GROUPS)]
    gain, bias, bsb = g_ref[...], b_ref[...], bsb_ref[...]
    for ci in range(nchunk):
        sl = pl.ds(ci * CHUNK, CHUNK)
        vn = _layer_norm_rows(v_ref[sl, :], gain, bias)
        vn_ref[sl, :] = vn
        vb = vn.astype(BF16)
        mixed = bsb
        for g in range(A_GROUPS):
            mixed = mixed + jnp.where(grp == g, _dot(wts[g], vb), 0.0)
        o_ref[sl, :] = (u_ref[sl, :] * mixed).astype(BF16)


def _gmlp(h2, gain, bias, ws, bsb):
    n = h2.shape[0]
    tm = min(n, 1024)
    blk = lambda col: pl.BlockSpec((tm, A_WIDTH), lambda i: (i, col // A_WIDTH))
    full2 = lambda a: pl.BlockSpec(a.shape, lambda i: (0, 0))
    return pl.pallas_call(
        functools.partial(_gmlp_kernel, nchunk=tm // CHUNK),
        grid=(n // tm,),
        in_specs=[blk(COL_AU), blk(COL_AV), full2(gain), full2(bias),
                  pl.BlockSpec(ws.shape, lambda i: (0, 0, 0)), full2(bsb)],
        out_specs=[pl.BlockSpec((tm, A_WIDTH), lambda i: (i, 0)),
                   pl.BlockSpec((tm, A_WIDTH), lambda i: (i, 0))],
        out_shape=[jax.ShapeDtypeStruct((n, A_WIDTH), BF16),
                   jax.ShapeDtypeStruct((n, A_WIDTH), F32)],
        compiler_params=_cparams(("arbitrary",)),
        name="gmlp",
    )(h2, h2, gain, bias, ws, bsb)


def _ret_tables(c):
    log_g = jnp.log(1.0 - 2.0 ** (-5.0 - jnp.arange(C_HEADS, dtype=F32)))
    i = jnp.arange(c, dtype=F32)
    diff = i[:, None] - i[None, :]
    dmat = jnp.where(diff >= 0, jnp.exp(log_g[:, None, None] * jnp.maximum(diff, 0.0)), 0.0)
    q_dec = jnp.exp(log_g[:, None] * (i + 1.0))
    k_dec = jnp.exp(log_g[:, None] * (c - 1.0 - i))
    s_dec = jnp.exp(log_g * c)
    qd = jnp.repeat(q_dec.T, HEAD_DIM, axis=1)
    kd = jnp.repeat(k_dec.T, HEAD_DIM, axis=1)
    hid = jnp.arange(C_WIDTH) // HEAD_DIM
    same = hid[:, None] == hid[None, :]
    sd = jnp.where(same, s_dec[hid][:, None], 0.0)
    return dmat, qd, kd, sd, same.astype(F32)


def _rope_tables(pos):
    half = HEAD_DIM // 2
    inv = ROPE_BASE ** (-jnp.arange(half, dtype=F32) / half)
    ang = pos.astype(F32)[:, None] * inv[None, :]
    cos, sin = jnp.cos(ang), jnp.sin(ang)
    cosf = jnp.tile(jnp.concatenate([cos, cos], axis=1), (1, C_HEADS))
    sins = jnp.tile(jnp.concatenate([-sin, sin], axis=1), (1, C_HEADS))
    return cosf, sins


def _ret_kernel(q_ref, k_ref, v_ref, g_ref, cos_ref, sin_ref, qd_ref, kd_ref, dm_ref, sd_ref, bd_ref,
                gg_ref, gb_ref, s0_ref, o_ref, sf_ref, s_scr, *, bt, c):
    ci = pl.program_id(1)

    @pl.when(ci == 0)
    def _():
        s_scr[...] = s0_ref[...]

    lane = lax.broadcasted_iota(I32, (c, C_WIDTH), 1)
    hid = lane // HEAD_DIM
    first = (lane % HEAD_DIM) < (HEAD_DIM // 2)
    cosf, sins = cos_ref[...], sin_ref[...]
    half = HEAD_DIM // 2

    def rot(x):
        partner = jnp.where(first, pltpu.roll(x, C_WIDTH - half, 1), pltpu.roll(x, half, 1))
        return x * cosf + partner * sins

    def seg_mean(x):
        out = jnp.zeros_like(x)
        for h in range(C_HEADS):
            hm = hid == h
            s = jnp.sum(jnp.where(hm, x, 0.0), axis=1, keepdims=True) * (1.0 / HEAD_DIM)
            out = jnp.where(hm, s, out)
        return out

    for bb in range(bt):
        q = rot(q_ref[bb])
        k = rot(k_ref[bb]) * (HEAD_DIM ** -0.5)
        v = v_ref[bb]
        kb, vb = k.astype(BF16), v.astype(BF16)
        s_old = s_scr[bb]
        o = _dot(q.astype(BF16), s_old.astype(BF16)) * qd_ref[...]
        for h in range(C_HEADS):
            hm = hid == h
            att = _nt_dot(jnp.where(hm, q, 0.0).astype(BF16), kb) * dm_ref[h]
            o = o + jnp.where(hm, _dot(att.astype(BF16), vb), 0.0)
        kdt = jnp.transpose(k * kd_ref[...]).astype(BF16)
        s_scr[bb] = s_old * sd_ref[...] + bd_ref[...] * _dot(kdt, vb)
        mu = seg_mean(o)
        oc = o - mu
        var = seg_mean(oc * oc)
        normed = oc * lax.rsqrt(var + LN_EPS) * gg_ref[...] + gb_ref[...]
        o_ref[bb] = (_silu(g_ref[bb]) * normed).astype(BF16)

    @pl.when(ci == pl.num_programs(1) - 1)
    def _():
        sf_ref[...] = s_scr[...]


def _retention(h3, cosf, sins, gn_gain, gn_bias, s0e, c, bt):
    b, l, _ = h3.shape
    dmat, qd, kd, sd, bd = _ret_tables(c)
    blk = lambda col: pl.BlockSpec((bt, c, C_WIDTH), lambda i, j: (i, j, col // C_WIDTH))
    const2 = lambda a: pl.BlockSpec(a.shape, lambda i, j: (0, 0))
    pos_blk = pl.BlockSpec((c, C_WIDTH), lambda i, j: (j, 0))
    st_blk = pl.BlockSpec((bt, C_WIDTH, C_WIDTH), lambda i, j: (i, 0, 0))
    return pl.pallas_call(
        functools.partial(_ret_kernel, bt=bt, c=c),
        grid=(b // bt, l // c),
        in_specs=[blk(COL_CQ), blk(COL_CK), blk(COL_CV), blk(COL_CG), pos_blk, pos_blk,
                  const2(qd), const2(kd), pl.BlockSpec(dmat.shape, lambda i, j: (0, 0, 0)),
                  const2(sd), const2(bd), const2(gn_gain), const2(gn_bias), st_blk],
        out_specs=[pl.BlockSpec((bt, c, C_WIDTH), lambda i, j: (i, j, 0)), st_blk],
        out_shape=[jax.ShapeDtypeStruct((b, l, C_WIDTH), BF16),
                   jax.ShapeDtypeStruct((b, C_WIDTH, C_WIDTH), F32)],
        scratch_shapes=[pltpu.VMEM((bt, C_WIDTH, C_WIDTH), F32)],
        compiler_params=_cparams(("arbitrary", "arbitrary")),
        name="retention",
    )(h3, h3, h3, h3, cosf, sins, qd, kd, dmat, sd, bd, gn_gain, gn_bias, s0e)


def _state_embed(s):
    b = s.shape[0]
    eye = jnp.eye(C_HEADS, dtype=s.dtype)
    return (s[:, :, :, None, :] * eye[None, :, None, :, None]).reshape(b, C_WIDTH, C_WIDTH)


def _state_extract(se):
    b = se.shape[0]
    s5 = se.reshape(b, C_HEADS, HEAD_DIM, C_HEADS, HEAD_DIM)
    return jnp.stack([s5[:, h, :, h, :] for h in range(C_HEADS)], axis=1)


def _sortable(score):
    bits = pltpu.bitcast(score, I32)
    return bits ^ ((bits >> 31) & 0x7FFFFFFF)


def _dsa_kernel(iq_ref, iw_ref, ik_ref, q_ref, k_ref, vt_ref, tt_ref, o_ref,
                sc_scr, s_scr, acc_scr, j_scr, *, topk):
    qb = pl.program_id(1)
    ntile = (qb + KEY_TILE // Q_BLOCK) // (KEY_TILE // Q_BLOCK)
    row = lax.broadcasted_iota(I32, (KEY_TILE, Q_BLOCK), 0)
    col = lax.broadcasted_iota(I32, (KEY_TILE, Q_BLOCK), 1)
    qpos = qb * Q_BLOCK + col
    iq_all = iq_ref[0].reshape(IDX_HEADS * Q_BLOCK, IDX_DIM)
    iw = iw_ref[0]
    int_min = jnp.int32(INT_MIN)
    score_scale = IDX_HEADS ** -0.5

    def tile_rows(j):
        return pl.ds(pl.multiple_of(j * KEY_TILE, KEY_TILE), KEY_TILE)

    def tree_sum(parts):
        while len(parts) > 1:
            parts = [a + b for a, b in zip(parts[::2], parts[1::2])]
        return parts[0]

    def score_tile(j, carry):
        s = _nt_dot(ik_ref[0, tile_rows(j), :].astype(BF16), iq_all)
        terms = [jnp.maximum(s[:, h * Q_BLOCK:(h + 1) * Q_BLOCK], 0.0) * iw[h:h + 1, :] for h in range(IDX_HEADS)]
        acc = terms[0]
        for term in terms[1:]:
            acc = acc + term
        score = (acc * (IDX_DIM ** -0.5)) * score_scale
        kint = jnp.where(j * KEY_TILE + row <= qpos, _sortable(score), int_min)
        sc_scr[tile_rows(j), :] = kint
        return carry

    lax.fori_loop(0, ntile, score_tile, 0)

    def count(pred):
        def body(j, acc):
            m = jnp.where(pred(sc_scr[tile_rows(j), :], j), 1.0, 0.0)
            return acc + tree_sum([m[i * 8:(i + 1) * 8] for i in range(KEY_TILE // 8)])
        acc = lax.fori_loop(0, ntile, body, jnp.zeros((8, Q_BLOCK), F32))
        return jnp.sum(acc, axis=0, keepdims=True)

    def bit_body(i, ans_u):
        cand_u = ans_u | lax.shift_left(jnp.int32(1), 31 - i)
        cand_s = cand_u ^ int_min
        cnt = count(lambda t, j: t >= cand_s)
        return jnp.where(cnt >= topk, cand_u, ans_u)

    thr = lax.fori_loop(0, 32, bit_body, jnp.zeros((1, Q_BLOCK), I32)) ^ int_min
    need = topk - count(lambda t, j: t > thr)
    n_ge = count(lambda t, j: t >= thr)

    j_scr[...] = jnp.full(j_scr.shape, 1 << 20, I32)
    has_tie = jnp.max(jnp.where((n_ge > topk) & (thr != int_min), 1.0, 0.0)) > 0.0

    @pl.when(has_tie)
    def _():
        def jbit(i, jc):
            cand = jc | lax.shift_left(jnp.int32(1), 12 - i)
            cnt = count(lambda t, j: (t == thr) & (j * KEY_TILE + row < cand))
            return jnp.where(cnt <= need, cand, jc)
        jc = lax.fori_loop(0, 13, jbit, jnp.zeros((1, Q_BLOCK), I32))
        j_scr[...] = jnp.broadcast_to(jc, j_scr.shape)

    j_cut = jnp.where(thr == int_min, 0, j_scr[0:1, :])

    q_scaled = [q_ref[0, h] * jnp.asarray(HEAD_DIM ** -0.5, BF16) for h in range(B_HEADS)]
    sub_tiles = KEY_TILE // Q_BLOCK
    groups = KEY_TILE // 8

    def tree_max(parts):
        while len(parts) > 1:
            parts = [jnp.maximum(a, b) for a, b in zip(parts[::2], parts[1::2])]
        return parts[0]

    def logits_tile(j, m_part):
        t = sc_scr[tile_rows(j), :]
        kidx = j * KEY_TILE + row
        madd = jnp.where(t > thr, 0.0, jnp.where(t == thr, jnp.where(kidx < j_cut, 0.0, MASK_NEG), MASK_NEG))
        offs = [jnp.clip(qb - (j * sub_tiles + i), 0, 2) for i in range(sub_tiles)]
        new = []
        for h in range(B_HEADS):
            bias = jnp.concatenate([tt_ref[h, off] for off in offs], axis=0)
            s = _nt_dot(k_ref[0, h, tile_rows(j), :], q_scaled[h]) + bias + madd
            s_scr[h, tile_rows(j), :] = s
            new.append(jnp.maximum(m_part[h], tree_max([s[i * 8:(i + 1) * 8] for i in range(groups)])))
        return jnp.stack(new)

    m_part = lax.fori_loop(0, ntile, logits_tile, jnp.full((B_HEADS, 8, Q_BLOCK), MASK_NEG, F32))
    m_rows = [jnp.max(m_part[h], axis=0, keepdims=True) for h in range(B_HEADS)]

    acc_scr[...] = jnp.zeros(acc_scr.shape, F32)

    def values_tile(j, l_part):
        new = []
        for h in range(B_HEADS):
            p = jnp.exp(s_scr[h, tile_rows(j), :] - m_rows[h])
            new.append(l_part[h] + tree_sum([p[i * 8:(i + 1) * 8] for i in range(groups)]))
            acc_scr[h] += _dot(vt_ref[0, h, :, tile_rows(j)], p.astype(BF16))
        return jnp.stack(new)

    l_part = lax.fori_loop(0, ntile, values_tile, jnp.zeros((B_HEADS, 8, Q_BLOCK), F32))
    for h in range(B_HEADS):
        l_row = jnp.sum(l_part[h], axis=0, keepdims=True)
        o_ref[0, h * HEAD_DIM:(h + 1) * HEAD_DIM, :] = (acc_scr[h] / l_row).astype(BF16)


def _dsa_prompt(iqh, iwt, ik3, qh, kh, vt, tt, topk):
    b, _, l, _ = qh.shape
    assert l % KEY_TILE == 0
    qblk = pl.BlockSpec((1, B_HEADS, Q_BLOCK, HEAD_DIM), lambda i, j: (i, 0, j, 0))
    once = pl.Buffered(1)
    return pl.pallas_call(
        functools.partial(_dsa_kernel, topk=topk),
        grid=(b, l // Q_BLOCK),
        in_specs=[qblk,
                  pl.BlockSpec((1, IDX_HEADS, Q_BLOCK), lambda i, j: (i, 0, j)),
                  pl.BlockSpec((1, l, IDX_DIM), lambda i, j: (i, 0, 0), pipeline_mode=once),
                  qblk,
                  pl.BlockSpec((1, B_HEADS, l, HEAD_DIM), lambda i, j: (i, 0, 0, 0), pipeline_mode=once),
                  pl.BlockSpec((1, B_HEADS, HEAD_DIM, l), lambda i, j: (i, 0, 0, 0), pipeline_mode=once),
                  pl.BlockSpec(tt.shape, lambda i, j: (0, 0, 0, 0), pipeline_mode=once)],
        out_specs=pl.BlockSpec((1, B_WIDTH, Q_BLOCK), lambda i, j: (i, 0, j)),
        out_shape=jax.ShapeDtypeStruct((b, B_WIDTH, l), BF16),
        scratch_shapes=[pltpu.VMEM((l, Q_BLOCK), I32),
                        pltpu.VMEM((B_HEADS, l, Q_BLOCK), F32),
                        pltpu.VMEM((B_HEADS, HEAD_DIM, Q_BLOCK), F32),
                        pltpu.VMEM((8, Q_BLOCK), I32)],
        compiler_params=_cparams(("arbitrary", "arbitrary"), 56),
        name="dsa_prompt",
    )(iqh, iwt, ik3, qh, kh, vt, tt)


def _sidx_kernel(pt_ref, iq_ref, iw_ref, *rest, npages):
    pages, ikn_ref, o_ref = rest[:npages], rest[npages], rest[npages + 1]
    iq, iw = iq_ref[0], iw_ref[0]
    nq = iq.shape[0] // IDX_HEADS
    for j in range(npages + 1):
        keys = (pages[j][0, 0] if j < npages else ikn_ref[0]).astype(BF16)
        r = jnp.maximum(_nt_dot(iq, keys), 0.0) * iw
        acc = r[0:nq]
        for h in range(1, IDX_HEADS):
            acc = acc + r[h * nq:(h + 1) * nq]
        o_ref[0, :, j * PAGE_SIZE:(j + 1) * PAGE_SIZE] = (acc * (IDX_DIM ** -0.5)) * (IDX_HEADS ** -0.5)


def _sample_scores(pt_flat, iqs, iws, cache_ik, layer, ikn, npages):
    bd, hq, _ = iqs.shape
    nq = hq // IDX_HEADS
    width = (npages + 1) * PAGE_SIZE
    page_spec = lambda p: pl.BlockSpec((1, 1, PAGE_SIZE, IDX_DIM),
                                       lambda i, pt, p=p: (layer, pt[i * npages + p], 0, 0))
    per_b = lambda a: pl.BlockSpec((1,) + a.shape[1:], lambda i, pt: (i, 0, 0))
    return pl.pallas_call(
        functools.partial(_sidx_kernel, npages=npages),
        grid_spec=pltpu.PrefetchScalarGridSpec(
            num_scalar_prefetch=1, grid=(bd,),
            in_specs=[per_b(iqs), per_b(iws)] + [page_spec(p) for p in range(npages)] + [per_b(ikn)],
            out_specs=pl.BlockSpec((1, nq, width), lambda i, pt: (i, 0, 0))),
        out_shape=jax.ShapeDtypeStruct((bd, nq, width), F32),
        compiler_params=_cparams(("arbitrary",)),
        name="sample_scores",
    )(pt_flat, iqs, iws, *([cache_ik] * npages), ikn)


def _sthr_kernel(s_ref, o_ref, k_scr, *, topk, past, nq):
    tr, width = s_ref.shape
    colw = lax.broadcasted_iota(I32, (tr, width), 1)
    rowq = lax.broadcasted_iota(I32, (tr, width), 0) % nq
    vis = colw <= past + rowq
    int_min = jnp.int32(INT_MIN)
    k_scr[...] = jnp.where(vis, _sortable(s_ref[...]), int_min)

    def count(pred):
        return jnp.sum(jnp.where(pred(k_scr[...]), 1.0, 0.0), axis=1, keepdims=True)

    def bit_body(i, ans_u):
        cand_u = ans_u | lax.shift_left(jnp.int32(1), 31 - i)
        cand_s = cand_u ^ int_min
        return jnp.where(count(lambda t: t >= cand_s) >= topk, cand_u, ans_u)

    thr = lax.fori_loop(0, 32, bit_body, jnp.zeros((tr, 1), I32)) ^ int_min
    need = topk - count(lambda t: t > thr)

    def jbit(i, jc):
        cand = jc | lax.shift_left(jnp.int32(1), 12 - i)
        cnt = count(lambda t: (t == thr) & (colw < cand))
        return jnp.where(cnt <= need, cand, jc)

    j_cut = lax.fori_loop(0, 13, jbit, jnp.zeros((tr, 1), I32))
    t = k_scr[...]
    sel = ((t > thr) | ((t == thr) & (colw < j_cut))) & vis
    o_ref[...] = jnp.where(sel, 0.0, MASK_NEG)


def _sample_mask(scores2, topk, past, nq):
    r, width = scores2.shape
    tr = min(r, 128)
    return pl.pallas_call(
        functools.partial(_sthr_kernel, topk=topk, past=past, nq=nq),
        grid=(r // tr,),
        in_specs=[pl.BlockSpec((tr, width), lambda i: (i, 0))],
        out_specs=pl.BlockSpec((tr, width), lambda i: (i, 0)),
        out_shape=jax.ShapeDtypeStruct((r, width), F32),
        scratch_shapes=[pltpu.VMEM((tr, width), I32)],
        compiler_params=_cparams(("arbitrary",)),
        name="sample_mask",
    )(scores2)


def _page_matrix(page_ref):
    heads = [page_ref[0, 0, pl.ds(h, PAGE_SIZE, stride=B_HEADS), :] for h in range(B_HEADS)]
    return jnp.concatenate(heads, axis=1).astype(BF16)


def _sattn_kernel(pt_ref, q_ref, m_ref, bfar_ref, bnear_ref, *rest, npages, nq):
    kpages, vpages = rest[:npages], rest[npages:2 * npages]
    kn_ref, vn_ref, o_ref, s_scr = rest[2 * npages:2 * npages + 4]
    q = q_ref[0]
    mx = jnp.full((B_HEADS * nq, PAGE_SIZE), MASK_NEG, F32)
    for j in range(npages + 1):
        kp = _page_matrix(kpages[j]) if j < npages else kn_ref[0].astype(BF16)
        cols = slice(j * PAGE_SIZE, (j + 1) * PAGE_SIZE)
        bias = bfar_ref[...] if j < npages - 1 else bnear_ref[:, (j - npages + 1) * PAGE_SIZE:(j - npages + 2) * PAGE_SIZE]
        madd = jnp.concatenate([m_ref[0, :, cols]] * B_HEADS, axis=0)
        s = _nt_dot(q, kp) * (HEAD_DIM ** -0.5) + bias + madd
        s_scr[:, cols] = s
        mx = jnp.maximum(mx, s)
    mrow = jnp.max(mx, axis=1, keepdims=True)
    lsum = jnp.zeros((B_HEADS * nq, PAGE_SIZE), F32)
    acc = jnp.zeros((B_HEADS * nq, B_WIDTH), F32)
    for j in range(npages + 1):
        vp = _page_matrix(vpages[j]) if j < npages else vn_ref[0].astype(BF16)
        p = jnp.exp(s_scr[:, j * PAGE_SIZE:(j + 1) * PAGE_SIZE] - mrow)
        lsum = lsum + p
        acc = acc + _dot(p.astype(BF16), vp)
    acc = acc / jnp.sum(lsum, axis=1, keepdims=True)
    head = lax.broadcasted_iota(I32, (nq, B_WIDTH), 1) // HEAD_DIM
    out = jnp.zeros((nq, B_WIDTH), F32)
    for h in range(B_HEADS):
        out = jnp.where(head == h, acc[h * nq:(h + 1) * nq], out)
    o_ref[0] = out.astype(BF16)


def _sample_attend(pt_flat, qbd, madd3, bfar, bnear, cache_k, cache_v, layer, kn, vn, npages):
    bd, hq, _ = qbd.shape
    nq = hq // B_HEADS
    width = (npages + 1) * PAGE_SIZE
    page_spec = lambda p: pl.BlockSpec((1, 1, PAGE_SIZE * B_HEADS, HEAD_DIM),
                                       lambda i, pt, p=p: (layer, pt[i * npages + p], 0, 0))
    per_b = lambda a: pl.BlockSpec((1,) + a.shape[1:], lambda i, pt: (i, 0, 0))
    const2 = lambda a: pl.BlockSpec(a.shape, lambda i, pt: (0, 0))
    pages = [page_spec(p) for p in range(npages)]
    return pl.pallas_call(
        functools.partial(_sattn_kernel, npages=npages, nq=nq),
        grid_spec=pltpu.PrefetchScalarGridSpec(
            num_scalar_prefetch=1, grid=(bd,),
            in_specs=[per_b(qbd), per_b(madd3), const2(bfar), const2(bnear)] + pages + pages + [per_b(kn), per_b(vn)],
            out_specs=pl.BlockSpec((1, nq, B_WIDTH), lambda i, pt: (i, 0, 0)),
            scratch_shapes=[pltpu.VMEM((hq, width), F32)]),
        out_shape=jax.ShapeDtypeStruct((bd, nq, B_WIDTH), BF16),
        compiler_params=_cparams(("arbitrary",), 56),
        name="sample_attend",
    )(pt_flat, qbd, madd3, bfar, bnear, *([cache_k] * npages), *([cache_v] * npages), kn, vn)


def _outproj_kernel(a_ref, b_ref, c_ref, wa_ref, wb_ref, wc_ref, x_ref, g_ref, bb_ref, o_ref):
    mix = _dot(a_ref[...], wa_ref[...]) + _dot(b_ref[...], wb_ref[...]) + _dot(c_ref[...], wc_ref[...])
    o_ref[...] = _layer_norm_rows(ALPHA * x_ref[...] + mix, g_ref[...], bb_ref[...])


def _outproj_ln(a, b, c, wa, wb, wc, x2, gain, bias):
    n = x2.shape[0]
    tm = 512
    rows = lambda a_: pl.BlockSpec((tm, a_.shape[1]), lambda i: (i, 0))
    full2 = lambda a_: pl.BlockSpec(a_.shape, lambda i: (0, 0))
    return pl.pallas_call(
        _outproj_kernel,
        grid=(n // tm,),
        in_specs=[rows(a), rows(b), rows(c), full2(wa), full2(wb), full2(wc), rows(x2), full2(gain), full2(bias)],
        out_specs=pl.BlockSpec((tm, D_MODEL), lambda i: (i, 0)),
        out_shape=jax.ShapeDtypeStruct((n, D_MODEL), F32),
        compiler_params=_cparams(("arbitrary",), 40),
        name="outproj_ln1",
    )(a, b, c, wa, wb, wc, x2, gain, bias)


def _router_kernel(x_ref, wh_ref, wl_ref, b_ref, o_ref):
    x = x_ref[...]
    xh = x.astype(BF16)
    xl = (x - xh.astype(F32)).astype(BF16)
    logits = _dot(xh, wh_ref[...]) + _dot(xh, wl_ref[...]) + _dot(xl, wh_ref[...]) + b_ref[...]
    lane = lax.broadcasted_iota(I32, logits.shape, 1).astype(F32)
    neg = -jnp.inf
    first_lane = lambda m: jnp.min(jnp.where(m, lane, float(LANES)), axis=1, keepdims=True)
    is_g = (lane >= N_EXPERTS) & (lane < N_EXPERTS + N_GROUPS)
    gmax = jnp.max(jnp.where(is_g, logits, neg), axis=1, keepdims=True)
    g_sel = first_lane(is_g & (logits == gmax)) - N_EXPERTS
    p_group = 1.0 / jnp.sum(jnp.where(is_g, jnp.exp(logits - gmax), 0.0), axis=1, keepdims=True)
    in_g = jnp.floor(lane * (1.0 / EXPERTS_PER_GROUP)) == g_sel
    ev = jnp.where(in_g, logits, neg)
    v1 = jnp.max(ev, axis=1, keepdims=True)
    i1 = first_lane(in_g & (logits == v1))
    ev2 = jnp.where(lane == i1, neg, ev)
    v2 = jnp.max(ev2, axis=1, keepdims=True)
    i2 = first_lane(in_g & (lane != i1) & (logits == v2))
    e2 = jnp.exp(v2 - v1)
    den = 1.0 + e2
    g1 = (1.0 / den) * p_group
    g2 = (e2 / den) * p_group
    o_ref[...] = jnp.where(lane == i1, g1, jnp.where(lane == i2, g2, 0.0))


def _router(x2, wh, wl, bias):
    n = x2.shape[0]
    tm = 512
    full2 = lambda a_: pl.BlockSpec(a_.shape, lambda i: (0, 0))
    return pl.pallas_call(
        _router_kernel,
        grid=(n // tm,),
        in_specs=[pl.BlockSpec((tm, D_MODEL), lambda i: (i, 0)), full2(wh), full2(wl), full2(bias)],
        out_specs=pl.BlockSpec((tm, LANES), lambda i: (i, 0)),
        out_shape=jax.ShapeDtypeStruct((n, LANES), F32),
        compiler_params=_cparams(("arbitrary",)),
        name="router",
    )(x2, wh, wl, bias)


def _moe_kernel(x_ref, cmb_ref, wgu_ref, wd_ref, g_ref, b_ref, o_ref, xb_scr, acc_scr):
    e = pl.program_id(1)

    @pl.when(e == 0)
    def _():
        xb_scr[...] = x_ref[...].astype(BF16)
        acc_scr[...] = jnp.zeros(acc_scr.shape, F32)

    gu = _dot(xb_scr[...], wgu_ref[0])
    cmb = cmb_ref[...]
    lane = lax.broadcasted_iota(I32, cmb.shape, 1)
    ce = jnp.sum(jnp.where(lane == e, cmb, 0.0), axis=1, keepdims=True)
    hid = _silu(gu[:, :EXPERT_FF]) * gu[:, EXPERT_FF:] * ce
    acc_scr[...] += _dot(hid.astype(BF16), wd_ref[0])

    @pl.when(e == pl.num_programs(1) - 1)
    def _():
        o_ref[...] = _layer_norm_rows(ALPHA * x_ref[...] + acc_scr[...], g_ref[...], b_ref[...])


def _moe_ln(x2, cmb, wgu, wd, gain, bias):
    n = x2.shape[0]
    tm = 1024 if n > 1024 else 512
    full2 = lambda a_: pl.BlockSpec(a_.shape, lambda i, e: (0, 0))
    return pl.pallas_call(
        _moe_kernel,
        grid=(n // tm, N_EXPERTS),
        in_specs=[pl.BlockSpec((tm, D_MODEL), lambda i, e: (i, 0)),
                  pl.BlockSpec((tm, LANES), lambda i, e: (i, 0)),
                  pl.BlockSpec((1, D_MODEL, 2 * EXPERT_FF), lambda i, e: (e, 0, 0)),
                  pl.BlockSpec((1, EXPERT_FF, D_MODEL), lambda i, e: (e, 0, 0)),
                  full2(gain), full2(bias)],
        out_specs=pl.BlockSpec((tm, D_MODEL), lambda i, e: (i, 0)),
        out_shape=jax.ShapeDtypeStruct((n, D_MODEL), F32),
        scratch_shapes=[pltpu.VMEM((tm, D_MODEL), BF16), pltpu.VMEM((tm, D_MODEL), F32)],
        compiler_params=_cparams(("arbitrary", "arbitrary"), 48),
        name="moe_ln2",
    )(x2, cmb, wgu, wd, gain, bias)


def _layer_weights(l, w_in, w_out, gv_gain, gv_bias, ws, bs, ret_gain, ret_bias, ln1_g, ln1_b, ln2_g, ln2_b,
                   rg_w, rg_b, re_w, re_b, e_gate, e_up, e_down):
    w = w_in[l]
    z = lambda k: jnp.zeros((D_MODEL, k), w.dtype)
    ik_end = 2560 + IDX_DIM
    iw_end = ik_end + IDX_HEADS
    w_pad = jnp.concatenate([w[:, :ik_end], z(COL_IW - COL_IK - IDX_DIM), w[:, ik_end:iw_end],
                             z(COL_CQ - COL_IW - IDX_HEADS), w[:, iw_end:]], axis=1).astype(BF16)
    wo = w_out[l].astype(BF16)
    wr = jnp.concatenate([re_w[l], rg_w[l], jnp.zeros((D_MODEL, LANES - N_EXPERTS - N_GROUPS), F32)], axis=1)
    wrh = wr.astype(BF16)
    wrl = (wr - wrh.astype(F32)).astype(BF16)
    br = jnp.concatenate([re_b[l], rg_b[l], jnp.zeros((LANES - N_EXPERTS - N_GROUPS,), F32)])[None, :]
    row = lambda a: a[l].reshape(1, -1)
    return dict(
        w_pad=w_pad, wa=wo[:A_WIDTH], wb=wo[A_WIDTH:A_WIDTH + B_WIDTH], wc=wo[A_WIDTH + B_WIDTH:],
        gv_gain=row(gv_gain), gv_bias=row(gv_bias), ws=ws[l], bs=bs[l],
        ret_gain=row(ret_gain), ret_bias=row(ret_bias),
        ln1_g=row(ln1_g), ln1_b=row(ln1_b), ln2_g=row(ln2_g), ln2_b=row(ln2_b),
        wrh=wrh, wrl=wrl, br=br,
        wgu=jnp.concatenate([e_gate[l], e_up[l]], axis=2).astype(BF16), wd=e_down[l].astype(BF16))


def _channel_mix(x2, a_out, b_out, c_out, lw):
    x1 = _outproj_ln(a_out, b_out, c_out, lw["wa"], lw["wb"], lw["wc"], x2, lw["ln1_g"], lw["ln1_b"])
    cmb = _router(x1, lw["wrh"], lw["wrl"], lw["br"])
    return _moe_ln(x1, cmb, lw["wgu"], lw["wd"], lw["ln2_g"], lw["ln2_b"])


def _heads(t2, b, l, nh):
    return t2.reshape(b, l, nh, t2.shape[-1] // nh)


def _prompt_layer(x3, lw, tt, cosf, sins):
    b, l, _ = x3.shape
    x2 = x3.reshape(b * l, D_MODEL)
    h2 = _proj(x2, lw["w_pad"])
    col = lambda c0, wdt: h2[:, c0:c0 + wdt]
    bsb = jnp.repeat(lw["bs"].T, HEAD_DIM, axis=1)
    a_out, a_vn = _gmlp(h2, lw["gv_gain"], lw["gv_bias"], lw["ws"], bsb)
    k4 = _heads(col(COL_K, B_WIDTH), b, l, B_HEADS)
    v4 = _heads(col(COL_V, B_WIDTH), b, l, B_HEADS)
    ik3 = col(COL_IK, IDX_DIM).reshape(b, l, IDX_DIM)
    hm = lambda t4: jnp.transpose(t4, (0, 2, 1, 3)).astype(BF16)
    qh = hm(_heads(col(COL_Q, B_WIDTH), b, l, B_HEADS))
    iqh = hm(_heads(col(COL_IQ, IDX_HEADS * IDX_DIM), b, l, IDX_HEADS))
    iwt = jnp.transpose(col(COL_IW, IDX_HEADS).reshape(b, l, IDX_HEADS), (0, 2, 1))
    vt = jnp.transpose(v4, (0, 2, 3, 1)).astype(BF16)
    topk = min(TOPK_MAX, l // 4)
    bt_out = _dsa_prompt(iqh, iwt, ik3, qh, hm(k4), vt, tt, topk)
    b_out = jnp.transpose(bt_out, (0, 2, 1)).reshape(b * l, B_WIDTH)
    s0e = jnp.zeros((b, C_WIDTH, C_WIDTH), F32)
    c_out, s_fin = _retention(h2.reshape(b, l, H_WIDTH), cosf, sins, lw["ret_gain"], lw["ret_bias"], s0e,
                              RET_CHUNK, 1)
    y = _channel_mix(x2, a_out, b_out, c_out.reshape(b * l, C_WIDTH), lw)
    last = ((l - 1) // CHUNK) * CHUNK
    state = (k4, v4, ik3, _state_extract(s_fin), a_vn.reshape(b, l, A_WIDTH)[:, last:])
    return y.reshape(b, l, D_MODEL), state


def _sample_layer(x3, lw, layer, cache_k, cache_v, cache_ik, state_l, pt_flat, npages, bfar, bnear, cosf, sins):
    bd, t, _ = x3.shape
    n = bd * t
    past = npages * PAGE_SIZE
    x2 = x3.reshape(n, D_MODEL)
    h2 = _proj(x2, lw["w_pad"])
    col = lambda c0, wdt: h2[:, c0:c0 + wdt]
    rep = CHUNK // t
    eye = jnp.eye(rep, dtype=F32)
    ws_t = lw["ws"][:, :t, :t]
    ws_bd = (eye[None, :, None, :, None] * ws_t[:, None, :, None, :]).reshape(A_GROUPS, CHUNK, CHUNK)
    bsb = jnp.repeat(jnp.tile(lw["bs"][:, :t], (1, rep)).T, HEAD_DIM, axis=1)
    a_out, a_vn = _gmlp(h2, lw["gv_gain"], lw["gv_bias"], ws_bd, bsb)
    k4 = _heads(col(COL_K, B_WIDTH), bd, t, B_HEADS)
    v4 = _heads(col(COL_V, B_WIDTH), bd, t, B_HEADS)
    ik3 = col(COL_IK, IDX_DIM).reshape(bd, t, IDX_DIM)
    pad_rows = lambda a3: jnp.pad(a3, ((0, 0), (0, PAGE_SIZE - t), (0, 0)))
    iq4 = _heads(col(COL_IQ, IDX_HEADS * IDX_DIM), bd, t, IDX_HEADS)
    iqs = jnp.transpose(iq4, (0, 2, 1, 3)).reshape(bd, IDX_HEADS * t, IDX_DIM).astype(BF16)
    iw3 = jnp.transpose(col(COL_IW, IDX_HEADS).reshape(bd, t, IDX_HEADS), (0, 2, 1))
    iws = jnp.broadcast_to(iw3.reshape(bd, IDX_HEADS * t, 1), (bd, IDX_HEADS * t, LANES))
    scores = _sample_scores(pt_flat, iqs, iws, cache_ik, layer, pad_rows(ik3), npages)
    width = (npages + 1) * PAGE_SIZE
    topk = min(TOPK_MAX, (past + t) // 4)
    madd = _sample_mask(scores.reshape(n, width), topk, past, t).reshape(bd, t, width)
    q4 = _heads(col(COL_Q, B_WIDTH), bd, t, B_HEADS)
    eye_h = jnp.eye(B_HEADS, dtype=F32)
    qbd = (jnp.transpose(q4, (0, 2, 1, 3))[:, :, :, None, :] * eye_h[None, :, None, :, None])
    qbd = qbd.reshape(bd, B_HEADS * t, B_WIDTH).astype(BF16)
    rows_view = lambda c: c.reshape(c.shape[0], c.shape[1], PAGE_SIZE * B_HEADS, HEAD_DIM)
    b_out = _sample_attend(pt_flat, qbd, madd, bfar, bnear, rows_view(cache_k), rows_view(cache_v), layer,
                           pad_rows(k4.reshape(bd, t, B_WIDTH)), pad_rows(v4.reshape(bd, t, B_WIDTH)), npages)
    c_out, s_fin = _retention(h2.reshape(bd, t, H_WIDTH), cosf, sins, lw["ret_gain"], lw["ret_bias"],
                              _state_embed(state_l), t, 8)
    y = _channel_mix(x2, a_out, b_out.reshape(n, B_WIDTH), c_out.reshape(n, C_WIDTH), lw)
    state = (k4, v4, ik3, _state_extract(s_fin), a_vn.reshape(bd, t, A_WIDTH))
    return y.reshape(bd, t, D_MODEL), state


def _distance_tables(rel_bias, t, past):
    r = jnp.arange(Q_BLOCK, dtype=I32)
    d_prompt = jnp.concatenate([off * Q_BLOCK + r[None, :] - r[:, None] for off in range(3)], axis=0)
    qpos = past + jnp.arange(t, dtype=I32)
    near0 = past - PAGE_SIZE
    d_near = [qpos[:, None] - (near0 + half * PAGE_SIZE + r[None, :]) for half in range(2)]
    d_far = jnp.full((t, LANES), MAX_DISTANCE * 2, I32)
    tables = _bias_tables(jnp.concatenate([d_prompt] + d_near + [d_far], axis=0), rel_bias)
    n0 = 3 * Q_BLOCK
    tt = tables[:, :n0].reshape(B_HEADS, 3, Q_BLOCK, Q_BLOCK)
    bnear = jnp.concatenate([tables[:, n0:n0 + t], tables[:, n0 + t:n0 + 2 * t]], axis=2).reshape(B_HEADS * t, 2 * LANES)
    bfar = tables[:, n0 + 2 * t:n0 + 3 * t].reshape(B_HEADS * t, LANES)
    return tt, bnear, bfar


def kernel(x_prompt, x_sample, cache_k, cache_v, cache_idx_k, state_ret, page_table, w_in, w_out, gmlp_v_gain,
           gmlp_v_bias, gmlp_ws, gmlp_bs, rel_bias, ret_gn_gain, ret_gn_bias, ln1_gain, ln1_bias, ln2_gain, ln2_bias,
           router_group_w, router_group_b, router_expert_w, router_expert_b, expert_w_gate, expert_w_up,
           expert_w_down):
    depth = w_in.shape[0]
    seq = x_prompt.shape[1]
    bd, t, _ = x_sample.shape
    npages = page_table.shape[1]
    past = npages * PAGE_SIZE
    pt_flat = page_table.reshape(-1).astype(I32)
    tt, bnear, bfar = _distance_tables(rel_bias, t, past)
    cos_p, sin_p = _rope_tables(jnp.arange(seq, dtype=I32))
    cos_s, sin_s = _rope_tables(past + jnp.arange(t, dtype=I32))
    xp, xs = x_prompt, x_sample
    st_p, st_s = [], []
    for l in range(depth):
        lw = _layer_weights(l, w_in, w_out, gmlp_v_gain, gmlp_v_bias, gmlp_ws, gmlp_bs, ret_gn_gain, ret_gn_bias,
                            ln1_gain, ln1_bias, ln2_gain, ln2_bias, router_group_w, router_group_b,
                            router_expert_w, router_expert_b, expert_w_gate, expert_w_up, expert_w_down)
        xp, sp = _prompt_layer(xp, lw, tt, cos_p, sin_p)
        xs, ss = _sample_layer(xs, lw, l, cache_k, cache_v, cache_idx_k, state_ret[l], pt_flat, npages,
                               bfar, bnear, cos_s, sin_s)
        st_p.append(sp)
        st_s.append(ss)
    stk = lambda sts, i: jnp.stack([s[i] for s in sts], axis=0)
    return (xp, xs,
            stk(st_p, 0), stk(st_p, 1), stk(st_p, 2), stk(st_p, 3), stk(st_p, 4),
            stk(st_s, 0), stk(st_s, 1), stk(st_s, 2), stk(st_s, 3), stk(st_s, 4))
```

```python
import functools
import math

import jax
import jax.numpy as jnp
from jax import lax
from jax.experimental import pallas as pl
from jax.experimental.pallas import tpu as pltpu

D_MODEL = 1024
HEAD_DIM = 64
A_GROUPS = 4
A_WIDTH = 256
CHUNK = 128
B_HEADS = 8
B_WIDTH = 512
IDX_HEADS = 8
IDX_DIM = 64
TOPK_MAX = 256
Q_BLOCK = 128
KEY_TILE = 256
NUM_BUCKETS = 32
MAX_DISTANCE = 128
C_HEADS = 4
C_WIDTH = 256
RET_CHUNK = 128
ROPE_BASE = 10000.0
PAGE_SIZE = 128
N_GROUPS = 4
EXPERTS_PER_GROUP = 8
N_EXPERTS = 32
EXPERT_FF = 256
DEPTH = 2
ALPHA = (2 * DEPTH) ** 0.25
LN_EPS = 1e-5

F32 = jnp.float32
BF16 = jnp.bfloat16
I32 = jnp.int32
LANES = 128
MASK_NEG = -1e30
INT_MIN = -2 ** 31

COL_AU, COL_AV, COL_Q, COL_K, COL_V, COL_IQ, COL_IK, COL_IW = 0, 256, 512, 1024, 1536, 2048, 2560, 2688
COL_CQ, COL_CK, COL_CV, COL_CG = 2816, 3072, 3328, 3584
H_WIDTH = 3840


def _cparams(sem, vmem_mb=None):
    kw = dict(dimension_semantics=sem)
    if vmem_mb is not None:
        kw["vmem_limit_bytes"] = vmem_mb << 20
    return pltpu.CompilerParams(**kw)


def _nt_dot(a, b):
    return lax.dot_general(a, b, (((1,), (1,)), ((), ())), preferred_element_type=F32)


def _dot(a, b):
    return jnp.dot(a, b, preferred_element_type=F32)


def _layer_norm_rows(x, gain, bias):
    mu = jnp.mean(x, axis=-1, keepdims=True)
    xc = x - mu
    var = jnp.mean(xc * xc, axis=-1, keepdims=True)
    return xc * lax.rsqrt(var + LN_EPS) * gain + bias


def _silu(x):
    return x * (1.0 / (1.0 + jnp.exp(-x)))


def _proj_kernel(x_ref, w_ref, o_ref):
    o_ref[...] = _dot(x_ref[...].astype(BF16), w_ref[...])


def _proj(x2, w_pad):
    n = x2.shape[0]
    tm = 1024 if n > 1024 else 512
    tn = 768
    return pl.pallas_call(
        _proj_kernel,
        grid=(n // tm, H_WIDTH // tn),
        in_specs=[pl.BlockSpec((tm, D_MODEL), lambda i, j: (i, 0)),
                  pl.BlockSpec((D_MODEL, tn), lambda i, j: (0, j))],
        out_specs=pl.BlockSpec((tm, tn), lambda i, j: (i, j)),
        out_shape=jax.ShapeDtypeStruct((n, H_WIDTH), F32),
        compiler_params=_cparams(("arbitrary", "arbitrary"), 40),
        name="in_proj",
    )(x2, w_pad)


PCOL_AU, PCOL_AV, PCOL_CQ, PCOL_CK, PCOL_CV, PCOL_CG = 0, 256, 512, 768, 1024, 1280
P_WIDTH = 1536


def _proj_prompt_kernel(x_ref, w_ref, ha_ref, kt_ref, vt_ref, vtb_ref, ikt_ref, kh_ref, qh_ref, iqh_ref, ikb_ref,
                        iwt_ref):
    x = x_ref[0].astype(BF16)
    piece = lambda c0, width: _dot(x, w_ref[:, c0:c0 + width])
    ha_ref[:, :COL_Q] = piece(COL_AU, COL_Q)
    ha_ref[:, COL_Q:] = piece(COL_CQ, H_WIDTH - COL_CQ)

    def head_major(t, ref):
        for h in range(t.shape[1] // HEAD_DIM):
            ref[0, h] = t[:, h * HEAD_DIM:(h + 1) * HEAD_DIM].astype(BF16)

    head_major(piece(COL_Q, B_WIDTH), qh_ref)
    head_major(piece(COL_IQ, IDX_HEADS * IDX_DIM), iqh_ref)
    k = piece(COL_K, B_WIDTH)
    head_major(k, kh_ref)
    kt_ref[0] = jnp.transpose(k)
    vt = jnp.transpose(piece(COL_V, B_WIDTH))
    vt_ref[0] = vt
    vtb_ref[0] = vt.astype(BF16)
    ik = piece(COL_IK, LANES)
    ikb_ref[0] = ik[:, :IDX_DIM].astype(BF16)
    ikt_ref[0] = jnp.transpose(ik)[:IDX_DIM]
    iwt_ref[0] = jnp.transpose(piece(COL_IW, LANES))[:IDX_HEADS]


def _proj_prompt(x3, w_pad):
    b, l, _ = x3.shape
    tm = 512
    nt = l // tm
    f32s = lambda *shape: jax.ShapeDtypeStruct(shape, F32)
    bf16s = lambda *shape: jax.ShapeDtypeStruct(shape, BF16)
    feat = lambda rows: pl.BlockSpec((1, rows, tm), lambda i, j: (i, 0, j))
    hmaj = pl.BlockSpec((1, B_HEADS, tm, HEAD_DIM), lambda i, j: (i, 0, j, 0))
    return pl.pallas_call(
        _proj_prompt_kernel,
        grid=(b, nt),
        in_specs=[pl.BlockSpec((1, tm, D_MODEL), lambda i, j: (i, j, 0)),
                  pl.BlockSpec(w_pad.shape, lambda i, j: (0, 0), pipeline_mode=pl.Buffered(1))],
        out_specs=[pl.BlockSpec((tm, P_WIDTH), lambda i, j: (i * nt + j, 0)),
                   feat(B_WIDTH), feat(B_WIDTH), feat(B_WIDTH), feat(IDX_DIM),
                   hmaj, hmaj, hmaj,
                   pl.BlockSpec((1, tm, IDX_DIM), lambda i, j: (i, j, 0)),
                   feat(IDX_HEADS)],
        out_shape=[f32s(b * l, P_WIDTH), f32s(b, B_WIDTH, l), f32s(b, B_WIDTH, l), bf16s(b, B_WIDTH, l),
                   f32s(b, IDX_DIM, l), bf16s(b, B_HEADS, l, HEAD_DIM), bf16s(b, B_HEADS, l, HEAD_DIM),
                   bf16s(b, IDX_HEADS, l, IDX_DIM), bf16s(b, l, IDX_DIM), f32s(b, IDX_HEADS, l)],
        compiler_params=_cparams(("arbitrary", "arbitrary"), 48),
        name="in_proj_prompt",
    )(x3, w_pad)


def _bias_kernel(rb_ref, d_ref, o_ref):
    n = jnp.maximum(d_ref[...], 0)
    max_exact = NUM_BUCKETS // 2
    nf = jnp.maximum(n, 1).astype(F32)
    large = max_exact + (jnp.log(nf / max_exact) / math.log(MAX_DISTANCE / max_exact)
                         * (NUM_BUCKETS - max_exact)).astype(I32)
    large = jnp.minimum(large, NUM_BUCKETS - 1)
    bucket = jnp.where(n < max_exact, n, large)
    for h in range(B_HEADS):
        acc = jnp.zeros(bucket.shape, F32)
        for bk in range(NUM_BUCKETS):
            acc = jnp.where(bucket == bk, rb_ref[bk * B_HEADS + h], acc)
        o_ref[h] = acc


def _bias_tables(dist, rel_bias):
    r = dist.shape[0]
    return pl.pallas_call(
        _bias_kernel,
        in_specs=[pl.BlockSpec(memory_space=pltpu.SMEM),
                  pl.BlockSpec((r, LANES), lambda: (0, 0))],
        out_specs=pl.BlockSpec((B_HEADS, r, LANES), lambda: (0, 0, 0)),
        out_shape=jax.ShapeDtypeStruct((B_HEADS, r, LANES), F32),
        name="bias_tables",
    )(rel_bias.reshape(-1), dist)


def _gmlp_kernel(u_ref, v_ref, g_ref, b_ref, ws_ref, bsb_ref, o_ref, vn_ref, *, nchunk):
    r = lax.broadcasted_iota(I32, (CHUNK, CHUNK), 0)
    c = lax.broadcasted_iota(I32, (CHUNK, CHUNK), 1)
    grp = lax.broadcasted_iota(I32, (CHUNK, A_WIDTH), 1) // HEAD_DIM
    wts = [jnp.where(r >= c, ws_ref[g], 0.0).astype(BF16) for g in range(A_GROUPS)]
    gain, bias, bsb = g_ref[...], b_ref[...], bsb_ref[...]
    for ci in range(nchunk):
        sl = pl.ds(ci * CHUNK, CHUNK)
        vn = _layer_norm_rows(v_ref[sl, :], gain, bias)
        vn_ref[sl, :] = vn
        vb = vn.astype(BF16)
        mixed = bsb
        for g in range(A_GROUPS):
            mixed = mixed + jnp.where(grp == g, _dot(wts[g], vb), 0.0)
        o_ref[sl, :] = (u_ref[sl, :] * mixed).astype(BF16)


def _gmlp(h2, col_u, col_v, gain, bias, ws, bsb):
    n = h2.shape[0]
    tm = min(n, 1024)
    blk = lambda col: pl.BlockSpec((tm, A_WIDTH), lambda i: (i, col // A_WIDTH))
    full2 = lambda a: pl.BlockSpec(a.shape, lambda i: (0, 0))
    return pl.pallas_call(
        functools.partial(_gmlp_kernel, nchunk=tm // CHUNK),
        grid=(n // tm,),
        in_specs=[blk(col_u), blk(col_v), full2(gain), full2(bias),
                  pl.BlockSpec(ws.shape, lambda i: (0, 0, 0)), full2(bsb)],
        out_specs=[pl.BlockSpec((tm, A_WIDTH), lambda i: (i, 0)),
                   pl.BlockSpec((tm, A_WIDTH), lambda i: (i, 0))],
        out_shape=[jax.ShapeDtypeStruct((n, A_WIDTH), BF16),
                   jax.ShapeDtypeStruct((n, A_WIDTH), F32)],
        compiler_params=_cparams(("arbitrary",)),
        name="gmlp",
    )(h2, h2, gain, bias, ws, bsb)


def _ret_tables(c):
    log_g = jnp.log(1.0 - 2.0 ** (-5.0 - jnp.arange(C_HEADS, dtype=F32)))
    i = jnp.arange(c, dtype=F32)
    diff = i[:, None] - i[None, :]
    dmat = jnp.where(diff >= 0, jnp.exp(log_g[:, None, None] * jnp.maximum(diff, 0.0)), 0.0)
    q_dec = jnp.exp(log_g[:, None] * (i + 1.0))
    k_dec = jnp.exp(log_g[:, None] * (c - 1.0 - i))
    s_dec = jnp.exp(log_g * c)
    qd = jnp.repeat(q_dec.T, HEAD_DIM, axis=1)
    kd = jnp.repeat(k_dec.T, HEAD_DIM, axis=1)
    hid = jnp.arange(C_WIDTH) // HEAD_DIM
    same = hid[:, None] == hid[None, :]
    sd = jnp.where(same, s_dec[hid][:, None], 0.0)
    return dmat, qd, kd, sd, same.astype(F32)


def _rope_tables(pos):
    half = HEAD_DIM // 2
    inv = ROPE_BASE ** (-jnp.arange(half, dtype=F32) / half)
    ang = pos.astype(F32)[:, None] * inv[None, :]
    cos, sin = jnp.cos(ang), jnp.sin(ang)
    cosf = jnp.tile(jnp.concatenate([cos, cos], axis=1), (1, C_HEADS))
    sins = jnp.tile(jnp.concatenate([-sin, sin], axis=1), (1, C_HEADS))
    return cosf, sins


def _ret_kernel(q_ref, k_ref, v_ref, g_ref, cos_ref, sin_ref, qd_ref, kd_ref, dm_ref, sd_ref, bd_ref,
                gg_ref, gb_ref, s0_ref, o_ref, sf_ref, s_scr, *, bt, c):
    ci = pl.program_id(1)

    @pl.when(ci == 0)
    def _():
        s_scr[...] = s0_ref[...]

    lane = lax.broadcasted_iota(I32, (c, C_WIDTH), 1)
    hid = lane // HEAD_DIM
    first = (lane % HEAD_DIM) < (HEAD_DIM // 2)
    cosf, sins = cos_ref[...], sin_ref[...]
    half = HEAD_DIM // 2

    def rot(x):
        partner = jnp.where(first, pltpu.roll(x, C_WIDTH - half, 1), pltpu.roll(x, half, 1))
        return x * cosf + partner * sins

    def seg_mean(x):
        out = jnp.zeros_like(x)
        for h in range(C_HEADS):
            hm = hid == h
            s = jnp.sum(jnp.where(hm, x, 0.0), axis=1, keepdims=True) * (1.0 / HEAD_DIM)
            out = jnp.where(hm, s, out)
        return out

    for bb in range(bt):
        q = rot(q_ref[bb])
        k = rot(k_ref[bb]) * (HEAD_DIM ** -0.5)
        v = v_ref[bb]
        kb, vb = k.astype(BF16), v.astype(BF16)
        s_old = s_scr[bb]
        o = _dot(q.astype(BF16), s_old.astype(BF16)) * qd_ref[...]
        for h in range(C_HEADS):
            hm = hid == h
            att = _nt_dot(jnp.where(hm, q, 0.0).astype(BF16), kb) * dm_ref[h]
            o = o + jnp.where(hm, _dot(att.astype(BF16), vb), 0.0)
        kdt = jnp.transpose(k * kd_ref[...]).astype(BF16)
        s_scr[bb] = s_old * sd_ref[...] + bd_ref[...] * _dot(kdt, vb)
        mu = seg_mean(o)
        oc = o - mu
        var = seg_mean(oc * oc)
        normed = oc * lax.rsqrt(var + LN_EPS) * gg_ref[...] + gb_ref[...]
        o_ref[bb] = (_silu(g_ref[bb]) * normed).astype(BF16)

    @pl.when(ci == pl.num_programs(1) - 1)
    def _():
        sf_ref[...] = s_scr[...]


def _retention(h3, cols, cosf, sins, gn_gain, gn_bias, s0e, c, bt):
    b, l, _ = h3.shape
    dmat, qd, kd, sd, bd = _ret_tables(c)
    blk = lambda col: pl.BlockSpec((bt, c, C_WIDTH), lambda i, j: (i, j, col // C_WIDTH))
    const2 = lambda a: pl.BlockSpec(a.shape, lambda i, j: (0, 0))
    pos_blk = pl.BlockSpec((c, C_WIDTH), lambda i, j: (j, 0))
    st_blk = pl.BlockSpec((bt, C_WIDTH, C_WIDTH), lambda i, j: (i, 0, 0))
    return pl.pallas_call(
        functools.partial(_ret_kernel, bt=bt, c=c),
        grid=(b // bt, l // c),
        in_specs=[blk(cols[0]), blk(cols[1]), blk(cols[2]), blk(cols[3]), pos_blk, pos_blk,
                  const2(qd), const2(kd), pl.BlockSpec(dmat.shape, lambda i, j: (0, 0, 0)),
                  const2(sd), const2(bd), const2(gn_gain), const2(gn_bias), st_blk],
        out_specs=[pl.BlockSpec((bt, c, C_WIDTH), lambda i, j: (i, j, 0)), st_blk],
        out_shape=[jax.ShapeDtypeStruct((b, l, C_WIDTH), BF16),
                   jax.ShapeDtypeStruct((b, C_WIDTH, C_WIDTH), F32)],
        scratch_shapes=[pltpu.VMEM((bt, C_WIDTH, C_WIDTH), F32)],
        compiler_params=_cparams(("arbitrary", "arbitrary")),
        name="retention",
    )(h3, h3, h3, h3, cosf, sins, qd, kd, dmat, sd, bd, gn_gain, gn_bias, s0e)


def _state_embed(s):
    b = s.shape[0]
    eye = jnp.eye(C_HEADS, dtype=s.dtype)
    return (s[:, :, :, None, :] * eye[None, :, None, :, None]).reshape(b, C_WIDTH, C_WIDTH)


def _state_extract(se):
    b = se.shape[0]
    s5 = se.reshape(b, C_HEADS, HEAD_DIM, C_HEADS, HEAD_DIM)
    return jnp.stack([s5[:, h, :, h, :] for h in range(C_HEADS)], axis=1)


def _sortable(score):
    bits = pltpu.bitcast(score, I32)
    return bits ^ ((bits >> 31) & 0x7FFFFFFF)


def _dsa_kernel(iq_ref, iw_ref, ik_ref, q_ref, k_ref, vt_ref, tt_ref, o_ref,
                sc_scr, s_scr, acc_scr, j_scr, *, topk):
    qb = pl.program_id(1)
    ntile = (qb + KEY_TILE // Q_BLOCK) // (KEY_TILE // Q_BLOCK)
    row = lax.broadcasted_iota(I32, (KEY_TILE, Q_BLOCK), 0)
    col = lax.broadcasted_iota(I32, (KEY_TILE, Q_BLOCK), 1)
    qpos = qb * Q_BLOCK + col
    iq_all = iq_ref[0].reshape(IDX_HEADS * Q_BLOCK, IDX_DIM)
    iw = iw_ref[0]
    int_min = jnp.int32(INT_MIN)
    score_scale = IDX_HEADS ** -0.5

    def tile_rows(j):
        return pl.ds(pl.multiple_of(j * KEY_TILE, KEY_TILE), KEY_TILE)

    def tree_sum(parts):
        while len(parts) > 1:
            parts = [a + b for a, b in zip(parts[::2], parts[1::2])]
        return parts[0]

    def score_tile(j, carry):
        s = _nt_dot(ik_ref[0, tile_rows(j), :].astype(BF16), iq_all)
        terms = [jnp.maximum(s[:, h * Q_BLOCK:(h + 1) * Q_BLOCK], 0.0) * iw[h:h + 1, :] for h in range(IDX_HEADS)]
        acc = terms[0]
        for term in terms[1:]:
            acc = acc + term
        score = (acc * (IDX_DIM ** -0.5)) * score_scale
        kint = jnp.where(j * KEY_TILE + row <= qpos, _sortable(score), int_min)
        sc_scr[tile_rows(j), :] = kint
        return carry

    lax.fori_loop(0, ntile, score_tile, 0)

    def count(pred):
        def body(j, acc):
            m = jnp.where(pred(sc_scr[tile_rows(j), :], j), 1.0, 0.0)
            return acc + tree_sum([m[i * 8:(i + 1) * 8] for i in range(KEY_TILE // 8)])
        acc = lax.fori_loop(0, ntile, body, jnp.zeros((8, Q_BLOCK), F32))
        return jnp.sum(acc, axis=0, keepdims=True)

    def bit_body(i, ans_u):
        cand_u = ans_u | lax.shift_left(jnp.int32(1), 31 - i)
        cand_s = cand_u ^ int_min
        cnt = count(lambda t, j: t >= cand_s)
        return jnp.where(cnt >= topk, cand_u, ans_u)

    thr = lax.fori_loop(0, 32, bit_body, jnp.zeros((1, Q_BLOCK), I32)) ^ int_min
    need = topk - count(lambda t, j: t > thr)
    n_ge = count(lambda t, j: t >= thr)

    j_scr[...] = jnp.full(j_scr.shape, 1 << 20, I32)
    has_tie = jnp.max(jnp.where((n_ge > topk) & (thr != int_min), 1.0, 0.0)) > 0.0

    @pl.when(has_tie)
    def _():
        def jbit(i, jc):
            cand = jc | lax.shift_left(jnp.int32(1), 12 - i)
            cnt = count(lambda t, j: (t == thr) & (j * KEY_TILE + row < cand))
            return jnp.where(cnt <= need, cand, jc)
        jc = lax.fori_loop(0, 13, jbit, jnp.zeros((1, Q_BLOCK), I32))
        j_scr[...] = jnp.broadcast_to(jc, j_scr.shape)

    j_cut = jnp.where(thr == int_min, 0, j_scr[0:1, :])

    q_scaled = [q_ref[0, h] * jnp.asarray(HEAD_DIM ** -0.5, BF16) for h in range(B_HEADS)]
    sub_tiles = KEY_TILE // Q_BLOCK
    groups = KEY_TILE // 8

    def tree_max(parts):
        while len(parts) > 1:
            parts = [jnp.maximum(a, b) for a, b in zip(parts[::2], parts[1::2])]
        return parts[0]

    def logits_tile(j, m_part):
        t = sc_scr[tile_rows(j), :]
        kidx = j * KEY_TILE + row
        madd = jnp.where(t > thr, 0.0, jnp.where(t == thr, jnp.where(kidx < j_cut, 0.0, MASK_NEG), MASK_NEG))
        offs = [jnp.clip(qb - (j * sub_tiles + i), 0, 2) for i in range(sub_tiles)]
        new = []
        for h in range(B_HEADS):
            bias = jnp.concatenate([tt_ref[h, off] for off in offs], axis=0)
            s = _nt_dot(k_ref[0, h, tile_rows(j), :], q_scaled[h]) + bias + madd
            s_scr[h, tile_rows(j), :] = s
            new.append(jnp.maximum(m_part[h], tree_max([s[i * 8:(i + 1) * 8] for i in range(groups)])))
        return jnp.stack(new)

    m_part = lax.fori_loop(0, ntile, logits_tile, jnp.full((B_HEADS, 8, Q_BLOCK), MASK_NEG, F32))
    m_rows = [jnp.max(m_part[h], axis=0, keepdims=True) for h in range(B_HEADS)]

    acc_scr[...] = jnp.zeros(acc_scr.shape, F32)

    def values_tile(j, l_part):
        new = []
        for h in range(B_HEADS):
            p = jnp.exp(s_scr[h, tile_rows(j), :] - m_rows[h])
            new.append(l_part[h] + tree_sum([p[i * 8:(i + 1) * 8] for i in range(groups)]))
            acc_scr[h] += _dot(vt_ref[0, h, :, tile_rows(j)], p.astype(BF16))
        return jnp.stack(new)

    l_part = lax.fori_loop(0, ntile, values_tile, jnp.zeros((B_HEADS, 8, Q_BLOCK), F32))
    out_t = jnp.concatenate([acc_scr[h] / jnp.sum(l_part[h], axis=0, keepdims=True) for h in range(B_HEADS)], axis=0)
    o_ref[0] = jnp.transpose(out_t).astype(BF16)


def _dsa_prompt(iqh, iwt, ik3, qh, kh, vt, tt, topk):
    b, _, l, _ = qh.shape
    assert l % KEY_TILE == 0
    qblk = pl.BlockSpec((1, B_HEADS, Q_BLOCK, HEAD_DIM), lambda i, j: (i, 0, j, 0))
    once = pl.Buffered(1)
    return pl.pallas_call(
        functools.partial(_dsa_kernel, topk=topk),
        grid=(b, l // Q_BLOCK),
        in_specs=[qblk,
                  pl.BlockSpec((1, IDX_HEADS, Q_BLOCK), lambda i, j: (i, 0, j)),
                  pl.BlockSpec((1, l, IDX_DIM), lambda i, j: (i, 0, 0), pipeline_mode=once),
                  qblk,
                  pl.BlockSpec((1, B_HEADS, l, HEAD_DIM), lambda i, j: (i, 0, 0, 0), pipeline_mode=once),
                  pl.BlockSpec((1, B_HEADS, HEAD_DIM, l), lambda i, j: (i, 0, 0, 0), pipeline_mode=once),
                  pl.BlockSpec(tt.shape, lambda i, j: (0, 0, 0, 0), pipeline_mode=once)],
        out_specs=pl.BlockSpec((1, Q_BLOCK, B_WIDTH), lambda i, j: (i, j, 0)),
        out_shape=jax.ShapeDtypeStruct((b, l, B_WIDTH), BF16),
        scratch_shapes=[pltpu.VMEM((l, Q_BLOCK), I32),
                        pltpu.VMEM((B_HEADS, l, Q_BLOCK), F32),
                        pltpu.VMEM((B_HEADS, HEAD_DIM, Q_BLOCK), F32),
                        pltpu.VMEM((8, Q_BLOCK), I32)],
        compiler_params=_cparams(("arbitrary", "arbitrary"), 56),
        name="dsa_prompt",
    )(iqh, iwt, ik3, qh, kh, vt, tt)


def _sidx_kernel(pt_ref, iq_ref, iw_ref, *rest, npages):
    pages, ikn_ref, o_ref = rest[:npages], rest[npages], rest[npages + 1]
    iq, iw = iq_ref[0], iw_ref[0]
    nq = iq.shape[0] // IDX_HEADS
    for j in range(npages + 1):
        keys_t = (pages[j][0, 0] if j < npages else ikn_ref[0]).astype(BF16)
        r = jnp.maximum(_dot(iq, keys_t), 0.0) * iw
        acc = r[0:nq]
        for h in range(1, IDX_HEADS):
            acc = acc + r[h * nq:(h + 1) * nq]
        o_ref[0, :, j * PAGE_SIZE:(j + 1) * PAGE_SIZE] = (acc * (IDX_DIM ** -0.5)) * (IDX_HEADS ** -0.5)


def _sample_scores(pt_flat, iqs, iws, cache_ik, layer, ikn, npages):
    bd, hq, _ = iqs.shape
    nq = hq // IDX_HEADS
    width = (npages + 1) * PAGE_SIZE
    page_spec = lambda p: pl.BlockSpec((1, 1, IDX_DIM, PAGE_SIZE),
                                       lambda i, pt, p=p: (layer, pt[i * npages + p], 0, 0))
    per_b = lambda a: pl.BlockSpec((1,) + a.shape[1:], lambda i, pt: (i, 0, 0))
    return pl.pallas_call(
        functools.partial(_sidx_kernel, npages=npages),
        grid_spec=pltpu.PrefetchScalarGridSpec(
            num_scalar_prefetch=1, grid=(bd,),
            in_specs=[per_b(iqs), per_b(iws)] + [page_spec(p) for p in range(npages)] + [per_b(ikn)],
            out_specs=pl.BlockSpec((1, nq, width), lambda i, pt: (i, 0, 0))),
        out_shape=jax.ShapeDtypeStruct((bd, nq, width), F32),
        compiler_params=_cparams(("arbitrary",)),
        name="sample_scores",
    )(pt_flat, iqs, iws, *([cache_ik] * npages), ikn)


def _sthr_kernel(s_ref, o_ref, k_scr, *, topk, past, nq):
    tr, width = s_ref.shape
    colw = lax.broadcasted_iota(I32, (tr, width), 1)
    rowq = lax.broadcasted_iota(I32, (tr, width), 0) % nq
    vis = colw <= past + rowq
    int_min = jnp.int32(INT_MIN)
    k_scr[...] = jnp.where(vis, _sortable(s_ref[...]), int_min)

    def count(pred):
        return jnp.sum(jnp.where(pred(k_scr[...]), 1.0, 0.0), axis=1, keepdims=True)

    def bit_body(i, ans_u):
        cand_u = ans_u | lax.shift_left(jnp.int32(1), 31 - i)
        cand_s = cand_u ^ int_min
        return jnp.where(count(lambda t: t >= cand_s) >= topk, cand_u, ans_u)

    thr = lax.fori_loop(0, 32, bit_body, jnp.zeros((tr, 1), I32)) ^ int_min
    need = topk - count(lambda t: t > thr)

    def jbit(i, jc):
        cand = jc | lax.shift_left(jnp.int32(1), 12 - i)
        cnt = count(lambda t: (t == thr) & (colw < cand))
        return jnp.where(cnt <= need, cand, jc)

    j_cut = lax.fori_loop(0, 13, jbit, jnp.zeros((tr, 1), I32))
    t = k_scr[...]
    sel = ((t > thr) | ((t == thr) & (colw < j_cut))) & vis
    o_ref[...] = jnp.where(sel, 0.0, MASK_NEG)


def _sample_mask(scores2, topk, past, nq):
    r, width = scores2.shape
    tr = min(r, 128)
    return pl.pallas_call(
        functools.partial(_sthr_kernel, topk=topk, past=past, nq=nq),
        grid=(r // tr,),
        in_specs=[pl.BlockSpec((tr, width), lambda i: (i, 0))],
        out_specs=pl.BlockSpec((tr, width), lambda i: (i, 0)),
        out_shape=jax.ShapeDtypeStruct((r, width), F32),
        scratch_shapes=[pltpu.VMEM((tr, width), I32)],
        compiler_params=_cparams(("arbitrary",)),
        name="sample_mask",
    )(scores2)


def _sattn_kernel(pt_ref, q_ref, m_ref, bfar_ref, bnear_ref, *rest, npages, nq):
    kpages, vpages = rest[:npages], rest[npages:2 * npages]
    kn_ref, vn_ref, o_ref, s_scr = rest[2 * npages:2 * npages + 4]
    q = q_ref[0]
    mx = jnp.full((B_HEADS * nq, PAGE_SIZE), MASK_NEG, F32)
    for j in range(npages + 1):
        kp = (kpages[j][0, 0] if j < npages else kn_ref[0]).astype(BF16)
        cols = slice(j * PAGE_SIZE, (j + 1) * PAGE_SIZE)
        bias = bfar_ref[...] if j < npages - 1 else bnear_ref[:, (j - npages + 1) * PAGE_SIZE:(j - npages + 2) * PAGE_SIZE]
        madd = jnp.concatenate([m_ref[0, :, cols]] * B_HEADS, axis=0)
        s = _dot(q, kp) * (HEAD_DIM ** -0.5) + bias + madd
        s_scr[:, cols] = s
        mx = jnp.maximum(mx, s)
    mrow = jnp.max(mx, axis=1, keepdims=True)
    lsum = jnp.zeros((B_HEADS * nq, PAGE_SIZE), F32)
    acc = jnp.zeros((B_HEADS * nq, B_WIDTH), F32)
    for j in range(npages + 1):
        vp = (vpages[j][0, 0] if j < npages else vn_ref[0]).astype(BF16)
        p = jnp.exp(s_scr[:, j * PAGE_SIZE:(j + 1) * PAGE_SIZE] - mrow)
        lsum = lsum + p
        acc = acc + _nt_dot(p.astype(BF16), vp)
    acc = acc / jnp.sum(lsum, axis=1, keepdims=True)
    head = lax.broadcasted_iota(I32, (nq, B_WIDTH), 1) // HEAD_DIM
    out = jnp.zeros((nq, B_WIDTH), F32)
    for h in range(B_HEADS):
        out = jnp.where(head == h, acc[h * nq:(h + 1) * nq], out)
    o_ref[0] = out.astype(BF16)


def _sample_attend(pt_flat, qbd, madd3, bfar, bnear, cache_k, cache_v, layer, kn, vn, npages):
    bd, hq, _ = qbd.shape
    nq = hq // B_HEADS
    width = (npages + 1) * PAGE_SIZE
    page_spec = lambda p: pl.BlockSpec((1, 1, B_WIDTH, PAGE_SIZE),
                                       lambda i, pt, p=p: (layer, pt[i * npages + p], 0, 0))
    per_b = lambda a: pl.BlockSpec((1,) + a.shape[1:], lambda i, pt: (i, 0, 0))
    const2 = lambda a: pl.BlockSpec(a.shape, lambda i, pt: (0, 0))
    pages = [page_spec(p) for p in range(npages)]
    return pl.pallas_call(
        functools.partial(_sattn_kernel, npages=npages, nq=nq),
        grid_spec=pltpu.PrefetchScalarGridSpec(
            num_scalar_prefetch=1, grid=(bd,),
            in_specs=[per_b(qbd), per_b(madd3), const2(bfar), const2(bnear)] + pages + pages + [per_b(kn), per_b(vn)],
            out_specs=pl.BlockSpec((1, nq, B_WIDTH), lambda i, pt: (i, 0, 0)),
            scratch_shapes=[pltpu.VMEM((hq, width), F32)]),
        out_shape=jax.ShapeDtypeStruct((bd, nq, B_WIDTH), BF16),
        compiler_params=_cparams(("arbitrary",), 56),
        name="sample_attend",
    )(pt_flat, qbd, madd3, bfar, bnear, *([cache_k] * npages), *([cache_v] * npages), kn, vn)


def _outproj_kernel(a_ref, b_ref, c_ref, wa_ref, wb_ref, wc_ref, x_ref, g_ref, bb_ref, o_ref):
    mix = _dot(a_ref[...], wa_ref[...]) + _dot(b_ref[...], wb_ref[...]) + _dot(c_ref[...], wc_ref[...])
    o_ref[...] = _layer_norm_rows(ALPHA * x_ref[...] + mix, g_ref[...], bb_ref[...])


def _outproj_ln(a, b, c, wa, wb, wc, x2, gain, bias):
    n = x2.shape[0]
    tm = 512
    rows = lambda a_: pl.BlockSpec((tm, a_.shape[1]), lambda i: (i, 0))
    full2 = lambda a_: pl.BlockSpec(a_.shape, lambda i: (0, 0))
    return pl.pallas_call(
        _outproj_kernel,
        grid=(n // tm,),
        in_specs=[rows(a), rows(b), rows(c), full2(wa), full2(wb), full2(wc), rows(x2), full2(gain), full2(bias)],
        out_specs=pl.BlockSpec((tm, D_MODEL), lambda i: (i, 0)),
        out_shape=jax.ShapeDtypeStruct((n, D_MODEL), F32),
        compiler_params=_cparams(("arbitrary",), 40),
        name="outproj_ln1",
    )(a, b, c, wa, wb, wc, x2, gain, bias)


def _router_kernel(x_ref, wh_ref, wl_ref, b_ref, o_ref):
    x = x_ref[...]
    xh = x.astype(BF16)
    xl = (x - xh.astype(F32)).astype(BF16)
    logits = _dot(xh, wh_ref[...]) + _dot(xh, wl_ref[...]) + _dot(xl, wh_ref[...]) + b_ref[...]
    lane = lax.broadcasted_iota(I32, logits.shape, 1).astype(F32)
    neg = -jnp.inf
    first_lane = lambda m: jnp.min(jnp.where(m, lane, float(LANES)), axis=1, keepdims=True)
    is_g = (lane >= N_EXPERTS) & (lane < N_EXPERTS + N_GROUPS)
    gmax = jnp.max(jnp.where(is_g, logits, neg), axis=1, keepdims=True)
    g_sel = first_lane(is_g & (logits == gmax)) - N_EXPERTS
    p_group = 1.0 / jnp.sum(jnp.where(is_g, jnp.exp(logits - gmax), 0.0), axis=1, keepdims=True)
    in_g = jnp.floor(lane * (1.0 / EXPERTS_PER_GROUP)) == g_sel
    ev = jnp.where(in_g, logits, neg)
    v1 = jnp.max(ev, axis=1, keepdims=True)
    i1 = first_lane(in_g & (logits == v1))
    ev2 = jnp.where(lane == i1, neg, ev)
    v2 = jnp.max(ev2, axis=1, keepdims=True)
    i2 = first_lane(in_g & (lane != i1) & (logits == v2))
    e2 = jnp.exp(v2 - v1)
    den = 1.0 + e2
    g1 = (1.0 / den) * p_group
    g2 = (e2 / den) * p_group
    o_ref[...] = jnp.where(lane == i1, g1, jnp.where(lane == i2, g2, 0.0))


def _router(x2, wh, wl, bias):
    n = x2.shape[0]
    tm = 512
    full2 = lambda a_: pl.BlockSpec(a_.shape, lambda i: (0, 0))
    return pl.pallas_call(
        _router_kernel,
        grid=(n // tm,),
        in_specs=[pl.BlockSpec((tm, D_MODEL), lambda i: (i, 0)), full2(wh), full2(wl), full2(bias)],
        out_specs=pl.BlockSpec((tm, LANES), lambda i: (i, 0)),
        out_shape=jax.ShapeDtypeStruct((n, LANES), F32),
        compiler_params=_cparams(("arbitrary",)),
        name="router",
    )(x2, wh, wl, bias)


def _moe_kernel(x_ref, cmb_ref, wgu_ref, wd_ref, g_ref, b_ref, o_ref, xb_scr, acc_scr):
    e = pl.program_id(1)

    @pl.when(e == 0)
    def _():
        xb_scr[...] = x_ref[...].astype(BF16)
        acc_scr[...] = jnp.zeros(acc_scr.shape, F32)

    gu = _dot(xb_scr[...], wgu_ref[0])
    cmb = cmb_ref[...]
    lane = lax.broadcasted_iota(I32, cmb.shape, 1)
    ce = jnp.sum(jnp.where(lane == e, cmb, 0.0), axis=1, keepdims=True)
    hid = _silu(gu[:, :EXPERT_FF]) * gu[:, EXPERT_FF:] * ce
    acc_scr[...] += _dot(hid.astype(BF16), wd_ref[0])

    @pl.when(e == pl.num_programs(1) - 1)
    def _():
        o_ref[...] = _layer_norm_rows(ALPHA * x_ref[...] + acc_scr[...], g_ref[...], b_ref[...])


def _moe_ln(x2, cmb, wgu, wd, gain, bias):
    n = x2.shape[0]
    tm = 1024 if n > 1024 else 512
    full2 = lambda a_: pl.BlockSpec(a_.shape, lambda i, e: (0, 0))
    return pl.pallas_call(
        _moe_kernel,
        grid=(n // tm, N_EXPERTS),
        in_specs=[pl.BlockSpec((tm, D_MODEL), lambda i, e: (i, 0)),
                  pl.BlockSpec((tm, LANES), lambda i, e: (i, 0)),
                  pl.BlockSpec((1, D_MODEL, 2 * EXPERT_FF), lambda i, e: (e, 0, 0)),
                  pl.BlockSpec((1, EXPERT_FF, D_MODEL), lambda i, e: (e, 0, 0)),
                  full2(gain), full2(bias)],
        out_specs=pl.BlockSpec((tm, D_MODEL), lambda i, e: (i, 0)),
        out_shape=jax.ShapeDtypeStruct((n, D_MODEL), F32),
        scratch_shapes=[pltpu.VMEM((tm, D_MODEL), BF16), pltpu.VMEM((tm, D_MODEL), F32)],
        compiler_params=_cparams(("arbitrary", "arbitrary"), 48),
        name="moe_ln2",
    )(x2, cmb, wgu, wd, gain, bias)


def _layer_weights(l, w_in, w_out, gv_gain, gv_bias, ws, bs, ret_gain, ret_bias, ln1_g, ln1_b, ln2_g, ln2_b,
                   rg_w, rg_b, re_w, re_b, e_gate, e_up, e_down):
    w = w_in[l]
    z = lambda k: jnp.zeros((D_MODEL, k), w.dtype)
    ik_end = 2560 + IDX_DIM
    iw_end = ik_end + IDX_HEADS
    w_pad = jnp.concatenate([w[:, :ik_end], z(COL_IW - COL_IK - IDX_DIM), w[:, ik_end:iw_end],
                             z(COL_CQ - COL_IW - IDX_HEADS), w[:, iw_end:]], axis=1).astype(BF16)
    wo = w_out[l].astype(BF16)
    wr = jnp.concatenate([re_w[l], rg_w[l], jnp.zeros((D_MODEL, LANES - N_EXPERTS - N_GROUPS), F32)], axis=1)
    wrh = wr.astype(BF16)
    wrl = (wr - wrh.astype(F32)).astype(BF16)
    br = jnp.concatenate([re_b[l], rg_b[l], jnp.zeros((LANES - N_EXPERTS - N_GROUPS,), F32)])[None, :]
    row = lambda a: a[l].reshape(1, -1)
    return dict(
        w_pad=w_pad, wa=wo[:A_WIDTH], wb=wo[A_WIDTH:A_WIDTH + B_WIDTH], wc=wo[A_WIDTH + B_WIDTH:],
        gv_gain=row(gv_gain), gv_bias=row(gv_bias), ws=ws[l], bs=bs[l],
        ret_gain=row(ret_gain), ret_bias=row(ret_bias),
        ln1_g=row(ln1_g), ln1_b=row(ln1_b), ln2_g=row(ln2_g), ln2_b=row(ln2_b),
        wrh=wrh, wrl=wrl, br=br,
        wgu=jnp.concatenate([e_gate[l], e_up[l]], axis=2).astype(BF16), wd=e_down[l].astype(BF16))


def _channel_mix(x2, a_out, b_out, c_out, lw):
    x1 = _outproj_ln(a_out, b_out, c_out, lw["wa"], lw["wb"], lw["wc"], x2, lw["ln1_g"], lw["ln1_b"])
    cmb = _router(x1, lw["wrh"], lw["wrl"], lw["br"])
    return _moe_ln(x1, cmb, lw["wgu"], lw["wd"], lw["ln2_g"], lw["ln2_b"])


def _heads(t2, b, l, nh):
    return t2.reshape(b, l, nh, t2.shape[-1] // nh)


def _prompt_layer(x3, lw, tt, cosf, sins):
    b, l, _ = x3.shape
    x2 = x3.reshape(b * l, D_MODEL)
    ha, kt, vt, vtb, ikt, kh, qh, iqh, ikb, iwt = _proj_prompt(x3, lw["w_pad"])
    bsb = jnp.repeat(lw["bs"].T, HEAD_DIM, axis=1)
    a_out, a_vn = _gmlp(ha, PCOL_AU, PCOL_AV, lw["gv_gain"], lw["gv_bias"], lw["ws"], bsb)
    topk = min(TOPK_MAX, l // 4)
    b_out = _dsa_prompt(iqh, iwt, ikb, qh, kh, vtb.reshape(b, B_HEADS, HEAD_DIM, l), tt, topk)
    s0e = jnp.zeros((b, C_WIDTH, C_WIDTH), F32)
    c_out, s_fin = _retention(ha.reshape(b, l, P_WIDTH), (PCOL_CQ, PCOL_CK, PCOL_CV, PCOL_CG), cosf, sins,
                              lw["ret_gain"], lw["ret_bias"], s0e, RET_CHUNK, 1)
    y = _channel_mix(x2, a_out, b_out.reshape(b * l, B_WIDTH), c_out.reshape(b * l, C_WIDTH), lw)
    last = ((l - 1) // CHUNK) * CHUNK
    to_rows = lambda t: jnp.transpose(t.reshape(b, B_HEADS, HEAD_DIM, l), (0, 3, 1, 2))
    state = (to_rows(kt), to_rows(vt), jnp.transpose(ikt, (0, 2, 1)), _state_extract(s_fin),
             a_vn.reshape(b, l, A_WIDTH)[:, last:])
    return y.reshape(b, l, D_MODEL), state


def _sample_layer(x3, lw, layer, cache_k, cache_v, cache_ik, state_l, pt_flat, npages, bfar, bnear, cosf, sins):
    bd, t, _ = x3.shape
    n = bd * t
    past = npages * PAGE_SIZE
    x2 = x3.reshape(n, D_MODEL)
    h2 = _proj(x2, lw["w_pad"])
    col = lambda c0, wdt: h2[:, c0:c0 + wdt]
    rep = CHUNK // t
    eye = jnp.eye(rep, dtype=F32)
    ws_t = lw["ws"][:, :t, :t]
    ws_bd = (eye[None, :, None, :, None] * ws_t[:, None, :, None, :]).reshape(A_GROUPS, CHUNK, CHUNK)
    bsb = jnp.repeat(jnp.tile(lw["bs"][:, :t], (1, rep)).T, HEAD_DIM, axis=1)
    a_out, a_vn = _gmlp(h2, COL_AU, COL_AV, lw["gv_gain"], lw["gv_bias"], ws_bd, bsb)
    k4 = _heads(col(COL_K, B_WIDTH), bd, t, B_HEADS)
    v4 = _heads(col(COL_V, B_WIDTH), bd, t, B_HEADS)
    ik3 = col(COL_IK, IDX_DIM).reshape(bd, t, IDX_DIM)
    feat_major = lambda a3: jnp.pad(jnp.transpose(a3, (0, 2, 1)), ((0, 0), (0, 0), (0, PAGE_SIZE - t)))
    iq4 = _heads(col(COL_IQ, IDX_HEADS * IDX_DIM), bd, t, IDX_HEADS)
    iqs = jnp.transpose(iq4, (0, 2, 1, 3)).reshape(bd, IDX_HEADS * t, IDX_DIM).astype(BF16)
    iw3 = jnp.transpose(col(COL_IW, IDX_HEADS).reshape(bd, t, IDX_HEADS), (0, 2, 1))
    iws = jnp.broadcast_to(iw3.reshape(bd, IDX_HEADS * t, 1), (bd, IDX_HEADS * t, LANES))
    scores = _sample_scores(pt_flat, iqs, iws, jnp.transpose(cache_ik, (0, 1, 3, 2)), layer, feat_major(ik3), npages)
    width = (npages + 1) * PAGE_SIZE
    topk = min(TOPK_MAX, (past + t) // 4)
    madd = _sample_mask(scores.reshape(n, width), topk, past, t).reshape(bd, t, width)
    q4 = _heads(col(COL_Q, B_WIDTH), bd, t, B_HEADS)
    eye_h = jnp.eye(B_HEADS, dtype=F32)
    qbd = (jnp.transpose(q4, (0, 2, 1, 3))[:, :, :, None, :] * eye_h[None, :, None, :, None])
    qbd = qbd.reshape(bd, B_HEADS * t, B_WIDTH).astype(BF16)
    page_view = lambda c: jnp.transpose(c, (0, 1, 3, 4, 2)).reshape(c.shape[0], c.shape[1], B_WIDTH, PAGE_SIZE)
    b_out = _sample_attend(pt_flat, qbd, madd, bfar, bnear, page_view(cache_k), page_view(cache_v), layer,
                           feat_major(k4.reshape(bd, t, B_WIDTH)), feat_major(v4.reshape(bd, t, B_WIDTH)), npages)
    c_out, s_fin = _retention(h2.reshape(bd, t, H_WIDTH), (COL_CQ, COL_CK, COL_CV, COL_CG), cosf, sins,
                              lw["ret_gain"], lw["ret_bias"], _state_embed(state_l), t, 8)
    y = _channel_mix(x2, a_out, b_out.reshape(n, B_WIDTH), c_out.reshape(n, C_WIDTH), lw)
    state = (k4, v4, ik3, _state_extract(s_fin), a_vn.reshape(bd, t, A_WIDTH))
    return y.reshape(bd, t, D_MODEL), state


def _distance_tables(rel_bias, t, past):
    r = jnp.arange(Q_BLOCK, dtype=I32)
    d_prompt = jnp.concatenate([off * Q_BLOCK + r[None, :] - r[:, None] for off in range(3)], axis=0)
    qpos = past + jnp.arange(t, dtype=I32)
    near0 = past - PAGE_SIZE
    d_near = [qpos[:, None] - (near0 + half * PAGE_SIZE + r[None, :]) for half in range(2)]
    d_far = jnp.full((t, LANES), MAX_DISTANCE * 2, I32)
    tables = _bias_tables(jnp.concatenate([d_prompt] + d_near + [d_far], axis=0), rel_bias)
    n0 = 3 * Q_BLOCK
    tt = tables[:, :n0].reshape(B_HEADS, 3, Q_BLOCK, Q_BLOCK)
    bnear = jnp.concatenate([tables[:, n0:n0 + t], tables[:, n0 + t:n0 + 2 * t]], axis=2).reshape(B_HEADS * t, 2 * LANES)
    bfar = tables[:, n0 + 2 * t:n0 + 3 * t].reshape(B_HEADS * t, LANES)
    return tt, bnear, bfar


def kernel(x_prompt, x_sample, cache_k, cache_v, cache_idx_k, state_ret, page_table, w_in, w_out, gmlp_v_gain,
           gmlp_v_bias, gmlp_ws, gmlp_bs, rel_bias, ret_gn_gain, ret_gn_bias, ln1_gain, ln1_bias, ln2_gain, ln2_bias,
           router_group_w, router_group_b, router_expert_w, router_expert_b, expert_w_gate, expert_w_up,
           expert_w_down):
    depth = w_in.shape[0]
    seq = x_prompt.shape[1]
    bd, t, _ = x_sample.shape
    npages = page_table.shape[1]
    past = npages * PAGE_SIZE
    pt_flat = page_table.reshape(-1).astype(I32)
    tt, bnear, bfar = _distance_tables(rel_bias, t, past)
    cos_p, sin_p = _rope_tables(jnp.arange(seq, dtype=I32))
    cos_s, sin_s = _rope_tables(past + jnp.arange(t, dtype=I32))
    xp, xs = x_prompt, x_sample
    st_p, st_s = [], []
    for l in range(depth):
        lw = _layer_weights(l, w_in, w_out, gmlp_v_gain, gmlp_v_bias, gmlp_ws, gmlp_bs, ret_gn_gain, ret_gn_bias,
                            ln1_gain, ln1_bias, ln2_gain, ln2_bias, router_group_w, router_group_b,
                            router_expert_w, router_expert_b, expert_w_gate, expert_w_up, expert_w_down)
        xp, sp = _prompt_layer(xp, lw, tt, cos_p, sin_p)
        xs, ss = _sample_layer(xs, lw, l, cache_k, cache_v, cache_idx_k, state_ret[l], pt_flat, npages,
                               bfar, bnear, cos_s, sin_s)
        st_p.append(sp)
        st_s.append(ss)
    stk = lambda sts, i: jnp.stack([s[i] for s in sts], axis=0)
    return (xp, xs,
            stk(st_p, 0), stk(st_p, 1), stk(st_p, 2), stk(st_p, 3), stk(st_p, 4),
            stk(st_s, 0), stk(st_s, 1), stk(st_s, 2), stk(st_s, 3), stk(st_s, 4))
```

```python
import functools
import math

import jax
import jax.numpy as jnp
from jax import lax
from jax.experimental import pallas as pl
from jax.experimental.pallas import tpu as pltpu

D_MODEL = 1024
HEAD_DIM = 64
A_GROUPS = 4
A_WIDTH = 256
CHUNK = 128
B_HEADS = 8
B_WIDTH = 512
IDX_HEADS = 8
IDX_DIM = 64
TOPK_MAX = 256
Q_BLOCK = 128
KEY_TILE = 256
PHASE_UNROLL = 2
NUM_BUCKETS = 32
MAX_DISTANCE = 128
C_HEADS = 4
C_WIDTH = 256
RET_CHUNK = 128
ROPE_BASE = 10000.0
PAGE_SIZE = 128
N_GROUPS = 4
EXPERTS_PER_GROUP = 8
N_EXPERTS = 32
EXPERT_FF = 256
DEPTH = 2
ALPHA = (2 * DEPTH) ** 0.25
LN_EPS = 1e-5

F32 = jnp.float32
BF16 = jnp.bfloat16
I32 = jnp.int32
I16 = jnp.int16
LANES = 128
MASK_NEG = -1e30
INT_MIN = -2 ** 31

COL_AU, COL_AV, COL_Q, COL_K, COL_V, COL_IQ, COL_IK, COL_IW = 0, 256, 512, 1024, 1536, 2048, 2560, 2688
COL_CQ, COL_CK, COL_CV, COL_CG = 2816, 3072, 3328, 3584
H_WIDTH = 3840


def _cparams(sem, vmem_mb=None):
    kw = dict(dimension_semantics=sem)
    if vmem_mb is not None:
        kw["vmem_limit_bytes"] = vmem_mb << 20
    return pltpu.CompilerParams(**kw)


def _nt_dot(a, b):
    return lax.dot_general(a, b, (((1,), (1,)), ((), ())), preferred_element_type=F32)


def _dot(a, b):
    return jnp.dot(a, b, preferred_element_type=F32)


def _layer_norm_rows(x, gain, bias):
    mu = jnp.mean(x, axis=-1, keepdims=True)
    xc = x - mu
    var = jnp.mean(xc * xc, axis=-1, keepdims=True)
    return xc * lax.rsqrt(var + LN_EPS) * gain + bias


def _silu(x):
    return x * (1.0 / (1.0 + jnp.exp(-x)))


def _proj_kernel(x_ref, w_ref, o_ref):
    o_ref[...] = _dot(x_ref[...].astype(BF16), w_ref[...])


def _proj(x2, w_pad):
    n = x2.shape[0]
    tm = 1024 if n > 1024 else 512
    tn = 768
    return pl.pallas_call(
        _proj_kernel,
        grid=(n // tm, H_WIDTH // tn),
        in_specs=[pl.BlockSpec((tm, D_MODEL), lambda i, j: (i, 0)),
                  pl.BlockSpec((D_MODEL, tn), lambda i, j: (0, j))],
        out_specs=pl.BlockSpec((tm, tn), lambda i, j: (i, j)),
        out_shape=jax.ShapeDtypeStruct((n, H_WIDTH), F32),
        compiler_params=_cparams(("arbitrary", "arbitrary"), 40),
        name="in_proj",
    )(x2, w_pad)


PCOL_AU, PCOL_AV, PCOL_CQ, PCOL_CK, PCOL_CV, PCOL_CG = 0, 256, 512, 768, 1024, 1280
P_WIDTH = 1536


def _proj_prompt_kernel(x_ref, w_ref, ha_ref, kt_ref, vt_ref, vtb_ref, ikt_ref, kh_ref, qh_ref, iqh_ref, ikb_ref,
                        iwt_ref):
    x = x_ref[0].astype(BF16)
    piece = lambda c0, width: _dot(x, w_ref[:, c0:c0 + width])
    ha_ref[:, :COL_Q] = piece(COL_AU, COL_Q)
    ha_ref[:, COL_Q:] = piece(COL_CQ, H_WIDTH - COL_CQ)

    def head_major(t, ref):
        for h in range(t.shape[1] // HEAD_DIM):
            ref[0, h] = t[:, h * HEAD_DIM:(h + 1) * HEAD_DIM].astype(BF16)

    head_major(piece(COL_Q, B_WIDTH) * (HEAD_DIM ** -0.5 * LOG2E), qh_ref)
    head_major(piece(COL_IQ, IDX_HEADS * IDX_DIM), iqh_ref)
    k = piece(COL_K, B_WIDTH)
    head_major(k, kh_ref)
    kt_ref[0] = jnp.transpose(k)
    vt = jnp.transpose(piece(COL_V, B_WIDTH))
    vt_ref[0] = vt
    vtb_ref[0] = vt.astype(BF16)
    ik = piece(COL_IK, LANES)
    ikb_ref[0] = ik[:, :IDX_DIM].astype(BF16)
    ikt_ref[0] = jnp.transpose(ik)[:IDX_DIM]
    iwt_ref[0] = jnp.transpose(piece(COL_IW, LANES))[:IDX_HEADS]


def _proj_prompt(x3, w_pad):
    b, l, _ = x3.shape
    tm = 512
    nt = l // tm
    f32s = lambda *shape: jax.ShapeDtypeStruct(shape, F32)
    bf16s = lambda *shape: jax.ShapeDtypeStruct(shape, BF16)
    feat = lambda rows: pl.BlockSpec((1, rows, tm), lambda i, j: (i, 0, j))
    hmaj = pl.BlockSpec((1, B_HEADS, tm, HEAD_DIM), lambda i, j: (i, 0, j, 0))
    return pl.pallas_call(
        _proj_prompt_kernel,
        grid=(b, nt),
        in_specs=[pl.BlockSpec((1, tm, D_MODEL), lambda i, j: (i, j, 0)),
                  pl.BlockSpec(w_pad.shape, lambda i, j: (0, 0), pipeline_mode=pl.Buffered(1))],
        out_specs=[pl.BlockSpec((tm, P_WIDTH), lambda i, j: (i * nt + j, 0)),
                   feat(B_WIDTH), feat(B_WIDTH), feat(B_WIDTH), feat(IDX_DIM),
                   hmaj, hmaj, hmaj,
                   pl.BlockSpec((1, tm, IDX_DIM), lambda i, j: (i, j, 0)),
                   feat(IDX_HEADS)],
        out_shape=[f32s(b * l, P_WIDTH), f32s(b, B_WIDTH, l), f32s(b, B_WIDTH, l), bf16s(b, B_WIDTH, l),
                   f32s(b, IDX_DIM, l), bf16s(b, B_HEADS, l, HEAD_DIM), bf16s(b, B_HEADS, l, HEAD_DIM),
                   bf16s(b, IDX_HEADS, l, IDX_DIM), bf16s(b, l, IDX_DIM), f32s(b, IDX_HEADS, l)],
        compiler_params=_cparams(("arbitrary", "arbitrary"), 48),
        name="in_proj_prompt",
    )(x3, w_pad)


LOG2E = math.log2(math.e)


def _bias_kernel(rb_ref, d_ref, o_ref, *, log2_rows):
    n = jnp.maximum(d_ref[...], 0)
    max_exact = NUM_BUCKETS // 2
    nf = jnp.maximum(n, 1).astype(F32)
    large = max_exact + (jnp.log(nf / max_exact) / math.log(MAX_DISTANCE / max_exact)
                         * (NUM_BUCKETS - max_exact)).astype(I32)
    large = jnp.minimum(large, NUM_BUCKETS - 1)
    bucket = jnp.where(n < max_exact, n, large)
    for h in range(B_HEADS):
        acc = jnp.zeros(bucket.shape, F32)
        for bk in range(NUM_BUCKETS):
            acc = jnp.where(bucket == bk, rb_ref[bk * B_HEADS + h], acc)
        o_ref[h, :log2_rows] = acc[:log2_rows] * LOG2E
        o_ref[h, log2_rows:] = acc[log2_rows:]


def _bias_tables(dist, rel_bias, log2_rows):
    r = dist.shape[0]
    return pl.pallas_call(
        functools.partial(_bias_kernel, log2_rows=log2_rows),
        in_specs=[pl.BlockSpec(memory_space=pltpu.SMEM),
                  pl.BlockSpec((r, LANES), lambda: (0, 0))],
        out_specs=pl.BlockSpec((B_HEADS, r, LANES), lambda: (0, 0, 0)),
        out_shape=jax.ShapeDtypeStruct((B_HEADS, r, LANES), F32),
        name="bias_tables",
    )(rel_bias.reshape(-1), dist)


def _gmlp_kernel(u_ref, v_ref, g_ref, b_ref, ws_ref, bsb_ref, o_ref, vn_ref, *, nchunk):
    r = lax.broadcasted_iota(I32, (CHUNK, CHUNK), 0)
    c = lax.broadcasted_iota(I32, (CHUNK, CHUNK), 1)
    grp = lax.broadcasted_iota(I32, (CHUNK, A_WIDTH), 1) // HEAD_DIM
    wts = [jnp.where(r >= c, ws_ref[g], 0.0).astype(BF16) for g in range(A_GROUPS)]
    gain, bias, bsb = g_ref[...], b_ref[...], bsb_ref[...]
    for ci in range(nchunk):
        sl = pl.ds(ci * CHUNK, CHUNK)
        vn = _layer_norm_rows(v_ref[sl, :], gain, bias)
        vn_ref[sl, :] = vn
        vb = vn.astype(BF16)
        mixed = bsb
        for g in range(A_GROUPS):
            mixed = mixed + jnp.where(grp == g, _dot(wts[g], vb), 0.0)
        o_ref[sl, :] = (u_ref[sl, :] * mixed).astype(BF16)


def _gmlp(h2, col_u, col_v, gain, bias, ws, bsb):
    n = h2.shape[0]
    tm = min(n, 1024)
    blk = lambda col: pl.BlockSpec((tm, A_WIDTH), lambda i: (i, col // A_WIDTH))
    full2 = lambda a: pl.BlockSpec(a.shape, lambda i: (0, 0))
    return pl.pallas_call(
        functools.partial(_gmlp_kernel, nchunk=tm // CHUNK),
        grid=(n // tm,),
        in_specs=[blk(col_u), blk(col_v), full2(gain), full2(bias),
                  pl.BlockSpec(ws.shape, lambda i: (0, 0, 0)), full2(bsb)],
        out_specs=[pl.BlockSpec((tm, A_WIDTH), lambda i: (i, 0)),
                   pl.BlockSpec((tm, A_WIDTH), lambda i: (i, 0))],
        out_shape=[jax.ShapeDtypeStruct((n, A_WIDTH), BF16),
                   jax.ShapeDtypeStruct((n, A_WIDTH), F32)],
        compiler_params=_cparams(("arbitrary",)),
        name="gmlp",
    )(h2, h2, gain, bias, ws, bsb)


def _ret_tables(c):
    log_g = jnp.log(1.0 - 2.0 ** (-5.0 - jnp.arange(C_HEADS, dtype=F32)))
    i = jnp.arange(c, dtype=F32)
    diff = i[:, None] - i[None, :]
    dmat = jnp.where(diff >= 0, jnp.exp(log_g[:, None, None] * jnp.maximum(diff, 0.0)), 0.0)
    q_dec = jnp.exp(log_g[:, None] * (i + 1.0))
    k_dec = jnp.exp(log_g[:, None] * (c - 1.0 - i))
    s_dec = jnp.exp(log_g * c)
    qd = jnp.repeat(q_dec.T, HEAD_DIM, axis=1)
    kd = jnp.repeat(k_dec.T, HEAD_DIM, axis=1)
    hid = jnp.arange(C_WIDTH) // HEAD_DIM
    same = hid[:, None] == hid[None, :]
    sd = jnp.where(same, s_dec[hid][:, None], 0.0)
    return dmat, qd, kd, sd, same.astype(F32)


def _rope_tables(pos):
    half = HEAD_DIM // 2
    inv = ROPE_BASE ** (-jnp.arange(half, dtype=F32) / half)
    ang = pos.astype(F32)[:, None] * inv[None, :]
    cos, sin = jnp.cos(ang), jnp.sin(ang)
    cosf = jnp.tile(jnp.concatenate([cos, cos], axis=1), (1, C_HEADS))
    sins = jnp.tile(jnp.concatenate([-sin, sin], axis=1), (1, C_HEADS))
    return cosf, sins


def _ret_kernel(q_ref, k_ref, v_ref, g_ref, cos_ref, sin_ref, qd_ref, kd_ref, dm_ref, sd_ref, bd_ref,
                gg_ref, gb_ref, s0_ref, o_ref, sf_ref, s_scr, *, bt, c):
    ci = pl.program_id(1)

    @pl.when(ci == 0)
    def _():
        s_scr[...] = s0_ref[...]

    lane = lax.broadcasted_iota(I32, (c, C_WIDTH), 1)
    hid = lane // HEAD_DIM
    first = (lane % HEAD_DIM) < (HEAD_DIM // 2)
    cosf, sins = cos_ref[...], sin_ref[...]
    half = HEAD_DIM // 2

    def rot(x):
        partner = jnp.where(first, pltpu.roll(x, C_WIDTH - half, 1), pltpu.roll(x, half, 1))
        return x * cosf + partner * sins

    def seg_mean(x):
        out = jnp.zeros_like(x)
        for h in range(C_HEADS):
            hm = hid == h
            s = jnp.sum(jnp.where(hm, x, 0.0), axis=1, keepdims=True) * (1.0 / HEAD_DIM)
            out = jnp.where(hm, s, out)
        return out

    for bb in range(bt):
        q = rot(q_ref[bb])
        k = rot(k_ref[bb]) * (HEAD_DIM ** -0.5)
        v = v_ref[bb]
        kb, vb = k.astype(BF16), v.astype(BF16)
        s_old = s_scr[bb]
        o = _dot(q.astype(BF16), s_old.astype(BF16)) * qd_ref[...]
        for h in range(C_HEADS):
            hm = hid == h
            att = _nt_dot(jnp.where(hm, q, 0.0).astype(BF16), kb) * dm_ref[h]
            o = o + jnp.where(hm, _dot(att.astype(BF16), vb), 0.0)
        kdt = jnp.transpose(k * kd_ref[...]).astype(BF16)
        s_scr[bb] = s_old * sd_ref[...] + bd_ref[...] * _dot(kdt, vb)
        mu = seg_mean(o)
        oc = o - mu
        var = seg_mean(oc * oc)
        normed = oc * lax.rsqrt(var + LN_EPS) * gg_ref[...] + gb_ref[...]
        o_ref[bb] = (_silu(g_ref[bb]) * normed).astype(BF16)

    @pl.when(ci == pl.num_programs(1) - 1)
    def _():
        sf_ref[...] = s_scr[...]


def _retention(h3, cols, cosf, sins, gn_gain, gn_bias, s0e, c, bt):
    b, l, _ = h3.shape
    dmat, qd, kd, sd, bd = _ret_tables(c)
    blk = lambda col: pl.BlockSpec((bt, c, C_WIDTH), lambda i, j: (i, j, col // C_WIDTH))
    const2 = lambda a: pl.BlockSpec(a.shape, lambda i, j: (0, 0))
    pos_blk = pl.BlockSpec((c, C_WIDTH), lambda i, j: (j, 0))
    st_blk = pl.BlockSpec((bt, C_WIDTH, C_WIDTH), lambda i, j: (i, 0, 0))
    return pl.pallas_call(
        functools.partial(_ret_kernel, bt=bt, c=c),
        grid=(b // bt, l // c),
        in_specs=[blk(cols[0]), blk(cols[1]), blk(cols[2]), blk(cols[3]), pos_blk, pos_blk,
                  const2(qd), const2(kd), pl.BlockSpec(dmat.shape, lambda i, j: (0, 0, 0)),
                  const2(sd), const2(bd), const2(gn_gain), const2(gn_bias), st_blk],
        out_specs=[pl.BlockSpec((bt, c, C_WIDTH), lambda i, j: (i, j, 0)), st_blk],
        out_shape=[jax.ShapeDtypeStruct((b, l, C_WIDTH), BF16),
                   jax.ShapeDtypeStruct((b, C_WIDTH, C_WIDTH), F32)],
        scratch_shapes=[pltpu.VMEM((bt, C_WIDTH, C_WIDTH), F32)],
        compiler_params=_cparams(("arbitrary", "arbitrary")),
        name="retention",
    )(h3, h3, h3, h3, cosf, sins, qd, kd, dmat, sd, bd, gn_gain, gn_bias, s0e)


def _state_embed(s):
    b = s.shape[0]
    eye = jnp.eye(C_HEADS, dtype=s.dtype)
    return (s[:, :, :, None, :] * eye[None, :, None, :, None]).reshape(b, C_WIDTH, C_WIDTH)


def _state_extract(se):
    b = se.shape[0]
    s5 = se.reshape(b, C_HEADS, HEAD_DIM, C_HEADS, HEAD_DIM)
    return jnp.stack([s5[:, h, :, h, :] for h in range(C_HEADS)], axis=1)


def _sortable(score):
    bits = pltpu.bitcast(score, I32)
    return bits ^ ((bits >> 31) & 0x7FFFFFFF)


def _dsa_kernel(iq_ref, iw_ref, ik_ref, q_ref, k_ref, vt_ref, tt_ref, o_ref,
                sc_scr, hi_scr, lo_scr, s_scr, acc_scr, j_scr, *, topk):
    qb = pl.program_id(1)
    ntile = (qb + KEY_TILE // Q_BLOCK) // (KEY_TILE // Q_BLOCK)
    row = lax.broadcasted_iota(I32, (KEY_TILE, Q_BLOCK), 0)
    col = lax.broadcasted_iota(I32, (KEY_TILE, Q_BLOCK), 1)
    qpos = qb * Q_BLOCK + col
    iq_all = iq_ref[0].reshape(IDX_HEADS * Q_BLOCK, IDX_DIM)
    iw = iw_ref[0]
    int_min = jnp.int32(INT_MIN)
    score_scale = IDX_HEADS ** -0.5

    def tile_rows(j):
        return pl.ds(pl.multiple_of(j * KEY_TILE, KEY_TILE), KEY_TILE)

    def tree_sum(parts):
        while len(parts) > 1:
            parts = [a + b for a, b in zip(parts[::2], parts[1::2])]
        return parts[0]

    def score_tile(j, carry):
        s = _nt_dot(ik_ref[0, tile_rows(j), :].astype(BF16), iq_all)
        terms = [jnp.maximum(s[:, h * Q_BLOCK:(h + 1) * Q_BLOCK], 0.0) * iw[h:h + 1, :] for h in range(IDX_HEADS)]
        acc = terms[0]
        for term in terms[1:]:
            acc = acc + term
        score = (acc * (IDX_DIM ** -0.5)) * score_scale
        kint = jnp.where(j * KEY_TILE + row <= qpos, _sortable(score), int_min)
        sc_scr[tile_rows(j), :] = kint
        hi_scr[tile_rows(j), :] = (kint >> 16).astype(I16)
        return carry

    nstep = (ntile + PHASE_UNROLL - 1) // PHASE_UNROLL

    def unrolled(tile_fn):
        def step(js, carry):
            for u in range(PHASE_UNROLL):
                carry = tile_fn(js * PHASE_UNROLL + u, carry)
            return carry
        return step

    lax.fori_loop(0, nstep, unrolled(score_tile), 0)

    def count(pred):
        def body(j, acc):
            m = jnp.where(pred(sc_scr[tile_rows(j), :], j), 1.0, 0.0)
            return acc + tree_sum([m[i * 8:(i + 1) * 8] for i in range(KEY_TILE // 8)])
        acc = lax.fori_loop(0, ntile, body, jnp.zeros((8, Q_BLOCK), F32))
        return jnp.sum(acc, axis=0, keepdims=True)

    half_min = -(1 << 15)
    ones16, zeros16 = jnp.ones((16, Q_BLOCK), I16), jnp.zeros((16, Q_BLOCK), I16)

    def count16(ref, cand, strict=False):
        cand16 = jnp.broadcast_to(cand, (16, Q_BLOCK)).astype(I16)

        def tile(j, acc):
            x = ref[tile_rows(j), :]
            hits = []
            for i in range(KEY_TILE // 16):
                xi = x[i * 16:(i + 1) * 16]
                hits.append(jnp.where((xi > cand16) if strict else (xi >= cand16), ones16, zeros16))
            return acc + tree_sum(hits)

        acc = lax.fori_loop(0, nstep, unrolled(tile), zeros16)
        return jnp.sum(acc.astype(F32), axis=0, keepdims=True)

    def high_bit(i, ans):
        cand = ans | lax.shift_left(jnp.int32(1), 15 - i)
        return jnp.where(count16(hi_scr, cand + half_min) >= topk, cand, ans)

    t_hi = lax.fori_loop(0, 16, high_bit, jnp.zeros((1, Q_BLOCK), I32)) + half_min
    n_above = count16(hi_scr, t_hi, strict=True)

    def low_halves(j, carry):
        t = sc_scr[tile_rows(j), :]
        lo_scr[tile_rows(j), :] = jnp.where((t >> 16) == t_hi, (t & 0xFFFF) + half_min, half_min).astype(I16)
        return carry

    lax.fori_loop(0, nstep, unrolled(low_halves), 0)

    def low_bit(i, ans):
        cand = ans | lax.shift_left(jnp.int32(1), 15 - i)
        return jnp.where(n_above + count16(lo_scr, cand + half_min) >= topk, cand, ans)

    t_lo = lax.fori_loop(0, 16, low_bit, jnp.zeros((1, Q_BLOCK), I32))
    thr = lax.shift_left(t_hi, 16) | t_lo
    need = topk - count(lambda t, j: t > thr)
    n_ge = count(lambda t, j: t >= thr)

    j_scr[...] = jnp.full(j_scr.shape, 1 << 20, I32)
    has_tie = jnp.max(jnp.where((n_ge > topk) & (thr != int_min), 1.0, 0.0)) > 0.0

    @pl.when(has_tie)
    def _():
        def jbit(i, jc):
            cand = jc | lax.shift_left(jnp.int32(1), 12 - i)
            cnt = count(lambda t, j: (t == thr) & (j * KEY_TILE + row < cand))
            return jnp.where(cnt <= need, cand, jc)
        jc = lax.fori_loop(0, 13, jbit, jnp.zeros((1, Q_BLOCK), I32))
        j_scr[...] = jnp.broadcast_to(jc, j_scr.shape)

    j_cut = jnp.where(thr == int_min, 0, j_scr[0:1, :])

    q_scaled = [q_ref[0, h] for h in range(B_HEADS)]
    sub_tiles = KEY_TILE // Q_BLOCK
    groups = KEY_TILE // 8

    def tree_max(parts):
        while len(parts) > 1:
            parts = [jnp.maximum(a, b) for a, b in zip(parts[::2], parts[1::2])]
        return parts[0]

    def logits_tile(j, m_part):
        t = sc_scr[tile_rows(j), :]
        kidx = j * KEY_TILE + row
        madd = jnp.where(t > thr, 0.0, jnp.where(t == thr, jnp.where(kidx < j_cut, 0.0, MASK_NEG), MASK_NEG))
        offs = [jnp.clip(qb - (j * sub_tiles + i), 0, 2) for i in range(sub_tiles)]
        new = []
        for h in range(B_HEADS):
            bias = jnp.concatenate([tt_ref[h, off] for off in offs], axis=0)
            s = _nt_dot(k_ref[0, h, tile_rows(j), :], q_scaled[h]) + bias + madd
            s_scr[h, tile_rows(j), :] = s
            new.append(jnp.maximum(m_part[h], tree_max([s[i * 8:(i + 1) * 8] for i in range(groups)])))
        return jnp.stack(new)

    m_part = lax.fori_loop(0, nstep, unrolled(logits_tile), jnp.full((B_HEADS, 8, Q_BLOCK), MASK_NEG, F32))
    m_rows = [jnp.max(m_part[h], axis=0, keepdims=True) for h in range(B_HEADS)]

    acc_scr[...] = jnp.zeros(acc_scr.shape, F32)

    def values_tile(j, l_part):
        new = []
        for h in range(B_HEADS):
            p = jnp.exp2(s_scr[h, tile_rows(j), :] - m_rows[h])
            new.append(l_part[h] + tree_sum([p[i * 8:(i + 1) * 8] for i in range(groups)]))
            acc_scr[h] += _dot(vt_ref[0, h, :, tile_rows(j)], p.astype(BF16))
        return jnp.stack(new)

    l_part = lax.fori_loop(0, nstep, unrolled(values_tile), jnp.zeros((B_HEADS, 8, Q_BLOCK), F32))
    out_t = jnp.concatenate([acc_scr[h] / jnp.sum(l_part[h], axis=0, keepdims=True) for h in range(B_HEADS)], axis=0)
    o_ref[0] = jnp.transpose(out_t).astype(BF16)


def _dsa_prompt(iqh, iwt, ik3, qh, kh, vt, tt, topk):
    b, _, l, _ = qh.shape
    assert l % (KEY_TILE * PHASE_UNROLL) == 0
    qblk = pl.BlockSpec((1, B_HEADS, Q_BLOCK, HEAD_DIM), lambda i, j: (i, 0, j, 0))
    once = pl.Buffered(1)
    return pl.pallas_call(
        functools.partial(_dsa_kernel, topk=topk),
        grid=(b, l // Q_BLOCK),
        in_specs=[qblk,
                  pl.BlockSpec((1, IDX_HEADS, Q_BLOCK), lambda i, j: (i, 0, j)),
                  pl.BlockSpec((1, l, IDX_DIM), lambda i, j: (i, 0, 0), pipeline_mode=once),
                  qblk,
                  pl.BlockSpec((1, B_HEADS, l, HEAD_DIM), lambda i, j: (i, 0, 0, 0), pipeline_mode=once),
                  pl.BlockSpec((1, B_HEADS, HEAD_DIM, l), lambda i, j: (i, 0, 0, 0), pipeline_mode=once),
                  pl.BlockSpec(tt.shape, lambda i, j: (0, 0, 0, 0), pipeline_mode=once)],
        out_specs=pl.BlockSpec((1, Q_BLOCK, B_WIDTH), lambda i, j: (i, j, 0)),
        out_shape=jax.ShapeDtypeStruct((b, l, B_WIDTH), BF16),
        scratch_shapes=[pltpu.VMEM((l, Q_BLOCK), I32),
                        pltpu.VMEM((l, Q_BLOCK), I16),
                        pltpu.VMEM((l, Q_BLOCK), I16),
                        pltpu.VMEM((B_HEADS, l, Q_BLOCK), F32),
                        pltpu.VMEM((B_HEADS, HEAD_DIM, Q_BLOCK), F32),
                        pltpu.VMEM((8, Q_BLOCK), I32)],
        compiler_params=_cparams(("arbitrary", "arbitrary"), 56),
        name="dsa_prompt",
    )(iqh, iwt, ik3, qh, kh, vt, tt)


def _sidx_kernel(pt_ref, iq_ref, iw_ref, *rest, npages):
    pages, ikn_ref, o_ref = rest[:npages], rest[npages], rest[npages + 1]
    iq, iw = iq_ref[0], iw_ref[0]
    nq = iq.shape[0] // IDX_HEADS
    for j in range(npages + 1):
        keys_t = (pages[j][0, 0] if j < npages else ikn_ref[0]).astype(BF16)
        r = jnp.maximum(_dot(iq, keys_t), 0.0) * iw
        acc = r[0:nq]
        for h in range(1, IDX_HEADS):
            acc = acc + r[h * nq:(h + 1) * nq]
        o_ref[0, :, j * PAGE_SIZE:(j + 1) * PAGE_SIZE] = (acc * (IDX_DIM ** -0.5)) * (IDX_HEADS ** -0.5)


def _sample_scores(pt_flat, iqs, iws, cache_ik, layer, ikn, npages):
    bd, hq, _ = iqs.shape
    nq = hq // IDX_HEADS
    width = (npages + 1) * PAGE_SIZE
    page_spec = lambda p: pl.BlockSpec((1, 1, IDX_DIM, PAGE_SIZE),
                                       lambda i, pt, p=p: (layer, pt[i * npages + p], 0, 0))
    per_b = lambda a: pl.BlockSpec((1,) + a.shape[1:], lambda i, pt: (i, 0, 0))
    return pl.pallas_call(
        functools.partial(_sidx_kernel, npages=npages),
        grid_spec=pltpu.PrefetchScalarGridSpec(
            num_scalar_prefetch=1, grid=(bd,),
            in_specs=[per_b(iqs), per_b(iws)] + [page_spec(p) for p in range(npages)] + [per_b(ikn)],
            out_specs=pl.BlockSpec((1, nq, width), lambda i, pt: (i, 0, 0))),
        out_shape=jax.ShapeDtypeStruct((bd, nq, width), F32),
        compiler_params=_cparams(("arbitrary",)),
        name="sample_scores",
    )(pt_flat, iqs, iws, *([cache_ik] * npages), ikn)


def _sthr_kernel(s_ref, o_ref, k_scr, *, topk, past, nq):
    tr, width = s_ref.shape
    colw = lax.broadcasted_iota(I32, (tr, width), 1)
    rowq = lax.broadcasted_iota(I32, (tr, width), 0) % nq
    vis = colw <= past + rowq
    int_min = jnp.int32(INT_MIN)
    k_scr[...] = jnp.where(vis, _sortable(s_ref[...]), int_min)

    def count(pred):
        return jnp.sum(jnp.where(pred(k_scr[...]), 1.0, 0.0), axis=1, keepdims=True)

    def bit_body(i, ans_u):
        cand_u = ans_u | lax.shift_left(jnp.int32(1), 31 - i)
        cand_s = cand_u ^ int_min
        return jnp.where(count(lambda t: t >= cand_s) >= topk, cand_u, ans_u)

    thr = lax.fori_loop(0, 32, bit_body, jnp.zeros((tr, 1), I32)) ^ int_min
    need = topk - count(lambda t: t > thr)

    def jbit(i, jc):
        cand = jc | lax.shift_left(jnp.int32(1), 12 - i)
        cnt = count(lambda t: (t == thr) & (colw < cand))
        return jnp.where(cnt <= need, cand, jc)

    j_cut = lax.fori_loop(0, 13, jbit, jnp.zeros((tr, 1), I32))
    t = k_scr[...]
    sel = ((t > thr) | ((t == thr) & (colw < j_cut))) & vis
    o_ref[...] = jnp.where(sel, 0.0, MASK_NEG)


def _sample_mask(scores2, topk, past, nq):
    r, width = scores2.shape
    tr = min(r, 128)
    return pl.pallas_call(
        functools.partial(_sthr_kernel, topk=topk, past=past, nq=nq),
        grid=(r // tr,),
        in_specs=[pl.BlockSpec((tr, width), lambda i: (i, 0))],
        out_specs=pl.BlockSpec((tr, width), lambda i: (i, 0)),
        out_shape=jax.ShapeDtypeStruct((r, width), F32),
        scratch_shapes=[pltpu.VMEM((tr, width), I32)],
        compiler_params=_cparams(("arbitrary",)),
        name="sample_mask",
    )(scores2)


def _sattn_kernel(pt_ref, q_ref, m_ref, bfar_ref, bnear_ref, *rest, npages, nq):
    kpages, vpages = rest[:npages], rest[npages:2 * npages]
    kn_ref, vn_ref, o_ref, s_scr = rest[2 * npages:2 * npages + 4]
    q = q_ref[0]
    mx = jnp.full((B_HEADS * nq, PAGE_SIZE), MASK_NEG, F32)
    for j in range(npages + 1):
        kp = (kpages[j][0, 0] if j < npages else kn_ref[0]).astype(BF16)
        cols = slice(j * PAGE_SIZE, (j + 1) * PAGE_SIZE)
        bias = bfar_ref[...] if j < npages - 1 else bnear_ref[:, (j - npages + 1) * PAGE_SIZE:(j - npages + 2) * PAGE_SIZE]
        madd = jnp.concatenate([m_ref[0, :, cols]] * B_HEADS, axis=0)
        s = _dot(q, kp) * (HEAD_DIM ** -0.5) + bias + madd
        s_scr[:, cols] = s
        mx = jnp.maximum(mx, s)
    mrow = jnp.max(mx, axis=1, keepdims=True)
    lsum = jnp.zeros((B_HEADS * nq, PAGE_SIZE), F32)
    acc = jnp.zeros((B_HEADS * nq, B_WIDTH), F32)
    for j in range(npages + 1):
        vp = (vpages[j][0, 0] if j < npages else vn_ref[0]).astype(BF16)
        p = jnp.exp(s_scr[:, j * PAGE_SIZE:(j + 1) * PAGE_SIZE] - mrow)
        lsum = lsum + p
        acc = acc + _nt_dot(p.astype(BF16), vp)
    acc = acc / jnp.sum(lsum, axis=1, keepdims=True)
    head = lax.broadcasted_iota(I32, (nq, B_WIDTH), 1) // HEAD_DIM
    out = jnp.zeros((nq, B_WIDTH), F32)
    for h in range(B_HEADS):
        out = jnp.where(head == h, acc[h * nq:(h + 1) * nq], out)
    o_ref[0] = out.astype(BF16)


def _sample_attend(pt_flat, qbd, madd3, bfar, bnear, cache_k, cache_v, layer, kn, vn, npages):
    bd, hq, _ = qbd.shape
    nq = hq // B_HEADS
    width = (npages + 1) * PAGE_SIZE
    page_spec = lambda p: pl.BlockSpec((1, 1, B_WIDTH, PAGE_SIZE),
                                       lambda i, pt, p=p: (layer, pt[i * npages + p], 0, 0))
    per_b = lambda a: pl.BlockSpec((1,) + a.shape[1:], lambda i, pt: (i, 0, 0))
    const2 = lambda a: pl.BlockSpec(a.shape, lambda i, pt: (0, 0))
    pages = [page_spec(p) for p in range(npages)]
    return pl.pallas_call(
        functools.partial(_sattn_kernel, npages=npages, nq=nq),
        grid_spec=pltpu.PrefetchScalarGridSpec(
            num_scalar_prefetch=1, grid=(bd,),
            in_specs=[per_b(qbd), per_b(madd3), const2(bfar), const2(bnear)] + pages + pages + [per_b(kn), per_b(vn)],
            out_specs=pl.BlockSpec((1, nq, B_WIDTH), lambda i, pt: (i, 0, 0)),
            scratch_shapes=[pltpu.VMEM((hq, width), F32)]),
        out_shape=jax.ShapeDtypeStruct((bd, nq, B_WIDTH), BF16),
        compiler_params=_cparams(("arbitrary",), 56),
        name="sample_attend",
    )(pt_flat, qbd, madd3, bfar, bnear, *([cache_k] * npages), *([cache_v] * npages), kn, vn)


def _outproj_kernel(a_ref, b_ref, c_ref, wa_ref, wb_ref, wc_ref, x_ref, g_ref, bb_ref, o_ref):
    mix = _dot(a_ref[...], wa_ref[...]) + _dot(b_ref[...], wb_ref[...]) + _dot(c_ref[...], wc_ref[...])
    o_ref[...] = _layer_norm_rows(ALPHA * x_ref[...] + mix, g_ref[...], bb_ref[...])


def _outproj_ln(a, b, c, wa, wb, wc, x2, gain, bias):
    n = x2.shape[0]
    tm = 512
    rows = lambda a_: pl.BlockSpec((tm, a_.shape[1]), lambda i: (i, 0))
    full2 = lambda a_: pl.BlockSpec(a_.shape, lambda i: (0, 0))
    return pl.pallas_call(
        _outproj_kernel,
        grid=(n // tm,),
        in_specs=[rows(a), rows(b), rows(c), full2(wa), full2(wb), full2(wc), rows(x2), full2(gain), full2(bias)],
        out_specs=pl.BlockSpec((tm, D_MODEL), lambda i: (i, 0)),
        out_shape=jax.ShapeDtypeStruct((n, D_MODEL), F32),
        compiler_params=_cparams(("arbitrary",), 40),
        name="outproj_ln1",
    )(a, b, c, wa, wb, wc, x2, gain, bias)


def _router_kernel(x_ref, wh_ref, wl_ref, b_ref, o_ref):
    x = x_ref[...]
    xh = x.astype(BF16)
    xl = (x - xh.astype(F32)).astype(BF16)
    logits = _dot(xh, wh_ref[...]) + _dot(xh, wl_ref[...]) + _dot(xl, wh_ref[...]) + b_ref[...]
    lane = lax.broadcasted_iota(I32, logits.shape, 1).astype(F32)
    neg = -jnp.inf
    first_lane = lambda m: jnp.min(jnp.where(m, lane, float(LANES)), axis=1, keepdims=True)
    is_g = (lane >= N_EXPERTS) & (lane < N_EXPERTS + N_GROUPS)
    gmax = jnp.max(jnp.where(is_g, logits, neg), axis=1, keepdims=True)
    g_sel = first_lane(is_g & (logits == gmax)) - N_EXPERTS
    p_group = 1.0 / jnp.sum(jnp.where(is_g, jnp.exp(logits - gmax), 0.0), axis=1, keepdims=True)
    in_g = jnp.floor(lane * (1.0 / EXPERTS_PER_GROUP)) == g_sel
    ev = jnp.where(in_g, logits, neg)
    v1 = jnp.max(ev, axis=1, keepdims=True)
    i1 = first_lane(in_g & (logits == v1))
    ev2 = jnp.where(lane == i1, neg, ev)
    v2 = jnp.max(ev2, axis=1, keepdims=True)
    i2 = first_lane(in_g & (lane != i1) & (logits == v2))
    e2 = jnp.exp(v2 - v1)
    den = 1.0 + e2
    g1 = (1.0 / den) * p_group
    g2 = (e2 / den) * p_group
    o_ref[...] = jnp.where(lane == i1, g1, jnp.where(lane == i2, g2, 0.0))


def _router(x2, wh, wl, bias):
    n = x2.shape[0]
    tm = 512
    full2 = lambda a_: pl.BlockSpec(a_.shape, lambda i: (0, 0))
    return pl.pallas_call(
        _router_kernel,
        grid=(n // tm,),
        in_specs=[pl.BlockSpec((tm, D_MODEL), lambda i: (i, 0)), full2(wh), full2(wl), full2(bias)],
        out_specs=pl.BlockSpec((tm, LANES), lambda i: (i, 0)),
        out_shape=jax.ShapeDtypeStruct((n, LANES), F32),
        compiler_params=_cparams(("arbitrary",)),
        name="router",
    )(x2, wh, wl, bias)


def _moe_kernel(x_ref, cmb_ref, wgu_ref, wd_ref, g_ref, b_ref, o_ref, xb_scr, acc_scr):
    e = pl.program_id(1)

    @pl.when(e == 0)
    def _():
        xb_scr[...] = x_ref[...].astype(BF16)
        acc_scr[...] = jnp.zeros(acc_scr.shape, F32)

    gu = _dot(xb_scr[...], wgu_ref[0])
    cmb = cmb_ref[...]
    lane = lax.broadcasted_iota(I32, cmb.shape, 1)
    ce = jnp.sum(jnp.where(lane == e, cmb, 0.0), axis=1, keepdims=True)
    hid = _silu(gu[:, :EXPERT_FF]) * gu[:, EXPERT_FF:] * ce
    acc_scr[...] += _dot(hid.astype(BF16), wd_ref[0])

    @pl.when(e == pl.num_programs(1) - 1)
    def _():
        o_ref[...] = _layer_norm_rows(ALPHA * x_ref[...] + acc_scr[...], g_ref[...], b_ref[...])


def _moe_ln(x2, cmb, wgu, wd, gain, bias):
    n = x2.shape[0]
    tm = 1024 if n > 1024 else 512
    full2 = lambda a_: pl.BlockSpec(a_.shape, lambda i, e: (0, 0))
    return pl.pallas_call(
        _moe_kernel,
        grid=(n // tm, N_EXPERTS),
        in_specs=[pl.BlockSpec((tm, D_MODEL), lambda i, e: (i, 0)),
                  pl.BlockSpec((tm, LANES), lambda i, e: (i, 0)),
                  pl.BlockSpec((1, D_MODEL, 2 * EXPERT_FF), lambda i, e: (e, 0, 0)),
                  pl.BlockSpec((1, EXPERT_FF, D_MODEL), lambda i, e: (e, 0, 0)),
                  full2(gain), full2(bias)],
        out_specs=pl.BlockSpec((tm, D_MODEL), lambda i, e: (i, 0)),
        out_shape=jax.ShapeDtypeStruct((n, D_MODEL), F32),
        scratch_shapes=[pltpu.VMEM((tm, D_MODEL), BF16), pltpu.VMEM((tm, D_MODEL), F32)],
        compiler_params=_cparams(("arbitrary", "arbitrary"), 48),
        name="moe_ln2",
    )(x2, cmb, wgu, wd, gain, bias)


def _layer_weights(l, w_in, w_out, gv_gain, gv_bias, ws, bs, ret_gain, ret_bias, ln1_g, ln1_b, ln2_g, ln2_b,
                   rg_w, rg_b, re_w, re_b, e_gate, e_up, e_down):
    w = w_in[l]
    z = lambda k: jnp.zeros((D_MODEL, k), w.dtype)
    ik_end = 2560 + IDX_DIM
    iw_end = ik_end + IDX_HEADS
    w_pad = jnp.concatenate([w[:, :ik_end], z(COL_IW - COL_IK - IDX_DIM), w[:, ik_end:iw_end],
                             z(COL_CQ - COL_IW - IDX_HEADS), w[:, iw_end:]], axis=1).astype(BF16)
    wo = w_out[l].astype(BF16)
    wr = jnp.concatenate([re_w[l], rg_w[l], jnp.zeros((D_MODEL, LANES - N_EXPERTS - N_GROUPS), F32)], axis=1)
    wrh = wr.astype(BF16)
    wrl = (wr - wrh.astype(F32)).astype(BF16)
    br = jnp.concatenate([re_b[l], rg_b[l], jnp.zeros((LANES - N_EXPERTS - N_GROUPS,), F32)])[None, :]
    row = lambda a: a[l].reshape(1, -1)
    return dict(
        w_pad=w_pad, wa=wo[:A_WIDTH], wb=wo[A_WIDTH:A_WIDTH + B_WIDTH], wc=wo[A_WIDTH + B_WIDTH:],
        gv_gain=row(gv_gain), gv_bias=row(gv_bias), ws=ws[l], bs=bs[l],
        ret_gain=row(ret_gain), ret_bias=row(ret_bias),
        ln1_g=row(ln1_g), ln1_b=row(ln1_b), ln2_g=row(ln2_g), ln2_b=row(ln2_b),
        wrh=wrh, wrl=wrl, br=br,
        wgu=jnp.concatenate([e_gate[l], e_up[l]], axis=2).astype(BF16), wd=e_down[l].astype(BF16))


def _channel_mix(x2, a_out, b_out, c_out, lw):
    x1 = _outproj_ln(a_out, b_out, c_out, lw["wa"], lw["wb"], lw["wc"], x2, lw["ln1_g"], lw["ln1_b"])
    cmb = _router(x1, lw["wrh"], lw["wrl"], lw["br"])
    return _moe_ln(x1, cmb, lw["wgu"], lw["wd"], lw["ln2_g"], lw["ln2_b"])


def _heads(t2, b, l, nh):
    return t2.reshape(b, l, nh, t2.shape[-1] // nh)


def _prompt_layer(x3, lw, tt, cosf, sins):
    b, l, _ = x3.shape
    x2 = x3.reshape(b * l, D_MODEL)
    ha, kt, vt, vtb, ikt, kh, qh, iqh, ikb, iwt = _proj_prompt(x3, lw["w_pad"])
    bsb = jnp.repeat(lw["bs"].T, HEAD_DIM, axis=1)
    a_out, a_vn = _gmlp(ha, PCOL_AU, PCOL_AV, lw["gv_gain"], lw["gv_bias"], lw["ws"], bsb)
    topk = min(TOPK_MAX, l // 4)
    b_out = _dsa_prompt(iqh, iwt, ikb, qh, kh, vtb.reshape(b, B_HEADS, HEAD_DIM, l), tt, topk)
    s0e = jnp.zeros((b, C_WIDTH, C_WIDTH), F32)
    c_out, s_fin = _retention(ha.reshape(b, l, P_WIDTH), (PCOL_CQ, PCOL_CK, PCOL_CV, PCOL_CG), cosf, sins,
                              lw["ret_gain"], lw["ret_bias"], s0e, RET_CHUNK, 1)
    y = _channel_mix(x2, a_out, b_out.reshape(b * l, B_WIDTH), c_out.reshape(b * l, C_WIDTH), lw)
    last = ((l - 1) // CHUNK) * CHUNK
    to_rows = lambda t: jnp.transpose(t.reshape(b, B_HEADS, HEAD_DIM, l), (0, 3, 1, 2))
    state = (to_rows(kt), to_rows(vt), jnp.transpose(ikt, (0, 2, 1)), _state_extract(s_fin),
             a_vn.reshape(b, l, A_WIDTH)[:, last:])
    return y.reshape(b, l, D_MODEL), state


def _sample_layer(x3, lw, layer, cache_k, cache_v, cache_ik, state_l, pt_flat, npages, bfar, bnear, cosf, sins):
    bd, t, _ = x3.shape
    n = bd * t
    past = npages * PAGE_SIZE
    x2 = x3.reshape(n, D_MODEL)
    h2 = _proj(x2, lw["w_pad"])
    col = lambda c0, wdt: h2[:, c0:c0 + wdt]
    rep = CHUNK // t
    eye = jnp.eye(rep, dtype=F32)
    ws_t = lw["ws"][:, :t, :t]
    ws_bd = (eye[None, :, None, :, None] * ws_t[:, None, :, None, :]).reshape(A_GROUPS, CHUNK, CHUNK)
    bsb = jnp.repeat(jnp.tile(lw["bs"][:, :t], (1, rep)).T, HEAD_DIM, axis=1)
    a_out, a_vn = _gmlp(h2, COL_AU, COL_AV, lw["gv_gain"], lw["gv_bias"], ws_bd, bsb)
    k4 = _heads(col(COL_K, B_WIDTH), bd, t, B_HEADS)
    v4 = _heads(col(COL_V, B_WIDTH), bd, t, B_HEADS)
    ik3 = col(COL_IK, IDX_DIM).reshape(bd, t, IDX_DIM)
    feat_major = lambda a3: jnp.pad(jnp.transpose(a3, (0, 2, 1)), ((0, 0), (0, 0), (0, PAGE_SIZE - t)))
    iq4 = _heads(col(COL_IQ, IDX_HEADS * IDX_DIM), bd, t, IDX_HEADS)
    iqs = jnp.transpose(iq4, (0, 2, 1, 3)).reshape(bd, IDX_HEADS * t, IDX_DIM).astype(BF16)
    iw3 = jnp.transpose(col(COL_IW, IDX_HEADS).reshape(bd, t, IDX_HEADS), (0, 2, 1))
    iws = jnp.broadcast_to(iw3.reshape(bd, IDX_HEADS * t, 1), (bd, IDX_HEADS * t, LANES))
    scores = _sample_scores(pt_flat, iqs, iws, jnp.transpose(cache_ik, (0, 1, 3, 2)), layer, feat_major(ik3), npages)
    width = (npages + 1) * PAGE_SIZE
    topk = min(TOPK_MAX, (past + t) // 4)
    madd = _sample_mask(scores.reshape(n, width), topk, past, t).reshape(bd, t, width)
    q4 = _heads(col(COL_Q, B_WIDTH), bd, t, B_HEADS)
    eye_h = jnp.eye(B_HEADS, dtype=F32)
    qbd = (jnp.transpose(q4, (0, 2, 1, 3))[:, :, :, None, :] * eye_h[None, :, None, :, None])
    qbd = qbd.reshape(bd, B_HEADS * t, B_WIDTH).astype(BF16)
    page_view = lambda c: jnp.transpose(c, (0, 1, 3, 4, 2)).reshape(c.shape[0], c.shape[1], B_WIDTH, PAGE_SIZE)
    b_out = _sample_attend(pt_flat, qbd, madd, bfar, bnear, page_view(cache_k), page_view(cache_v), layer,
                           feat_major(k4.reshape(bd, t, B_WIDTH)), feat_major(v4.reshape(bd, t, B_WIDTH)), npages)
    c_out, s_fin = _retention(h2.reshape(bd, t, H_WIDTH), (COL_CQ, COL_CK, COL_CV, COL_CG), cosf, sins,
                              lw["ret_gain"], lw["ret_bias"], _state_embed(state_l), t, 8)
    y = _channel_mix(x2, a_out, b_out.reshape(n, B_WIDTH), c_out.reshape(n, C_WIDTH), lw)
    state = (k4, v4, ik3, _state_extract(s_fin), a_vn.reshape(bd, t, A_WIDTH))
    return y.reshape(bd, t, D_MODEL), state


def _distance_tables(rel_bias, t, past):
    r = jnp.arange(Q_BLOCK, dtype=I32)
    d_prompt = jnp.concatenate([off * Q_BLOCK + r[None, :] - r[:, None] for off in range(3)], axis=0)
    qpos = past + jnp.arange(t, dtype=I32)
    near0 = past - PAGE_SIZE
    d_near = [qpos[:, None] - (near0 + half * PAGE_SIZE + r[None, :]) for half in range(2)]
    d_far = jnp.full((t, LANES), MAX_DISTANCE * 2, I32)
    n0 = 3 * Q_BLOCK
    tables = _bias_tables(jnp.concatenate([d_prompt] + d_near + [d_far], axis=0), rel_bias, n0)
    tt = tables[:, :n0].reshape(B_HEADS, 3, Q_BLOCK, Q_BLOCK)
    bnear = jnp.concatenate([tables[:, n0:n0 + t], tables[:, n0 + t:n0 + 2 * t]], axis=2).reshape(B_HEADS * t, 2 * LANES)
    bfar = tables[:, n0 + 2 * t:n0 + 3 * t].reshape(B_HEADS * t, LANES)
    return tt, bnear, bfar


def kernel(x_prompt, x_sample, cache_k, cache_v, cache_idx_k, state_ret, page_table, w_in, w_out, gmlp_v_gain,
           gmlp_v_bias, gmlp_ws, gmlp_bs, rel_bias, ret_gn_gain, ret_gn_bias, ln1_gain, ln1_bias, ln2_gain, ln2_bias,
           router_group_w, router_group_b, router_expert_w, router_expert_b, expert_w_gate, expert_w_up,
           expert_w_down):
    depth = w_in.shape[0]
    seq = x_prompt.shape[1]
    bd, t, _ = x_sample.shape
    npages = page_table.shape[1]
    past = npages * PAGE_SIZE
    pt_flat = page_table.reshape(-1).astype(I32)
    tt, bnear, bfar = _distance_tables(rel_bias, t, past)
    cos_p, sin_p = _rope_tables(jnp.arange(seq, dtype=I32))
    cos_s, sin_s = _rope_tables(past + jnp.arange(t, dtype=I32))
    xp, xs = x_prompt, x_sample
    st_p, st_s = [], []
    for l in range(depth):
        lw = _layer_weights(l, w_in, w_out, gmlp_v_gain, gmlp_v_bias, gmlp_ws, gmlp_bs, ret_gn_gain, ret_gn_bias,
                            ln1_gain, ln1_bias, ln2_gain, ln2_bias, router_group_w, router_group_b,
                            router_expert_w, router_expert_b, expert_w_gate, expert_w_up, expert_w_down)
        xp, sp = _prompt_layer(xp, lw, tt, cos_p, sin_p)
        xs, ss = _sample_layer(xs, lw, l, cache_k, cache_v, cache_idx_k, state_ret[l], pt_flat, npages,
                               bfar, bnear, cos_s, sin_s)
        st_p.append(sp)
        st_s.append(ss)
    stk = lambda sts, i: jnp.stack([s[i] for s in sts], axis=0)
    return (xp, xs,
            stk(st_p, 0), stk(st_p, 1), stk(st_p, 2), stk(st_p, 3), stk(st_p, 4),
            stk(st_s, 0), stk(st_s, 1), stk(st_s, 2), stk(st_s, 3), stk(st_s, 4))
```

```python
import functools
import math

import jax
import jax.numpy as jnp
from jax import lax
from jax.experimental import pallas as pl
from jax.experimental.pallas import tpu as pltpu

D_MODEL = 1024
HEAD_DIM = 64
A_GROUPS = 4
A_WIDTH = 256
CHUNK = 128
B_HEADS = 8
B_WIDTH = 512
IDX_HEADS = 8
IDX_DIM = 64
TOPK_MAX = 256
Q_BLOCK = 128
KEY_TILE = 256
PHASE_UNROLL = 2
NUM_BUCKETS = 32
MAX_DISTANCE = 128
C_HEADS = 4
C_WIDTH = 256
RET_CHUNK = 128
ROPE_BASE = 10000.0
PAGE_SIZE = 128
N_GROUPS = 4
EXPERTS_PER_GROUP = 8
N_EXPERTS = 32
EXPERT_FF = 256
DEPTH = 2
ALPHA = (2 * DEPTH) ** 0.25
LN_EPS = 1e-5

F32 = jnp.float32
BF16 = jnp.bfloat16
I32 = jnp.int32
I16 = jnp.int16
LANES = 128
MASK_NEG = -1e30
INT_MIN = -2 ** 31

COL_AU, COL_AV, COL_Q, COL_K, COL_V, COL_IQ, COL_IK, COL_IW = 0, 256, 512, 1024, 1536, 2048, 2560, 2688
COL_CQ, COL_CK, COL_CV, COL_CG = 2816, 3072, 3328, 3584
H_WIDTH = 3840


def _cparams(sem, vmem_mb=None):
    kw = dict(dimension_semantics=sem)
    if vmem_mb is not None:
        kw["vmem_limit_bytes"] = vmem_mb << 20
    return pltpu.CompilerParams(**kw)


def _nt_dot(a, b):
    return lax.dot_general(a, b, (((1,), (1,)), ((), ())), preferred_element_type=F32)


def _dot(a, b):
    return jnp.dot(a, b, preferred_element_type=F32)


def _layer_norm_rows(x, gain, bias):
    mu = jnp.mean(x, axis=-1, keepdims=True)
    xc = x - mu
    var = jnp.mean(xc * xc, axis=-1, keepdims=True)
    return xc * lax.rsqrt(var + LN_EPS) * gain + bias


def _silu(x):
    return x * (1.0 / (1.0 + jnp.exp(-x)))


def _proj_kernel(x_ref, w_ref, o_ref):
    o_ref[...] = _dot(x_ref[...].astype(BF16), w_ref[...])


def _proj(x2, w_pad):
    n = x2.shape[0]
    tm = 1024 if n > 1024 else 512
    tn = 768
    return pl.pallas_call(
        _proj_kernel,
        grid=(n // tm, H_WIDTH // tn),
        in_specs=[pl.BlockSpec((tm, D_MODEL), lambda i, j: (i, 0)),
                  pl.BlockSpec((D_MODEL, tn), lambda i, j: (0, j))],
        out_specs=pl.BlockSpec((tm, tn), lambda i, j: (i, j)),
        out_shape=jax.ShapeDtypeStruct((n, H_WIDTH), F32),
        compiler_params=_cparams(("arbitrary", "arbitrary"), 40),
        name="in_proj",
    )(x2, w_pad)


PCOL_AU, PCOL_AV, PCOL_CQ, PCOL_CK, PCOL_CV, PCOL_CG = 0, 256, 512, 768, 1024, 1280
P_WIDTH = 1536


def _proj_prompt_kernel(x_ref, w_ref, ha_ref, kt_ref, vt_ref, vtb_ref, ikt_ref, kh_ref, qt_ref, iqt_ref, ikb_ref,
                        iwt_ref):
    x = x_ref[0].astype(BF16)
    piece = lambda c0, width: _dot(x, w_ref[:, c0:c0 + width])
    ha_ref[:, :COL_Q] = piece(COL_AU, COL_Q)
    ha_ref[:, COL_Q:] = piece(COL_CQ, H_WIDTH - COL_CQ)
    qt_ref[0] = jnp.transpose(piece(COL_Q, B_WIDTH) * (HEAD_DIM ** -0.5 * LOG2E)).astype(BF16)
    iqt_ref[0] = jnp.transpose(piece(COL_IQ, IDX_HEADS * IDX_DIM)).astype(BF16)
    k = piece(COL_K, B_WIDTH)
    for h in range(B_HEADS):
        kh_ref[0, h] = k[:, h * HEAD_DIM:(h + 1) * HEAD_DIM].astype(BF16)
    kt_ref[0] = jnp.transpose(k)
    vt = jnp.transpose(piece(COL_V, B_WIDTH))
    vt_ref[0] = vt
    vtb_ref[0] = vt.astype(BF16)
    ik = piece(COL_IK, LANES)
    ikb_ref[0] = ik[:, :IDX_DIM].astype(BF16)
    ikt_ref[0] = jnp.transpose(ik)[:IDX_DIM]
    iwt_ref[0] = jnp.transpose(piece(COL_IW, LANES))[:IDX_HEADS]


def _proj_prompt(x3, w_pad):
    b, l, _ = x3.shape
    tm = 512
    nt = l // tm
    f32s = lambda *shape: jax.ShapeDtypeStruct(shape, F32)
    bf16s = lambda *shape: jax.ShapeDtypeStruct(shape, BF16)
    feat = lambda rows: pl.BlockSpec((1, rows, tm), lambda i, j: (i, 0, j))
    hmaj = pl.BlockSpec((1, B_HEADS, tm, HEAD_DIM), lambda i, j: (i, 0, j, 0))
    return pl.pallas_call(
        _proj_prompt_kernel,
        grid=(b, nt),
        in_specs=[pl.BlockSpec((1, tm, D_MODEL), lambda i, j: (i, j, 0)),
                  pl.BlockSpec(w_pad.shape, lambda i, j: (0, 0), pipeline_mode=pl.Buffered(1))],
        out_specs=[pl.BlockSpec((tm, P_WIDTH), lambda i, j: (i * nt + j, 0)),
                   feat(B_WIDTH), feat(B_WIDTH), feat(B_WIDTH), feat(IDX_DIM),
                   hmaj, feat(B_WIDTH), feat(IDX_HEADS * IDX_DIM),
                   pl.BlockSpec((1, tm, IDX_DIM), lambda i, j: (i, j, 0)),
                   feat(IDX_HEADS)],
        out_shape=[f32s(b * l, P_WIDTH), f32s(b, B_WIDTH, l), f32s(b, B_WIDTH, l), bf16s(b, B_WIDTH, l),
                   f32s(b, IDX_DIM, l), bf16s(b, B_HEADS, l, HEAD_DIM), bf16s(b, B_WIDTH, l),
                   bf16s(b, IDX_HEADS * IDX_DIM, l), bf16s(b, l, IDX_DIM), f32s(b, IDX_HEADS, l)],
        compiler_params=_cparams(("arbitrary", "arbitrary"), 48),
        name="in_proj_prompt",
    )(x3, w_pad)


LOG2E = math.log2(math.e)


def _bias_kernel(rb_ref, d_ref, o_ref, *, log2_rows):
    n = jnp.maximum(d_ref[...], 0)
    max_exact = NUM_BUCKETS // 2
    nf = jnp.maximum(n, 1).astype(F32)
    large = max_exact + (jnp.log(nf / max_exact) / math.log(MAX_DISTANCE / max_exact)
                         * (NUM_BUCKETS - max_exact)).astype(I32)
    large = jnp.minimum(large, NUM_BUCKETS - 1)
    bucket = jnp.where(n < max_exact, n, large)
    for h in range(B_HEADS):
        acc = jnp.zeros(bucket.shape, F32)
        for bk in range(NUM_BUCKETS):
            acc = jnp.where(bucket == bk, rb_ref[bk * B_HEADS + h], acc)
        o_ref[h, :log2_rows] = acc[:log2_rows] * LOG2E
        o_ref[h, log2_rows:] = acc[log2_rows:]


def _bias_tables(dist, rel_bias, log2_rows):
    r = dist.shape[0]
    return pl.pallas_call(
        functools.partial(_bias_kernel, log2_rows=log2_rows),
        in_specs=[pl.BlockSpec(memory_space=pltpu.SMEM),
                  pl.BlockSpec((r, LANES), lambda: (0, 0))],
        out_specs=pl.BlockSpec((B_HEADS, r, LANES), lambda: (0, 0, 0)),
        out_shape=jax.ShapeDtypeStruct((B_HEADS, r, LANES), F32),
        name="bias_tables",
    )(rel_bias.reshape(-1), dist)


def _gmlp_kernel(u_ref, v_ref, g_ref, b_ref, ws_ref, bsb_ref, o_ref, vn_ref, *, nchunk):
    r = lax.broadcasted_iota(I32, (CHUNK, CHUNK), 0)
    c = lax.broadcasted_iota(I32, (CHUNK, CHUNK), 1)
    grp = lax.broadcasted_iota(I32, (CHUNK, A_WIDTH), 1) // HEAD_DIM
    wts = [jnp.where(r >= c, ws_ref[g], 0.0).astype(BF16) for g in range(A_GROUPS)]
    gain, bias, bsb = g_ref[...], b_ref[...], bsb_ref[...]
    for ci in range(nchunk):
        sl = pl.ds(ci * CHUNK, CHUNK)
        vn = _layer_norm_rows(v_ref[sl, :], gain, bias)
        vn_ref[sl, :] = vn
        vb = vn.astype(BF16)
        mixed = bsb
        for g in range(A_GROUPS):
            mixed = mixed + jnp.where(grp == g, _dot(wts[g], vb), 0.0)
        o_ref[sl, :] = (u_ref[sl, :] * mixed).astype(BF16)


def _gmlp(h2, col_u, col_v, gain, bias, ws, bsb):
    n = h2.shape[0]
    tm = min(n, 1024)
    blk = lambda col: pl.BlockSpec((tm, A_WIDTH), lambda i: (i, col // A_WIDTH))
    full2 = lambda a: pl.BlockSpec(a.shape, lambda i: (0, 0))
    return pl.pallas_call(
        functools.partial(_gmlp_kernel, nchunk=tm // CHUNK),
        grid=(n // tm,),
        in_specs=[blk(col_u), blk(col_v), full2(gain), full2(bias),
                  pl.BlockSpec(ws.shape, lambda i: (0, 0, 0)), full2(bsb)],
        out_specs=[pl.BlockSpec((tm, A_WIDTH), lambda i: (i, 0)),
                   pl.BlockSpec((tm, A_WIDTH), lambda i: (i, 0))],
        out_shape=[jax.ShapeDtypeStruct((n, A_WIDTH), BF16),
                   jax.ShapeDtypeStruct((n, A_WIDTH), F32)],
        compiler_params=_cparams(("arbitrary",)),
        name="gmlp",
    )(h2, h2, gain, bias, ws, bsb)


def _ret_tables(c):
    log_g = jnp.log(1.0 - 2.0 ** (-5.0 - jnp.arange(C_HEADS, dtype=F32)))
    i = jnp.arange(c, dtype=F32)
    diff = i[:, None] - i[None, :]
    dmat = jnp.where(diff >= 0, jnp.exp(log_g[:, None, None] * jnp.maximum(diff, 0.0)), 0.0)
    q_dec = jnp.exp(log_g[:, None] * (i + 1.0))
    k_dec = jnp.exp(log_g[:, None] * (c - 1.0 - i))
    s_dec = jnp.exp(log_g * c)
    qd = jnp.repeat(q_dec.T, HEAD_DIM, axis=1)
    kd = jnp.repeat(k_dec.T, HEAD_DIM, axis=1)
    hid = jnp.arange(C_WIDTH) // HEAD_DIM
    same = hid[:, None] == hid[None, :]
    sd = jnp.where(same, s_dec[hid][:, None], 0.0)
    return dmat, qd, kd, sd, same.astype(F32)


def _rope_tables(pos):
    half = HEAD_DIM // 2
    inv = ROPE_BASE ** (-jnp.arange(half, dtype=F32) / half)
    ang = pos.astype(F32)[:, None] * inv[None, :]
    cos, sin = jnp.cos(ang), jnp.sin(ang)
    cosf = jnp.tile(jnp.concatenate([cos, cos], axis=1), (1, C_HEADS))
    sins = jnp.tile(jnp.concatenate([-sin, sin], axis=1), (1, C_HEADS))
    return cosf, sins


def _ret_kernel(q_ref, k_ref, v_ref, g_ref, cos_ref, sin_ref, qd_ref, kd_ref, dm_ref, sd_ref, bd_ref,
                gg_ref, gb_ref, s0_ref, o_ref, sf_ref, s_scr, *, bt, c):
    ci = pl.program_id(1)

    @pl.when(ci == 0)
    def _():
        s_scr[...] = s0_ref[...]

    lane = lax.broadcasted_iota(I32, (c, C_WIDTH), 1)
    hid = lane // HEAD_DIM
    first = (lane % HEAD_DIM) < (HEAD_DIM // 2)
    cosf, sins = cos_ref[...], sin_ref[...]
    half = HEAD_DIM // 2

    def rot(x):
        partner = jnp.where(first, pltpu.roll(x, C_WIDTH - half, 1), pltpu.roll(x, half, 1))
        return x * cosf + partner * sins

    def seg_mean(x):
        out = jnp.zeros_like(x)
        for h in range(C_HEADS):
            hm = hid == h
            s = jnp.sum(jnp.where(hm, x, 0.0), axis=1, keepdims=True) * (1.0 / HEAD_DIM)
            out = jnp.where(hm, s, out)
        return out

    for bb in range(bt):
        q = rot(q_ref[bb])
        k = rot(k_ref[bb]) * (HEAD_DIM ** -0.5)
        v = v_ref[bb]
        kb, vb = k.astype(BF16), v.astype(BF16)
        s_old = s_scr[bb]
        o = _dot(q.astype(BF16), s_old.astype(BF16)) * qd_ref[...]
        for h in range(C_HEADS):
            hm = hid == h
            att = _nt_dot(jnp.where(hm, q, 0.0).astype(BF16), kb) * dm_ref[h]
            o = o + jnp.where(hm, _dot(att.astype(BF16), vb), 0.0)
        kdt = jnp.transpose(k * kd_ref[...]).astype(BF16)
        s_scr[bb] = s_old * sd_ref[...] + bd_ref[...] * _dot(kdt, vb)
        mu = seg_mean(o)
        oc = o - mu
        var = seg_mean(oc * oc)
        normed = oc * lax.rsqrt(var + LN_EPS) * gg_ref[...] + gb_ref[...]
        o_ref[bb] = (_silu(g_ref[bb]) * normed).astype(BF16)

    @pl.when(ci == pl.num_programs(1) - 1)
    def _():
        sf_ref[...] = s_scr[...]


def _retention(h3, cols, cosf, sins, gn_gain, gn_bias, s0e, c, bt):
    b, l, _ = h3.shape
    dmat, qd, kd, sd, bd = _ret_tables(c)
    blk = lambda col: pl.BlockSpec((bt, c, C_WIDTH), lambda i, j: (i, j, col // C_WIDTH))
    const2 = lambda a: pl.BlockSpec(a.shape, lambda i, j: (0, 0))
    pos_blk = pl.BlockSpec((c, C_WIDTH), lambda i, j: (j, 0))
    st_blk = pl.BlockSpec((bt, C_WIDTH, C_WIDTH), lambda i, j: (i, 0, 0))
    return pl.pallas_call(
        functools.partial(_ret_kernel, bt=bt, c=c),
        grid=(b // bt, l // c),
        in_specs=[blk(cols[0]), blk(cols[1]), blk(cols[2]), blk(cols[3]), pos_blk, pos_blk,
                  const2(qd), const2(kd), pl.BlockSpec(dmat.shape, lambda i, j: (0, 0, 0)),
                  const2(sd), const2(bd), const2(gn_gain), const2(gn_bias), st_blk],
        out_specs=[pl.BlockSpec((bt, c, C_WIDTH), lambda i, j: (i, j, 0)), st_blk],
        out_shape=[jax.ShapeDtypeStruct((b, l, C_WIDTH), BF16),
                   jax.ShapeDtypeStruct((b, C_WIDTH, C_WIDTH), F32)],
        scratch_shapes=[pltpu.VMEM((bt, C_WIDTH, C_WIDTH), F32)],
        compiler_params=_cparams(("arbitrary", "arbitrary")),
        name="retention",
    )(h3, h3, h3, h3, cosf, sins, qd, kd, dmat, sd, bd, gn_gain, gn_bias, s0e)


def _state_embed(s):
    b = s.shape[0]
    eye = jnp.eye(C_HEADS, dtype=s.dtype)
    return (s[:, :, :, None, :] * eye[None, :, None, :, None]).reshape(b, C_WIDTH, C_WIDTH)


def _state_extract(se):
    b = se.shape[0]
    s5 = se.reshape(b, C_HEADS, HEAD_DIM, C_HEADS, HEAD_DIM)
    return jnp.stack([s5[:, h, :, h, :] for h in range(C_HEADS)], axis=1)


def _sortable(score):
    bits = pltpu.bitcast(score, I32)
    return bits ^ ((bits >> 31) & 0x7FFFFFFF)


def _dsa_kernel(iq_ref, iw_ref, ik_ref, q_ref, k_ref, vt_ref, tt_ref, o_ref,
                sc_scr, hi_scr, lo_scr, s_scr, acc_scr, j_scr, *, topk):
    qb = pl.program_id(1)
    ntile = (qb + KEY_TILE // Q_BLOCK) // (KEY_TILE // Q_BLOCK)
    row = lax.broadcasted_iota(I32, (KEY_TILE, Q_BLOCK), 0)
    col = lax.broadcasted_iota(I32, (KEY_TILE, Q_BLOCK), 1)
    qpos = qb * Q_BLOCK + col
    head_rows = lambda ref, h: ref[0, h * HEAD_DIM:(h + 1) * HEAD_DIM, :]
    iq_all = jnp.concatenate([head_rows(iq_ref, h) for h in range(IDX_HEADS)], axis=1)
    iw = iw_ref[0]
    int_min = jnp.int32(INT_MIN)
    score_scale = IDX_HEADS ** -0.5

    def tile_rows(j):
        return pl.ds(pl.multiple_of(j * KEY_TILE, KEY_TILE), KEY_TILE)

    def tree_sum(parts):
        while len(parts) > 1:
            parts = [a + b for a, b in zip(parts[::2], parts[1::2])]
        return parts[0]

    def score_tile(j, carry):
        s = _dot(ik_ref[0, tile_rows(j), :], iq_all)
        terms = [jnp.maximum(s[:, h * Q_BLOCK:(h + 1) * Q_BLOCK], 0.0) * iw[h:h + 1, :] for h in range(IDX_HEADS)]
        acc = terms[0]
        for term in terms[1:]:
            acc = acc + term
        score = (acc * (IDX_DIM ** -0.5)) * score_scale
        kint = jnp.where(j * KEY_TILE + row <= qpos, _sortable(score), int_min)
        sc_scr[tile_rows(j), :] = kint
        hi_scr[tile_rows(j), :] = (kint >> 16).astype(I16)
        return carry

    nstep = (ntile + PHASE_UNROLL - 1) // PHASE_UNROLL

    def unrolled(tile_fn):
        def step(js, carry):
            for u in range(PHASE_UNROLL):
                carry = tile_fn(js * PHASE_UNROLL + u, carry)
            return carry
        return step

    lax.fori_loop(0, nstep, unrolled(score_tile), 0)

    def count(pred):
        def body(j, acc):
            m = jnp.where(pred(sc_scr[tile_rows(j), :], j), 1.0, 0.0)
            return acc + tree_sum([m[i * 8:(i + 1) * 8] for i in range(KEY_TILE // 8)])
        acc = lax.fori_loop(0, ntile, body, jnp.zeros((8, Q_BLOCK), F32))
        return jnp.sum(acc, axis=0, keepdims=True)

    half_min = -(1 << 15)
    ones16, zeros16 = jnp.ones((16, Q_BLOCK), I16), jnp.zeros((16, Q_BLOCK), I16)

    def count16(ref, cand, strict=False):
        cand16 = jnp.broadcast_to(cand, (16, Q_BLOCK)).astype(I16)

        def tile(j, acc):
            x = ref[tile_rows(j), :]
            hits = []
            for i in range(KEY_TILE // 16):
                xi = x[i * 16:(i + 1) * 16]
                hits.append(jnp.where((xi > cand16) if strict else (xi >= cand16), ones16, zeros16))
            return acc + tree_sum(hits)

        acc = lax.fori_loop(0, nstep, unrolled(tile), zeros16)
        return jnp.sum(acc.astype(F32), axis=0, keepdims=True)

    def high_bit(i, ans):
        cand = ans | lax.shift_left(jnp.int32(1), 15 - i)
        return jnp.where(count16(hi_scr, cand + half_min) >= topk, cand, ans)

    t_hi = lax.fori_loop(0, 16, high_bit, jnp.zeros((1, Q_BLOCK), I32)) + half_min
    n_above = count16(hi_scr, t_hi, strict=True)

    def low_halves(j, carry):
        t = sc_scr[tile_rows(j), :]
        lo_scr[tile_rows(j), :] = jnp.where((t >> 16) == t_hi, (t & 0xFFFF) + half_min, half_min).astype(I16)
        return carry

    lax.fori_loop(0, nstep, unrolled(low_halves), 0)

    def low_bit(i, ans):
        cand = ans | lax.shift_left(jnp.int32(1), 15 - i)
        return jnp.where(n_above + count16(lo_scr, cand + half_min) >= topk, cand, ans)

    t_lo = lax.fori_loop(0, 16, low_bit, jnp.zeros((1, Q_BLOCK), I32))
    thr = lax.shift_left(t_hi, 16) | t_lo
    need = topk - count(lambda t, j: t > thr)
    n_ge = count(lambda t, j: t >= thr)

    j_scr[...] = jnp.full(j_scr.shape, 1 << 20, I32)
    has_tie = jnp.max(jnp.where((n_ge > topk) & (thr != int_min), 1.0, 0.0)) > 0.0

    @pl.when(has_tie)
    def _():
        def jbit(i, jc):
            cand = jc | lax.shift_left(jnp.int32(1), 12 - i)
            cnt = count(lambda t, j: (t == thr) & (j * KEY_TILE + row < cand))
            return jnp.where(cnt <= need, cand, jc)
        jc = lax.fori_loop(0, 13, jbit, jnp.zeros((1, Q_BLOCK), I32))
        j_scr[...] = jnp.broadcast_to(jc, j_scr.shape)

    j_cut = jnp.where(thr == int_min, 0, j_scr[0:1, :])

    q_scaled = [head_rows(q_ref, h) for h in range(B_HEADS)]
    sub_tiles = KEY_TILE // Q_BLOCK
    groups = KEY_TILE // 8

    def tree_max(parts):
        while len(parts) > 1:
            parts = [jnp.maximum(a, b) for a, b in zip(parts[::2], parts[1::2])]
        return parts[0]

    def logits_tile(j, m_part):
        t = sc_scr[tile_rows(j), :]
        kidx = j * KEY_TILE + row
        madd = jnp.where(t > thr, 0.0, jnp.where(t == thr, jnp.where(kidx < j_cut, 0.0, MASK_NEG), MASK_NEG))
        offs = [jnp.clip(qb - (j * sub_tiles + i), 0, 2) for i in range(sub_tiles)]
        new = []
        for h in range(B_HEADS):
            bias = jnp.concatenate([tt_ref[h, off] for off in offs], axis=0)
            s = _dot(k_ref[0, h, tile_rows(j), :], q_scaled[h]) + bias + madd
            s_scr[h, tile_rows(j), :] = s
            new.append(jnp.maximum(m_part[h], tree_max([s[i * 8:(i + 1) * 8] for i in range(groups)])))
        return jnp.stack(new)

    m_part = lax.fori_loop(0, nstep, unrolled(logits_tile), jnp.full((B_HEADS, 8, Q_BLOCK), MASK_NEG, F32))
    m_rows = [jnp.max(m_part[h], axis=0, keepdims=True) for h in range(B_HEADS)]

    acc_scr[...] = jnp.zeros(acc_scr.shape, F32)

    def values_tile(j, l_part):
        new = []
        for h in range(B_HEADS):
            p = jnp.exp2(s_scr[h, tile_rows(j), :] - m_rows[h])
            new.append(l_part[h] + tree_sum([p[i * 8:(i + 1) * 8] for i in range(groups)]))
            acc_scr[h] += _dot(vt_ref[0, h, :, tile_rows(j)], p.astype(BF16))
        return jnp.stack(new)

    l_part = lax.fori_loop(0, nstep, unrolled(values_tile), jnp.zeros((B_HEADS, 8, Q_BLOCK), F32))
    out_t = jnp.concatenate([acc_scr[h] / jnp.sum(l_part[h], axis=0, keepdims=True) for h in range(B_HEADS)], axis=0)
    o_ref[0] = jnp.transpose(out_t).astype(BF16)


def _dsa_prompt(iqt, iwt, ik3, qt, kh, vt, tt, topk):
    b, _, l, _ = kh.shape
    assert l % (KEY_TILE * PHASE_UNROLL) == 0
    qblk = pl.BlockSpec((1, B_WIDTH, Q_BLOCK), lambda i, j: (i, 0, j))
    once = pl.Buffered(1)
    return pl.pallas_call(
        functools.partial(_dsa_kernel, topk=topk),
        grid=(b, l // Q_BLOCK),
        in_specs=[qblk,
                  pl.BlockSpec((1, IDX_HEADS, Q_BLOCK), lambda i, j: (i, 0, j)),
                  pl.BlockSpec((1, l, IDX_DIM), lambda i, j: (i, 0, 0), pipeline_mode=once),
                  qblk,
                  pl.BlockSpec((1, B_HEADS, l, HEAD_DIM), lambda i, j: (i, 0, 0, 0), pipeline_mode=once),
                  pl.BlockSpec((1, B_HEADS, HEAD_DIM, l), lambda i, j: (i, 0, 0, 0), pipeline_mode=once),
                  pl.BlockSpec(tt.shape, lambda i, j: (0, 0, 0, 0), pipeline_mode=once)],
        out_specs=pl.BlockSpec((1, Q_BLOCK, B_WIDTH), lambda i, j: (i, j, 0)),
        out_shape=jax.ShapeDtypeStruct((b, l, B_WIDTH), BF16),
        scratch_shapes=[pltpu.VMEM((l, Q_BLOCK), I32),
                        pltpu.VMEM((l, Q_BLOCK), I16),
                        pltpu.VMEM((l, Q_BLOCK), I16),
                        pltpu.VMEM((B_HEADS, l, Q_BLOCK), F32),
                        pltpu.VMEM((B_HEADS, HEAD_DIM, Q_BLOCK), F32),
                        pltpu.VMEM((8, Q_BLOCK), I32)],
        compiler_params=_cparams(("arbitrary", "arbitrary"), 56),
        name="dsa_prompt",
    )(iqt, iwt, ik3, qt, kh, vt, tt)


def _sidx_kernel(pt_ref, iq_ref, iw_ref, *rest, npages):
    pages, ikn_ref, o_ref = rest[:npages], rest[npages], rest[npages + 1]
    iq, iw = iq_ref[0], iw_ref[0]
    nq = iq.shape[0] // IDX_HEADS
    for j in range(npages + 1):
        keys_t = (pages[j][0, 0] if j < npages else ikn_ref[0]).astype(BF16)
        r = jnp.maximum(_dot(iq, keys_t), 0.0) * iw
        acc = r[0:nq]
        for h in range(1, IDX_HEADS):
            acc = acc + r[h * nq:(h + 1) * nq]
        o_ref[0, :, j * PAGE_SIZE:(j + 1) * PAGE_SIZE] = (acc * (IDX_DIM ** -0.5)) * (IDX_HEADS ** -0.5)


def _sample_scores(pt_flat, iqs, iws, cache_ik, layer, ikn, npages):
    bd, hq, _ = iqs.shape
    nq = hq // IDX_HEADS
    width = (npages + 1) * PAGE_SIZE
    page_spec = lambda p: pl.BlockSpec((1, 1, IDX_DIM, PAGE_SIZE),
                                       lambda i, pt, p=p: (layer, pt[i * npages + p], 0, 0))
    per_b = lambda a: pl.BlockSpec((1,) + a.shape[1:], lambda i, pt: (i, 0, 0))
    return pl.pallas_call(
        functools.partial(_sidx_kernel, npages=npages),
        grid_spec=pltpu.PrefetchScalarGridSpec(
            num_scalar_prefetch=1, grid=(bd,),
            in_specs=[per_b(iqs), per_b(iws)] + [page_spec(p) for p in range(npages)] + [per_b(ikn)],
            out_specs=pl.BlockSpec((1, nq, width), lambda i, pt: (i, 0, 0))),
        out_shape=jax.ShapeDtypeStruct((bd, nq, width), F32),
        compiler_params=_cparams(("arbitrary",)),
        name="sample_scores",
    )(pt_flat, iqs, iws, *([cache_ik] * npages), ikn)


def _sthr_kernel(s_ref, o_ref, k_scr, *, topk, past, nq):
    tr, width = s_ref.shape
    colw = lax.broadcasted_iota(I32, (tr, width), 1)
    rowq = lax.broadcasted_iota(I32, (tr, width), 0) % nq
    vis = colw <= past + rowq
    int_min = jnp.int32(INT_MIN)
    k_scr[...] = jnp.where(vis, _sortable(s_ref[...]), int_min)

    def count(pred):
        return jnp.sum(jnp.where(pred(k_scr[...]), 1.0, 0.0), axis=1, keepdims=True)

    def bit_body(i, ans_u):
        cand_u = ans_u | lax.shift_left(jnp.int32(1), 31 - i)
        cand_s = cand_u ^ int_min
        return jnp.where(count(lambda t: t >= cand_s) >= topk, cand_u, ans_u)

    thr = lax.fori_loop(0, 32, bit_body, jnp.zeros((tr, 1), I32)) ^ int_min
    need = topk - count(lambda t: t > thr)

    def jbit(i, jc):
        cand = jc | lax.shift_left(jnp.int32(1), 12 - i)
        cnt = count(lambda t: (t == thr) & (colw < cand))
        return jnp.where(cnt <= need, cand, jc)

    j_cut = lax.fori_loop(0, 13, jbit, jnp.zeros((tr, 1), I32))
    t = k_scr[...]
    sel = ((t > thr) | ((t == thr) & (colw < j_cut))) & vis
    o_ref[...] = jnp.where(sel, 0.0, MASK_NEG)


def _sample_mask(scores2, topk, past, nq):
    r, width = scores2.shape
    tr = min(r, 128)
    return pl.pallas_call(
        functools.partial(_sthr_kernel, topk=topk, past=past, nq=nq),
        grid=(r // tr,),
        in_specs=[pl.BlockSpec((tr, width), lambda i: (i, 0))],
        out_specs=pl.BlockSpec((tr, width), lambda i: (i, 0)),
        out_shape=jax.ShapeDtypeStruct((r, width), F32),
        scratch_shapes=[pltpu.VMEM((tr, width), I32)],
        compiler_params=_cparams(("arbitrary",)),
        name="sample_mask",
    )(scores2)


def _sattn_kernel(pt_ref, q_ref, m_ref, bfar_ref, bnear_ref, *rest, npages, nq):
    kpages, vpages = rest[:npages], rest[npages:2 * npages]
    kn_ref, vn_ref, o_ref, s_scr = rest[2 * npages:2 * npages + 4]
    q = q_ref[0]
    mx = jnp.full((B_HEADS * nq, PAGE_SIZE), MASK_NEG, F32)
    for j in range(npages + 1):
        kp = (kpages[j][0, 0] if j < npages else kn_ref[0]).astype(BF16)
        cols = slice(j * PAGE_SIZE, (j + 1) * PAGE_SIZE)
        bias = bfar_ref[...] if j < npages - 1 else bnear_ref[:, (j - npages + 1) * PAGE_SIZE:(j - npages + 2) * PAGE_SIZE]
        madd = jnp.concatenate([m_ref[0, :, cols]] * B_HEADS, axis=0)
        s = _dot(q, kp) * (HEAD_DIM ** -0.5) + bias + madd
        s_scr[:, cols] = s
        mx = jnp.maximum(mx, s)
    mrow = jnp.max(mx, axis=1, keepdims=True)
    lsum = jnp.zeros((B_HEADS * nq, PAGE_SIZE), F32)
    acc = jnp.zeros((B_HEADS * nq, B_WIDTH), F32)
    for j in range(npages + 1):
        vp = (vpages[j][0, 0] if j < npages else vn_ref[0]).astype(BF16)
        p = jnp.exp(s_scr[:, j * PAGE_SIZE:(j + 1) * PAGE_SIZE] - mrow)
        lsum = lsum + p
        acc = acc + _nt_dot(p.astype(BF16), vp)
    acc = acc / jnp.sum(lsum, axis=1, keepdims=True)
    head = lax.broadcasted_iota(I32, (nq, B_WIDTH), 1) // HEAD_DIM
    out = jnp.zeros((nq, B_WIDTH), F32)
    for h in range(B_HEADS):
        out = jnp.where(head == h, acc[h * nq:(h + 1) * nq], out)
    o_ref[0] = out.astype(BF16)


def _sample_attend(pt_flat, qbd, madd3, bfar, bnear, cache_k, cache_v, layer, kn, vn, npages):
    bd, hq, _ = qbd.shape
    nq = hq // B_HEADS
    width = (npages + 1) * PAGE_SIZE
    page_spec = lambda p: pl.BlockSpec((1, 1, B_WIDTH, PAGE_SIZE),
                                       lambda i, pt, p=p: (layer, pt[i * npages + p], 0, 0))
    per_b = lambda a: pl.BlockSpec((1,) + a.shape[1:], lambda i, pt: (i, 0, 0))
    const2 = lambda a: pl.BlockSpec(a.shape, lambda i, pt: (0, 0))
    pages = [page_spec(p) for p in range(npages)]
    return pl.pallas_call(
        functools.partial(_sattn_kernel, npages=npages, nq=nq),
        grid_spec=pltpu.PrefetchScalarGridSpec(
            num_scalar_prefetch=1, grid=(bd,),
            in_specs=[per_b(qbd), per_b(madd3), const2(bfar), const2(bnear)] + pages + pages + [per_b(kn), per_b(vn)],
            out_specs=pl.BlockSpec((1, nq, B_WIDTH), lambda i, pt: (i, 0, 0)),
            scratch_shapes=[pltpu.VMEM((hq, width), F32)]),
        out_shape=jax.ShapeDtypeStruct((bd, nq, B_WIDTH), BF16),
        compiler_params=_cparams(("arbitrary",), 56),
        name="sample_attend",
    )(pt_flat, qbd, madd3, bfar, bnear, *([cache_k] * npages), *([cache_v] * npages), kn, vn)


def _outproj_kernel(a_ref, b_ref, c_ref, wa_ref, wb_ref, wc_ref, x_ref, g_ref, bb_ref, o_ref):
    mix = _dot(a_ref[...], wa_ref[...]) + _dot(b_ref[...], wb_ref[...]) + _dot(c_ref[...], wc_ref[...])
    o_ref[...] = _layer_norm_rows(ALPHA * x_ref[...] + mix, g_ref[...], bb_ref[...])


def _outproj_ln(a, b, c, wa, wb, wc, x2, gain, bias):
    n = x2.shape[0]
    tm = 512
    rows = lambda a_: pl.BlockSpec((tm, a_.shape[1]), lambda i: (i, 0))
    full2 = lambda a_: pl.BlockSpec(a_.shape, lambda i: (0, 0))
    return pl.pallas_call(
        _outproj_kernel,
        grid=(n // tm,),
        in_specs=[rows(a), rows(b), rows(c), full2(wa), full2(wb), full2(wc), rows(x2), full2(gain), full2(bias)],
        out_specs=pl.BlockSpec((tm, D_MODEL), lambda i: (i, 0)),
        out_shape=jax.ShapeDtypeStruct((n, D_MODEL), F32),
        compiler_params=_cparams(("arbitrary",), 40),
        name="outproj_ln1",
    )(a, b, c, wa, wb, wc, x2, gain, bias)


def _router_kernel(x_ref, wh_ref, wl_ref, b_ref, o_ref):
    x = x_ref[...]
    xh = x.astype(BF16)
    xl = (x - xh.astype(F32)).astype(BF16)
    logits = _dot(xh, wh_ref[...]) + _dot(xh, wl_ref[...]) + _dot(xl, wh_ref[...]) + b_ref[...]
    lane = lax.broadcasted_iota(I32, logits.shape, 1).astype(F32)
    neg = -jnp.inf
    first_lane = lambda m: jnp.min(jnp.where(m, lane, float(LANES)), axis=1, keepdims=True)
    is_g = (lane >= N_EXPERTS) & (lane < N_EXPERTS + N_GROUPS)
    gmax = jnp.max(jnp.where(is_g, logits, neg), axis=1, keepdims=True)
    g_sel = first_lane(is_g & (logits == gmax)) - N_EXPERTS
    p_group = 1.0 / jnp.sum(jnp.where(is_g, jnp.exp(logits - gmax), 0.0), axis=1, keepdims=True)
    in_g = jnp.floor(lane * (1.0 / EXPERTS_PER_GROUP)) == g_sel
    ev = jnp.where(in_g, logits, neg)
    v1 = jnp.max(ev, axis=1, keepdims=True)
    i1 = first_lane(in_g & (logits == v1))
    ev2 = jnp.where(lane == i1, neg, ev)
    v2 = jnp.max(ev2, axis=1, keepdims=True)
    i2 = first_lane(in_g & (lane != i1) & (logits == v2))
    e2 = jnp.exp(v2 - v1)
    den = 1.0 + e2
    g1 = (1.0 / den) * p_group
    g2 = (e2 / den) * p_group
    o_ref[...] = jnp.where(lane == i1, g1, jnp.where(lane == i2, g2, 0.0))


def _router(x2, wh, wl, bias):
    n = x2.shape[0]
    tm = 512
    full2 = lambda a_: pl.BlockSpec(a_.shape, lambda i: (0, 0))
    return pl.pallas_call(
        _router_kernel,
        grid=(n // tm,),
        in_specs=[pl.BlockSpec((tm, D_MODEL), lambda i: (i, 0)), full2(wh), full2(wl), full2(bias)],
        out_specs=pl.BlockSpec((tm, LANES), lambda i: (i, 0)),
        out_shape=jax.ShapeDtypeStruct((n, LANES), F32),
        compiler_params=_cparams(("arbitrary",)),
        name="router",
    )(x2, wh, wl, bias)


def _moe_kernel(x_ref, cmb_ref, wgu_ref, wd_ref, g_ref, b_ref, o_ref, xb_scr, acc_scr):
    e = pl.program_id(1)

    @pl.when(e == 0)
    def _():
        xb_scr[...] = x_ref[...].astype(BF16)
        acc_scr[...] = jnp.zeros(acc_scr.shape, F32)

    gu = _dot(xb_scr[...], wgu_ref[0])
    cmb = cmb_ref[...]
    lane = lax.broadcasted_iota(I32, cmb.shape, 1)
    ce = jnp.sum(jnp.where(lane == e, cmb, 0.0), axis=1, keepdims=True)
    hid = _silu(gu[:, :EXPERT_FF]) * gu[:, EXPERT_FF:] * ce
    acc_scr[...] += _dot(hid.astype(BF16), wd_ref[0])

    @pl.when(e == pl.num_programs(1) - 1)
    def _():
        o_ref[...] = _layer_norm_rows(ALPHA * x_ref[...] + acc_scr[...], g_ref[...], b_ref[...])


def _moe_ln(x2, cmb, wgu, wd, gain, bias):
    n = x2.shape[0]
    tm = 1024 if n > 1024 else 512
    full2 = lambda a_: pl.BlockSpec(a_.shape, lambda i, e: (0, 0))
    return pl.pallas_call(
        _moe_kernel,
        grid=(n // tm, N_EXPERTS),
        in_specs=[pl.BlockSpec((tm, D_MODEL), lambda i, e: (i, 0)),
                  pl.BlockSpec((tm, LANES), lambda i, e: (i, 0)),
                  pl.BlockSpec((1, D_MODEL, 2 * EXPERT_FF), lambda i, e: (e, 0, 0)),
                  pl.BlockSpec((1, EXPERT_FF, D_MODEL), lambda i, e: (e, 0, 0)),
                  full2(gain), full2(bias)],
        out_specs=pl.BlockSpec((tm, D_MODEL), lambda i, e: (i, 0)),
        out_shape=jax.ShapeDtypeStruct((n, D_MODEL), F32),
        scratch_shapes=[pltpu.VMEM((tm, D_MODEL), BF16), pltpu.VMEM((tm, D_MODEL), F32)],
        compiler_params=_cparams(("arbitrary", "arbitrary"), 48),
        name="moe_ln2",
    )(x2, cmb, wgu, wd, gain, bias)


def _layer_weights(l, w_in, w_out, gv_gain, gv_bias, ws, bs, ret_gain, ret_bias, ln1_g, ln1_b, ln2_g, ln2_b,
                   rg_w, rg_b, re_w, re_b, e_gate, e_up, e_down):
    w = w_in[l]
    z = lambda k: jnp.zeros((D_MODEL, k), w.dtype)
    ik_end = 2560 + IDX_DIM
    iw_end = ik_end + IDX_HEADS
    w_pad = jnp.concatenate([w[:, :ik_end], z(COL_IW - COL_IK - IDX_DIM), w[:, ik_end:iw_end],
                             z(COL_CQ - COL_IW - IDX_HEADS), w[:, iw_end:]], axis=1).astype(BF16)
    wo = w_out[l].astype(BF16)
    wr = jnp.concatenate([re_w[l], rg_w[l], jnp.zeros((D_MODEL, LANES - N_EXPERTS - N_GROUPS), F32)], axis=1)
    wrh = wr.astype(BF16)
    wrl = (wr - wrh.astype(F32)).astype(BF16)
    br = jnp.concatenate([re_b[l], rg_b[l], jnp.zeros((LANES - N_EXPERTS - N_GROUPS,), F32)])[None, :]
    row = lambda a: a[l].reshape(1, -1)
    return dict(
        w_pad=w_pad, wa=wo[:A_WIDTH], wb=wo[A_WIDTH:A_WIDTH + B_WIDTH], wc=wo[A_WIDTH + B_WIDTH:],
        gv_gain=row(gv_gain), gv_bias=row(gv_bias), ws=ws[l], bs=bs[l],
        ret_gain=row(ret_gain), ret_bias=row(ret_bias),
        ln1_g=row(ln1_g), ln1_b=row(ln1_b), ln2_g=row(ln2_g), ln2_b=row(ln2_b),
        wrh=wrh, wrl=wrl, br=br,
        wgu=jnp.concatenate([e_gate[l], e_up[l]], axis=2).astype(BF16), wd=e_down[l].astype(BF16))


def _channel_mix(x2, a_out, b_out, c_out, lw):
    x1 = _outproj_ln(a_out, b_out, c_out, lw["wa"], lw["wb"], lw["wc"], x2, lw["ln1_g"], lw["ln1_b"])
    cmb = _router(x1, lw["wrh"], lw["wrl"], lw["br"])
    return _moe_ln(x1, cmb, lw["wgu"], lw["wd"], lw["ln2_g"], lw["ln2_b"])


def _heads(t2, b, l, nh):
    return t2.reshape(b, l, nh, t2.shape[-1] // nh)


def _prompt_layer(x3, lw, tt, cosf, sins):
    b, l, _ = x3.shape
    x2 = x3.reshape(b * l, D_MODEL)
    ha, kt, vt, vtb, ikt, kh, qt, iqt, ikb, iwt = _proj_prompt(x3, lw["w_pad"])
    bsb = jnp.repeat(lw["bs"].T, HEAD_DIM, axis=1)
    a_out, a_vn = _gmlp(ha, PCOL_AU, PCOL_AV, lw["gv_gain"], lw["gv_bias"], lw["ws"], bsb)
    topk = min(TOPK_MAX, l // 4)
    b_out = _dsa_prompt(iqt, iwt, ikb, qt, kh, vtb.reshape(b, B_HEADS, HEAD_DIM, l), tt, topk)
    s0e = jnp.zeros((b, C_WIDTH, C_WIDTH), F32)
    c_out, s_fin = _retention(ha.reshape(b, l, P_WIDTH), (PCOL_CQ, PCOL_CK, PCOL_CV, PCOL_CG), cosf, sins,
                              lw["ret_gain"], lw["ret_bias"], s0e, RET_CHUNK, 1)
    y = _channel_mix(x2, a_out, b_out.reshape(b * l, B_WIDTH), c_out.reshape(b * l, C_WIDTH), lw)
    last = ((l - 1) // CHUNK) * CHUNK
    to_rows = lambda t: jnp.transpose(t.reshape(b, B_HEADS, HEAD_DIM, l), (0, 3, 1, 2))
    state = (to_rows(kt), to_rows(vt), jnp.transpose(ikt, (0, 2, 1)), _state_extract(s_fin),
             a_vn.reshape(b, l, A_WIDTH)[:, last:])
    return y.reshape(b, l, D_MODEL), state


def _sample_layer(x3, lw, layer, cache_k, cache_v, cache_ik, state_l, pt_flat, npages, bfar, bnear, cosf, sins):
    bd, t, _ = x3.shape
    n = bd * t
    past = npages * PAGE_SIZE
    x2 = x3.reshape(n, D_MODEL)
    h2 = _proj(x2, lw["w_pad"])
    col = lambda c0, wdt: h2[:, c0:c0 + wdt]
    rep = CHUNK // t
    eye = jnp.eye(rep, dtype=F32)
    ws_t = lw["ws"][:, :t, :t]
    ws_bd = (eye[None, :, None, :, None] * ws_t[:, None, :, None, :]).reshape(A_GROUPS, CHUNK, CHUNK)
    bsb = jnp.repeat(jnp.tile(lw["bs"][:, :t], (1, rep)).T, HEAD_DIM, axis=1)
    a_out, a_vn = _gmlp(h2, COL_AU, COL_AV, lw["gv_gain"], lw["gv_bias"], ws_bd, bsb)
    k4 = _heads(col(COL_K, B_WIDTH), bd, t, B_HEADS)
    v4 = _heads(col(COL_V, B_WIDTH), bd, t, B_HEADS)
    ik3 = col(COL_IK, IDX_DIM).reshape(bd, t, IDX_DIM)
    feat_major = lambda a3: jnp.pad(jnp.transpose(a3, (0, 2, 1)), ((0, 0), (0, 0), (0, PAGE_SIZE - t)))
    iq4 = _heads(col(COL_IQ, IDX_HEADS * IDX_DIM), bd, t, IDX_HEADS)
    iqs = jnp.transpose(iq4, (0, 2, 1, 3)).reshape(bd, IDX_HEADS * t, IDX_DIM).astype(BF16)
    iw3 = jnp.transpose(col(COL_IW, IDX_HEADS).reshape(bd, t, IDX_HEADS), (0, 2, 1))
    iws = jnp.broadcast_to(iw3.reshape(bd, IDX_HEADS * t, 1), (bd, IDX_HEADS * t, LANES))
    scores = _sample_scores(pt_flat, iqs, iws, jnp.transpose(cache_ik, (0, 1, 3, 2)), layer, feat_major(ik3), npages)
    width = (npages + 1) * PAGE_SIZE
    topk = min(TOPK_MAX, (past + t) // 4)
    madd = _sample_mask(scores.reshape(n, width), topk, past, t).reshape(bd, t, width)
    q4 = _heads(col(COL_Q, B_WIDTH), bd, t, B_HEADS)
    eye_h = jnp.eye(B_HEADS, dtype=F32)
    qbd = (jnp.transpose(q4, (0, 2, 1, 3))[:, :, :, None, :] * eye_h[None, :, None, :, None])
    qbd = qbd.reshape(bd, B_HEADS * t, B_WIDTH).astype(BF16)
    page_view = lambda c: jnp.transpose(c, (0, 1, 3, 4, 2)).reshape(c.shape[0], c.shape[1], B_WIDTH, PAGE_SIZE)
    b_out = _sample_attend(pt_flat, qbd, madd, bfar, bnear, page_view(cache_k), page_view(cache_v), layer,
                           feat_major(k4.reshape(bd, t, B_WIDTH)), feat_major(v4.reshape(bd, t, B_WIDTH)), npages)
    c_out, s_fin = _retention(h2.reshape(bd, t, H_WIDTH), (COL_CQ, COL_CK, COL_CV, COL_CG), cosf, sins,
                              lw["ret_gain"], lw["ret_bias"], _state_embed(state_l), t, 8)
    y = _channel_mix(x2, a_out, b_out.reshape(n, B_WIDTH), c_out.reshape(n, C_WIDTH), lw)
    state = (k4, v4, ik3, _state_extract(s_fin), a_vn.reshape(bd, t, A_WIDTH))
    return y.reshape(bd, t, D_MODEL), state


def _distance_tables(rel_bias, t, past):
    r = jnp.arange(Q_BLOCK, dtype=I32)
    d_prompt = jnp.concatenate([off * Q_BLOCK + r[None, :] - r[:, None] for off in range(3)], axis=0)
    qpos = past + jnp.arange(t, dtype=I32)
    near0 = past - PAGE_SIZE
    d_near = [qpos[:, None] - (near0 + half * PAGE_SIZE + r[None, :]) for half in range(2)]
    d_far = jnp.full((t, LANES), MAX_DISTANCE * 2, I32)
    n0 = 3 * Q_BLOCK
    tables = _bias_tables(jnp.concatenate([d_prompt] + d_near + [d_far], axis=0), rel_bias, n0)
    tt = tables[:, :n0].reshape(B_HEADS, 3, Q_BLOCK, Q_BLOCK)
    bnear = jnp.concatenate([tables[:, n0:n0 + t], tables[:, n0 + t:n0 + 2 * t]], axis=2).reshape(B_HEADS * t, 2 * LANES)
    bfar = tables[:, n0 + 2 * t:n0 + 3 * t].reshape(B_HEADS * t, LANES)
    return tt, bnear, bfar


def kernel(x_prompt, x_sample, cache_k, cache_v, cache_idx_k, state_ret, page_table, w_in, w_out, gmlp_v_gain,
           gmlp_v_bias, gmlp_ws, gmlp_bs, rel_bias, ret_gn_gain, ret_gn_bias, ln1_gain, ln1_bias, ln2_gain, ln2_bias,
           router_group_w, router_group_b, router_expert_w, router_expert_b, expert_w_gate, expert_w_up,
           expert_w_down):
    depth = w_in.shape[0]
    seq = x_prompt.shape[1]
    bd, t, _ = x_sample.shape
    npages = page_table.shape[1]
    past = npages * PAGE_SIZE
    pt_flat = page_table.reshape(-1).astype(I32)
    tt, bnear, bfar = _distance_tables(rel_bias, t, past)
    cos_p, sin_p = _rope_tables(jnp.arange(seq, dtype=I32))
    cos_s, sin_s = _rope_tables(past + jnp.arange(t, dtype=I32))
    xp, xs = x_prompt, x_sample
    st_p, st_s = [], []
    for l in range(depth):
        lw = _layer_weights(l, w_in, w_out, gmlp_v_gain, gmlp_v_bias, gmlp_ws, gmlp_bs, ret_gn_gain, ret_gn_bias,
                            ln1_gain, ln1_bias, ln2_gain, ln2_bias, router_group_w, router_group_b,
                            router_expert_w, router_expert_b, expert_w_gate, expert_w_up, expert_w_down)
        xp, sp = _prompt_layer(xp, lw, tt, cos_p, sin_p)
        xs, ss = _sample_layer(xs, lw, l, cache_k, cache_v, cache_idx_k, state_ret[l], pt_flat, npages,
                               bfar, bnear, cos_s, sin_s)
        st_p.append(sp)
        st_s.append(ss)
    stk = lambda sts, i: jnp.stack([s[i] for s in sts], axis=0)
    return (xp, xs,
            stk(st_p, 0), stk(st_p, 1), stk(st_p, 2), stk(st_p, 3), stk(st_p, 4),
            stk(st_s, 0), stk(st_s, 1), stk(st_s, 2), stk(st_s, 3), stk(st_s, 4))
```

```python
import functools
import math

import jax
import jax.numpy as jnp
from jax import lax
from jax.experimental import pallas as pl
from jax.experimental.pallas import tpu as pltpu

D_MODEL = 1024
HEAD_DIM = 64
A_GROUPS = 4
A_WIDTH = 256
CHUNK = 128
B_HEADS = 8
B_WIDTH = 512
IDX_HEADS = 8
IDX_DIM = 64
TOPK_MAX = 256
Q_BLOCK = 128
KEY_TILE = 256
PHASE_UNROLL = 2
NUM_BUCKETS = 32
MAX_DISTANCE = 128
C_HEADS = 4
C_WIDTH = 256
RET_CHUNK = 128
ROPE_BASE = 10000.0
PAGE_SIZE = 128
N_GROUPS = 4
EXPERTS_PER_GROUP = 8
N_EXPERTS = 32
EXPERT_FF = 256
DEPTH = 2
ALPHA = (2 * DEPTH) ** 0.25
LN_EPS = 1e-5

F32 = jnp.float32
BF16 = jnp.bfloat16
I32 = jnp.int32
I16 = jnp.int16
LANES = 128
MASK_NEG = -1e30
INT_MIN = -2 ** 31

COL_AU, COL_AV, COL_Q, COL_K, COL_V, COL_IQ, COL_IK, COL_IW = 0, 256, 512, 1024, 1536, 2048, 2560, 2688
COL_CQ, COL_CK, COL_CV, COL_CG = 2816, 3072, 3328, 3584
H_WIDTH = 3840


def _cparams(sem, vmem_mb=None):
    kw = dict(dimension_semantics=sem)
    if vmem_mb is not None:
        kw["vmem_limit_bytes"] = vmem_mb << 20
    return pltpu.CompilerParams(**kw)


def _nt_dot(a, b):
    return lax.dot_general(a, b, (((1,), (1,)), ((), ())), preferred_element_type=F32)


def _dot(a, b):
    return jnp.dot(a, b, preferred_element_type=F32)


def _layer_norm_rows(x, gain, bias):
    mu = jnp.mean(x, axis=-1, keepdims=True)
    xc = x - mu
    var = jnp.mean(xc * xc, axis=-1, keepdims=True)
    return xc * lax.rsqrt(var + LN_EPS) * gain + bias


def _silu(x):
    return x * (1.0 / (1.0 + jnp.exp(-x)))


def _proj_kernel(x_ref, w_ref, o_ref):
    o_ref[...] = _dot(x_ref[...].astype(BF16), w_ref[...])


def _proj(x2, w_pad):
    n = x2.shape[0]
    tm = 1024 if n > 1024 else 512
    tn = 768
    return pl.pallas_call(
        _proj_kernel,
        grid=(n // tm, H_WIDTH // tn),
        in_specs=[pl.BlockSpec((tm, D_MODEL), lambda i, j: (i, 0)),
                  pl.BlockSpec((D_MODEL, tn), lambda i, j: (0, j))],
        out_specs=pl.BlockSpec((tm, tn), lambda i, j: (i, j)),
        out_shape=jax.ShapeDtypeStruct((n, H_WIDTH), F32),
        compiler_params=_cparams(("arbitrary", "arbitrary"), 40),
        name="in_proj",
    )(x2, w_pad)


PCOL_AU, PCOL_AV, PCOL_CQ, PCOL_CK, PCOL_CV, PCOL_CG = 0, 256, 512, 768, 1024, 1280
P_WIDTH = 1536


def _proj_prompt_kernel(x_ref, w_ref, ha_ref, kt_ref, vt_ref, vtb_ref, ikt_ref, kh_ref, qt_ref, iqt_ref, ikb_ref,
                        iwt_ref):
    x = x_ref[0].astype(BF16)
    piece = lambda c0, width: _dot(x, w_ref[:, c0:c0 + width])
    ha_ref[:, :COL_Q] = piece(COL_AU, COL_Q)
    ha_ref[:, COL_Q:] = piece(COL_CQ, H_WIDTH - COL_CQ)
    qt_ref[0] = jnp.transpose(piece(COL_Q, B_WIDTH) * (HEAD_DIM ** -0.5 * LOG2E)).astype(BF16)
    iqt_ref[0] = jnp.transpose(piece(COL_IQ, IDX_HEADS * IDX_DIM)).astype(BF16)
    k = piece(COL_K, B_WIDTH)
    for h in range(B_HEADS):
        kh_ref[0, h] = k[:, h * HEAD_DIM:(h + 1) * HEAD_DIM].astype(BF16)
    kt_ref[0] = jnp.transpose(k)
    vt = jnp.transpose(piece(COL_V, B_WIDTH))
    vt_ref[0] = vt
    vtb_ref[0] = vt.astype(BF16)
    ik = piece(COL_IK, LANES)
    ikb_ref[0] = ik[:, :IDX_DIM].astype(BF16)
    ikt_ref[0] = jnp.transpose(ik)[:IDX_DIM]
    iwt_ref[0] = jnp.transpose(piece(COL_IW, LANES))[:IDX_HEADS]


def _proj_prompt(x3, w_pad):
    b, l, _ = x3.shape
    tm = 512
    nt = l // tm
    f32s = lambda *shape: jax.ShapeDtypeStruct(shape, F32)
    bf16s = lambda *shape: jax.ShapeDtypeStruct(shape, BF16)
    feat = lambda rows: pl.BlockSpec((1, rows, tm), lambda i, j: (i, 0, j))
    hmaj = pl.BlockSpec((1, B_HEADS, tm, HEAD_DIM), lambda i, j: (i, 0, j, 0))
    return pl.pallas_call(
        _proj_prompt_kernel,
        grid=(b, nt),
        in_specs=[pl.BlockSpec((1, tm, D_MODEL), lambda i, j: (i, j, 0)),
                  pl.BlockSpec(w_pad.shape, lambda i, j: (0, 0), pipeline_mode=pl.Buffered(1))],
        out_specs=[pl.BlockSpec((tm, P_WIDTH), lambda i, j: (i * nt + j, 0)),
                   feat(B_WIDTH), feat(B_WIDTH), feat(B_WIDTH), feat(IDX_DIM),
                   hmaj, feat(B_WIDTH), feat(IDX_HEADS * IDX_DIM),
                   pl.BlockSpec((1, tm, IDX_DIM), lambda i, j: (i, j, 0)),
                   feat(IDX_HEADS)],
        out_shape=[f32s(b * l, P_WIDTH), f32s(b, B_WIDTH, l), f32s(b, B_WIDTH, l), bf16s(b, B_WIDTH, l),
                   f32s(b, IDX_DIM, l), bf16s(b, B_HEADS, l, HEAD_DIM), bf16s(b, B_WIDTH, l),
                   bf16s(b, IDX_HEADS * IDX_DIM, l), bf16s(b, l, IDX_DIM), f32s(b, IDX_HEADS, l)],
        compiler_params=_cparams(("arbitrary", "arbitrary"), 48),
        name="in_proj_prompt",
    )(x3, w_pad)


LOG2E = math.log2(math.e)


def _bias_kernel(rb_ref, d_ref, o_ref, *, log2_rows):
    n = jnp.maximum(d_ref[...], 0)
    max_exact = NUM_BUCKETS // 2
    nf = jnp.maximum(n, 1).astype(F32)
    large = max_exact + (jnp.log(nf / max_exact) / math.log(MAX_DISTANCE / max_exact)
                         * (NUM_BUCKETS - max_exact)).astype(I32)
    large = jnp.minimum(large, NUM_BUCKETS - 1)
    bucket = jnp.where(n < max_exact, n, large)
    for h in range(B_HEADS):
        acc = jnp.zeros(bucket.shape, F32)
        for bk in range(NUM_BUCKETS):
            acc = jnp.where(bucket == bk, rb_ref[bk * B_HEADS + h], acc)
        o_ref[h, :log2_rows] = acc[:log2_rows] * LOG2E
        o_ref[h, log2_rows:] = acc[log2_rows:]


def _bias_tables(dist, rel_bias, log2_rows):
    r = dist.shape[0]
    return pl.pallas_call(
        functools.partial(_bias_kernel, log2_rows=log2_rows),
        in_specs=[pl.BlockSpec(memory_space=pltpu.SMEM),
                  pl.BlockSpec((r, LANES), lambda: (0, 0))],
        out_specs=pl.BlockSpec((B_HEADS, r, LANES), lambda: (0, 0, 0)),
        out_shape=jax.ShapeDtypeStruct((B_HEADS, r, LANES), F32),
        name="bias_tables",
    )(rel_bias.reshape(-1), dist)


def _gmlp_kernel(u_ref, v_ref, g_ref, b_ref, ws_ref, bsb_ref, o_ref, vn_ref, *, nchunk):
    r = lax.broadcasted_iota(I32, (CHUNK, CHUNK), 0)
    c = lax.broadcasted_iota(I32, (CHUNK, CHUNK), 1)
    grp = lax.broadcasted_iota(I32, (CHUNK, A_WIDTH), 1) // HEAD_DIM
    wts = [jnp.where(r >= c, ws_ref[g], 0.0).astype(BF16) for g in range(A_GROUPS)]
    gain, bias, bsb = g_ref[...], b_ref[...], bsb_ref[...]
    for ci in range(nchunk):
        sl = pl.ds(ci * CHUNK, CHUNK)
        vn = _layer_norm_rows(v_ref[sl, :], gain, bias)
        vn_ref[sl, :] = vn
        vb = vn.astype(BF16)
        mixed = bsb
        for g in range(A_GROUPS):
            mixed = mixed + jnp.where(grp == g, _dot(wts[g], vb), 0.0)
        o_ref[sl, :] = (u_ref[sl, :] * mixed).astype(BF16)


def _gmlp(h2, col_u, col_v, gain, bias, ws, bsb):
    n = h2.shape[0]
    tm = min(n, 1024)
    blk = lambda col: pl.BlockSpec((tm, A_WIDTH), lambda i: (i, col // A_WIDTH))
    full2 = lambda a: pl.BlockSpec(a.shape, lambda i: (0, 0))
    return pl.pallas_call(
        functools.partial(_gmlp_kernel, nchunk=tm // CHUNK),
        grid=(n // tm,),
        in_specs=[blk(col_u), blk(col_v), full2(gain), full2(bias),
                  pl.BlockSpec(ws.shape, lambda i: (0, 0, 0)), full2(bsb)],
        out_specs=[pl.BlockSpec((tm, A_WIDTH), lambda i: (i, 0)),
                   pl.BlockSpec((tm, A_WIDTH), lambda i: (i, 0))],
        out_shape=[jax.ShapeDtypeStruct((n, A_WIDTH), BF16),
                   jax.ShapeDtypeStruct((n, A_WIDTH), F32)],
        compiler_params=_cparams(("arbitrary",)),
        name="gmlp",
    )(h2, h2, gain, bias, ws, bsb)


def _ret_tables(c):
    log_g = jnp.log(1.0 - 2.0 ** (-5.0 - jnp.arange(C_HEADS, dtype=F32)))
    i = jnp.arange(c, dtype=F32)
    diff = i[:, None] - i[None, :]
    dmat = jnp.where(diff >= 0, jnp.exp(log_g[:, None, None] * jnp.maximum(diff, 0.0)), 0.0)
    q_dec = jnp.exp(log_g[:, None] * (i + 1.0))
    k_dec = jnp.exp(log_g[:, None] * (c - 1.0 - i))
    s_dec = jnp.exp(log_g * c)
    qd = jnp.repeat(q_dec.T, HEAD_DIM, axis=1)
    kd = jnp.repeat(k_dec.T, HEAD_DIM, axis=1)
    hid = jnp.arange(C_WIDTH) // HEAD_DIM
    same = hid[:, None] == hid[None, :]
    sd = jnp.where(same, s_dec[hid][:, None], 0.0)
    return dmat, qd, kd, sd, same.astype(F32)


def _rope_tables(pos):
    half = HEAD_DIM // 2
    inv = ROPE_BASE ** (-jnp.arange(half, dtype=F32) / half)
    ang = pos.astype(F32)[:, None] * inv[None, :]
    cos, sin = jnp.cos(ang), jnp.sin(ang)
    cosf = jnp.tile(jnp.concatenate([cos, cos], axis=1), (1, C_HEADS))
    sins = jnp.tile(jnp.concatenate([-sin, sin], axis=1), (1, C_HEADS))
    return cosf, sins


def _ret_kernel(q_ref, k_ref, v_ref, g_ref, cos_ref, sin_ref, qd_ref, kd_ref, dm_ref, sd_ref, bd_ref,
                gg_ref, gb_ref, s0_ref, o_ref, sf_ref, s_scr, *, bt, c):
    ci = pl.program_id(1)

    @pl.when(ci == 0)
    def _():
        s_scr[...] = s0_ref[...]

    lane = lax.broadcasted_iota(I32, (c, C_WIDTH), 1)
    hid = lane // HEAD_DIM
    first = (lane % HEAD_DIM) < (HEAD_DIM // 2)
    cosf, sins = cos_ref[...], sin_ref[...]
    half = HEAD_DIM // 2

    def rot(x):
        partner = jnp.where(first, pltpu.roll(x, C_WIDTH - half, 1), pltpu.roll(x, half, 1))
        return x * cosf + partner * sins

    def seg_mean(x):
        out = jnp.zeros_like(x)
        for h in range(C_HEADS):
            hm = hid == h
            s = jnp.sum(jnp.where(hm, x, 0.0), axis=1, keepdims=True) * (1.0 / HEAD_DIM)
            out = jnp.where(hm, s, out)
        return out

    for bb in range(bt):
        q = rot(q_ref[bb])
        k = rot(k_ref[bb]) * (HEAD_DIM ** -0.5)
        v = v_ref[bb]
        kb, vb = k.astype(BF16), v.astype(BF16)
        s_old = s_scr[bb]
        o = _dot(q.astype(BF16), s_old.astype(BF16)) * qd_ref[...]
        for h in range(C_HEADS):
            hm = hid == h
            att = _nt_dot(jnp.where(hm, q, 0.0).astype(BF16), kb) * dm_ref[h]
            o = o + jnp.where(hm, _dot(att.astype(BF16), vb), 0.0)
        kdt = jnp.transpose(k * kd_ref[...]).astype(BF16)
        s_scr[bb] = s_old * sd_ref[...] + bd_ref[...] * _dot(kdt, vb)
        mu = seg_mean(o)
        oc = o - mu
        var = seg_mean(oc * oc)
        normed = oc * lax.rsqrt(var + LN_EPS) * gg_ref[...] + gb_ref[...]
        o_ref[bb] = (_silu(g_ref[bb]) * normed).astype(BF16)

    @pl.when(ci == pl.num_programs(1) - 1)
    def _():
        sf_ref[...] = s_scr[...]


def _retention(h3, cols, cosf, sins, gn_gain, gn_bias, s0e, c, bt):
    b, l, _ = h3.shape
    dmat, qd, kd, sd, bd = _ret_tables(c)
    blk = lambda col: pl.BlockSpec((bt, c, C_WIDTH), lambda i, j: (i, j, col // C_WIDTH))
    const2 = lambda a: pl.BlockSpec(a.shape, lambda i, j: (0, 0))
    pos_blk = pl.BlockSpec((c, C_WIDTH), lambda i, j: (j, 0))
    st_blk = pl.BlockSpec((bt, C_WIDTH, C_WIDTH), lambda i, j: (i, 0, 0))
    return pl.pallas_call(
        functools.partial(_ret_kernel, bt=bt, c=c),
        grid=(b // bt, l // c),
        in_specs=[blk(cols[0]), blk(cols[1]), blk(cols[2]), blk(cols[3]), pos_blk, pos_blk,
                  const2(qd), const2(kd), pl.BlockSpec(dmat.shape, lambda i, j: (0, 0, 0)),
                  const2(sd), const2(bd), const2(gn_gain), const2(gn_bias), st_blk],
        out_specs=[pl.BlockSpec((bt, c, C_WIDTH), lambda i, j: (i, j, 0)), st_blk],
        out_shape=[jax.ShapeDtypeStruct((b, l, C_WIDTH), BF16),
                   jax.ShapeDtypeStruct((b, C_WIDTH, C_WIDTH), F32)],
        scratch_shapes=[pltpu.VMEM((bt, C_WIDTH, C_WIDTH), F32)],
        compiler_params=_cparams(("arbitrary", "arbitrary")),
        name="retention",
    )(h3, h3, h3, h3, cosf, sins, qd, kd, dmat, sd, bd, gn_gain, gn_bias, s0e)


def _state_embed(s):
    b = s.shape[0]
    eye = jnp.eye(C_HEADS, dtype=s.dtype)
    return (s[:, :, :, None, :] * eye[None, :, None, :, None]).reshape(b, C_WIDTH, C_WIDTH)


def _state_extract(se):
    b = se.shape[0]
    s5 = se.reshape(b, C_HEADS, HEAD_DIM, C_HEADS, HEAD_DIM)
    return jnp.stack([s5[:, h, :, h, :] for h in range(C_HEADS)], axis=1)


def _sortable(score):
    bits = pltpu.bitcast(score, I32)
    return bits ^ ((bits >> 31) & 0x7FFFFFFF)


def _dsa_kernel(iq_ref, iw_ref, ik_ref, q_ref, k_ref, vt_ref, tt_ref, o_ref,
                sc_scr, hi_scr, lo_scr, s_scr, acc_scr, j_scr, *, topk):
    qb = pl.program_id(1)
    ntile = (qb + KEY_TILE // Q_BLOCK) // (KEY_TILE // Q_BLOCK)
    row = lax.broadcasted_iota(I32, (KEY_TILE, Q_BLOCK), 0)
    col = lax.broadcasted_iota(I32, (KEY_TILE, Q_BLOCK), 1)
    qpos = qb * Q_BLOCK + col
    head_rows = lambda ref, h: ref[0, h * HEAD_DIM:(h + 1) * HEAD_DIM, :]
    iq_all = jnp.concatenate([head_rows(iq_ref, h) for h in range(IDX_HEADS)], axis=1)
    iw = iw_ref[0]
    int_min = jnp.int32(INT_MIN)
    score_scale = IDX_HEADS ** -0.5

    def tile_rows(j):
        return pl.ds(pl.multiple_of(j * KEY_TILE, KEY_TILE), KEY_TILE)

    def tree_sum(parts):
        while len(parts) > 1:
            parts = [a + b for a, b in zip(parts[::2], parts[1::2])]
        return parts[0]

    def score_tile(j, carry):
        s = _dot(ik_ref[0, tile_rows(j), :], iq_all)
        terms = [jnp.maximum(s[:, h * Q_BLOCK:(h + 1) * Q_BLOCK], 0.0) * iw[h:h + 1, :] for h in range(IDX_HEADS)]
        acc = terms[0]
        for term in terms[1:]:
            acc = acc + term
        score = (acc * (IDX_DIM ** -0.5)) * score_scale
        kint = jnp.where(j * KEY_TILE + row <= qpos, _sortable(score), int_min)
        sc_scr[tile_rows(j), :] = kint
        hi_scr[tile_rows(j), :] = (kint >> 16).astype(I16)
        return carry

    nstep = (ntile + PHASE_UNROLL - 1) // PHASE_UNROLL

    def unrolled(tile_fn):
        def step(js, carry):
            for u in range(PHASE_UNROLL):
                carry = tile_fn(js * PHASE_UNROLL + u, carry)
            return carry
        return step

    lax.fori_loop(0, nstep, unrolled(score_tile), 0)

    def count(pred):
        def body(j, acc):
            m = jnp.where(pred(sc_scr[tile_rows(j), :], j), 1.0, 0.0)
            return acc + tree_sum([m[i * 8:(i + 1) * 8] for i in range(KEY_TILE // 8)])
        acc = lax.fori_loop(0, ntile, body, jnp.zeros((8, Q_BLOCK), F32))
        return jnp.sum(acc, axis=0, keepdims=True)

    half_min = -(1 << 15)
    ones16, zeros16 = jnp.ones((16, Q_BLOCK), I16), jnp.zeros((16, Q_BLOCK), I16)

    def count16(ref, cand, strict=False):
        cand16 = jnp.broadcast_to(cand, (16, Q_BLOCK)).astype(I16)

        def tile(j, acc):
            x = ref[tile_rows(j), :]
            hits = []
            for i in range(KEY_TILE // 16):
                xi = x[i * 16:(i + 1) * 16]
                hits.append(jnp.where((xi > cand16) if strict else (xi >= cand16), ones16, zeros16))
            return acc + tree_sum(hits)

        acc = lax.fori_loop(0, nstep, unrolled(tile), zeros16)
        return jnp.sum(acc.astype(F32), axis=0, keepdims=True)

    def high_bit(i, ans):
        cand = ans | lax.shift_left(jnp.int32(1), 15 - i)
        return jnp.where(count16(hi_scr, cand + half_min) >= topk, cand, ans)

    t_hi = lax.fori_loop(0, 16, high_bit, jnp.zeros((1, Q_BLOCK), I32)) + half_min
    n_above = count16(hi_scr, t_hi, strict=True)

    def low_halves(j, carry):
        t = sc_scr[tile_rows(j), :]
        lo_scr[tile_rows(j), :] = jnp.where((t >> 16) == t_hi, (t & 0xFFFF) + half_min, half_min).astype(I16)
        return carry

    lax.fori_loop(0, nstep, unrolled(low_halves), 0)

    def low_bit(i, ans):
        cand = ans | lax.shift_left(jnp.int32(1), 15 - i)
        return jnp.where(n_above + count16(lo_scr, cand + half_min) >= topk, cand, ans)

    t_lo = lax.fori_loop(0, 16, low_bit, jnp.zeros((1, Q_BLOCK), I32))
    thr = lax.shift_left(t_hi, 16) | t_lo
    need = topk - count(lambda t, j: t > thr)
    n_ge = count(lambda t, j: t >= thr)

    j_scr[...] = jnp.full(j_scr.shape, 1 << 20, I32)
    has_tie = jnp.max(jnp.where((n_ge > topk) & (thr != int_min), 1.0, 0.0)) > 0.0

    @pl.when(has_tie)
    def _():
        def jbit(i, jc):
            cand = jc | lax.shift_left(jnp.int32(1), 12 - i)
            cnt = count(lambda t, j: (t == thr) & (j * KEY_TILE + row < cand))
            return jnp.where(cnt <= need, cand, jc)
        jc = lax.fori_loop(0, 13, jbit, jnp.zeros((1, Q_BLOCK), I32))
        j_scr[...] = jnp.broadcast_to(jc, j_scr.shape)

    j_cut = jnp.where(thr == int_min, 0, j_scr[0:1, :])

    q_scaled = [head_rows(q_ref, h) for h in range(B_HEADS)]
    sub_tiles = KEY_TILE // Q_BLOCK
    groups = KEY_TILE // 8

    def tree_max(parts):
        while len(parts) > 1:
            parts = [jnp.maximum(a, b) for a, b in zip(parts[::2], parts[1::2])]
        return parts[0]

    def logits_tile(j, m_part):
        t = sc_scr[tile_rows(j), :]
        kidx = j * KEY_TILE + row
        madd = jnp.where(t > thr, 0.0, jnp.where(t == thr, jnp.where(kidx < j_cut, 0.0, MASK_NEG), MASK_NEG))
        offs = [jnp.clip(qb - (j * sub_tiles + i), 0, 2) for i in range(sub_tiles)]
        new = []
        for h in range(B_HEADS):
            bias = jnp.concatenate([tt_ref[h, off] for off in offs], axis=0)
            s = _dot(k_ref[0, h, tile_rows(j), :], q_scaled[h]) + bias + madd
            s_scr[h, tile_rows(j), :] = s
            new.append(jnp.maximum(m_part[h], tree_max([s[i * 8:(i + 1) * 8] for i in range(groups)])))
        return jnp.stack(new)

    m_part = lax.fori_loop(0, nstep, unrolled(logits_tile), jnp.full((B_HEADS, 8, Q_BLOCK), MASK_NEG, F32))
    m_rows = [jnp.max(m_part[h], axis=0, keepdims=True) for h in range(B_HEADS)]

    acc_scr[...] = jnp.zeros(acc_scr.shape, F32)

    def values_tile(j, l_part):
        new = []
        for h in range(B_HEADS):
            p = jnp.exp2(s_scr[h, tile_rows(j), :] - m_rows[h])
            new.append(l_part[h] + tree_sum([p[i * 8:(i + 1) * 8] for i in range(groups)]))
            acc_scr[h] += _dot(vt_ref[0, h, :, tile_rows(j)], p.astype(BF16))
        return jnp.stack(new)

    l_part = lax.fori_loop(0, nstep, unrolled(values_tile), jnp.zeros((B_HEADS, 8, Q_BLOCK), F32))
    out_t = jnp.concatenate([acc_scr[h] / jnp.sum(l_part[h], axis=0, keepdims=True) for h in range(B_HEADS)], axis=0)
    o_ref[0] = jnp.transpose(out_t).astype(BF16)


def _dsa_prompt(iqt, iwt, ik3, qt, kh, vt, tt, topk):
    b, _, l, _ = kh.shape
    assert l % (KEY_TILE * PHASE_UNROLL) == 0
    qblk = pl.BlockSpec((1, B_WIDTH, Q_BLOCK), lambda i, j: (i, 0, j))
    once = pl.Buffered(1)
    return pl.pallas_call(
        functools.partial(_dsa_kernel, topk=topk),
        grid=(b, l // Q_BLOCK),
        in_specs=[qblk,
                  pl.BlockSpec((1, IDX_HEADS, Q_BLOCK), lambda i, j: (i, 0, j)),
                  pl.BlockSpec((1, l, IDX_DIM), lambda i, j: (i, 0, 0), pipeline_mode=once),
                  qblk,
                  pl.BlockSpec((1, B_HEADS, l, HEAD_DIM), lambda i, j: (i, 0, 0, 0), pipeline_mode=once),
                  pl.BlockSpec((1, B_HEADS, HEAD_DIM, l), lambda i, j: (i, 0, 0, 0), pipeline_mode=once),
                  pl.BlockSpec(tt.shape, lambda i, j: (0, 0, 0, 0), pipeline_mode=once)],
        out_specs=pl.BlockSpec((1, Q_BLOCK, B_WIDTH), lambda i, j: (i, j, 0)),
        out_shape=jax.ShapeDtypeStruct((b, l, B_WIDTH), BF16),
        scratch_shapes=[pltpu.VMEM((l, Q_BLOCK), I32),
                        pltpu.VMEM((l, Q_BLOCK), I16),
                        pltpu.VMEM((l, Q_BLOCK), I16),
                        pltpu.VMEM((B_HEADS, l, Q_BLOCK), F32),
                        pltpu.VMEM((B_HEADS, HEAD_DIM, Q_BLOCK), F32),
                        pltpu.VMEM((8, Q_BLOCK), I32)],
        compiler_params=_cparams(("arbitrary", "arbitrary"), 56),
        name="dsa_prompt",
    )(iqt, iwt, ik3, qt, kh, vt, tt)


def _sidx_kernel(pt_ref, iq_ref, iw_ref, *rest, npages):
    pages, ikn_ref, o_ref = rest[:npages], rest[npages], rest[npages + 1]
    iq, iw = iq_ref[0], iw_ref[0]
    nq = iq.shape[0] // IDX_HEADS
    for j in range(npages + 1):
        keys_t = (pages[j][0, 0] if j < npages else ikn_ref[0]).astype(BF16)
        r = jnp.maximum(_dot(iq, keys_t), 0.0) * iw
        acc = r[0:nq]
        for h in range(1, IDX_HEADS):
            acc = acc + r[h * nq:(h + 1) * nq]
        o_ref[0, :, j * PAGE_SIZE:(j + 1) * PAGE_SIZE] = (acc * (IDX_DIM ** -0.5)) * (IDX_HEADS ** -0.5)


def _sample_scores(pt_flat, iqs, iws, cache_ik, layer, ikn, npages):
    bd, hq, _ = iqs.shape
    nq = hq // IDX_HEADS
    width = (npages + 1) * PAGE_SIZE
    page_spec = lambda p: pl.BlockSpec((1, 1, IDX_DIM, PAGE_SIZE),
                                       lambda i, pt, p=p: (layer, pt[i * npages + p], 0, 0))
    per_b = lambda a: pl.BlockSpec((1,) + a.shape[1:], lambda i, pt: (i, 0, 0))
    return pl.pallas_call(
        functools.partial(_sidx_kernel, npages=npages),
        grid_spec=pltpu.PrefetchScalarGridSpec(
            num_scalar_prefetch=1, grid=(bd,),
            in_specs=[per_b(iqs), per_b(iws)] + [page_spec(p) for p in range(npages)] + [per_b(ikn)],
            out_specs=pl.BlockSpec((1, nq, width), lambda i, pt: (i, 0, 0))),
        out_shape=jax.ShapeDtypeStruct((bd, nq, width), F32),
        compiler_params=_cparams(("arbitrary",)),
        name="sample_scores",
    )(pt_flat, iqs, iws, *([cache_ik] * npages), ikn)


def _sthr_kernel(s_ref, o_ref, k_scr, *, topk, past, nq):
    tr, width = s_ref.shape
    colw = lax.broadcasted_iota(I32, (tr, width), 1)
    rowq = lax.broadcasted_iota(I32, (tr, width), 0) % nq
    vis = colw <= past + rowq
    int_min = jnp.int32(INT_MIN)
    k_scr[...] = jnp.where(vis, _sortable(s_ref[...]), int_min)

    def count(pred):
        return jnp.sum(jnp.where(pred(k_scr[...]), 1.0, 0.0), axis=1, keepdims=True)

    def bit_body(i, ans_u):
        cand_u = ans_u | lax.shift_left(jnp.int32(1), 31 - i)
        cand_s = cand_u ^ int_min
        return jnp.where(count(lambda t: t >= cand_s) >= topk, cand_u, ans_u)

    thr = lax.fori_loop(0, 32, bit_body, jnp.zeros((tr, 1), I32)) ^ int_min
    need = topk - count(lambda t: t > thr)

    def jbit(i, jc):
        cand = jc | lax.shift_left(jnp.int32(1), 12 - i)
        cnt = count(lambda t: (t == thr) & (colw < cand))
        return jnp.where(cnt <= need, cand, jc)

    j_cut = lax.fori_loop(0, 13, jbit, jnp.zeros((tr, 1), I32))
    t = k_scr[...]
    sel = ((t > thr) | ((t == thr) & (colw < j_cut))) & vis
    o_ref[...] = jnp.where(sel, 0.0, MASK_NEG)


def _sample_mask(scores2, topk, past, nq):
    r, width = scores2.shape
    tr = min(r, 128)
    return pl.pallas_call(
        functools.partial(_sthr_kernel, topk=topk, past=past, nq=nq),
        grid=(r // tr,),
        in_specs=[pl.BlockSpec((tr, width), lambda i: (i, 0))],
        out_specs=pl.BlockSpec((tr, width), lambda i: (i, 0)),
        out_shape=jax.ShapeDtypeStruct((r, width), F32),
        scratch_shapes=[pltpu.VMEM((tr, width), I32)],
        compiler_params=_cparams(("arbitrary",)),
        name="sample_mask",
    )(scores2)


def _sattn_kernel(pt_ref, q_ref, m_ref, bfar_ref, bnear_ref, *rest, npages, nq):
    kpages, vpages = rest[:npages], rest[npages:2 * npages]
    kn_ref, vn_ref, o_ref, s_scr = rest[2 * npages:2 * npages + 4]
    q = q_ref[0]
    mx = jnp.full((B_HEADS * nq, PAGE_SIZE), MASK_NEG, F32)
    for j in range(npages + 1):
        kp = (kpages[j][0, 0] if j < npages else kn_ref[0]).astype(BF16)
        cols = slice(j * PAGE_SIZE, (j + 1) * PAGE_SIZE)
        bias = bfar_ref[...] if j < npages - 1 else bnear_ref[:, (j - npages + 1) * PAGE_SIZE:(j - npages + 2) * PAGE_SIZE]
        madd = jnp.concatenate([m_ref[0, :, cols]] * B_HEADS, axis=0)
        s = _dot(q, kp) * (HEAD_DIM ** -0.5) + bias + madd
        s_scr[:, cols] = s
        mx = jnp.maximum(mx, s)
    mrow = jnp.max(mx, axis=1, keepdims=True)
    lsum = jnp.zeros((B_HEADS * nq, PAGE_SIZE), F32)
    acc = jnp.zeros((B_HEADS * nq, B_WIDTH), F32)
    for j in range(npages + 1):
        vp = (vpages[j][0, 0] if j < npages else vn_ref[0]).astype(BF16)
        p = jnp.exp(s_scr[:, j * PAGE_SIZE:(j + 1) * PAGE_SIZE] - mrow)
        lsum = lsum + p
        acc = acc + _nt_dot(p.astype(BF16), vp)
    acc = acc / jnp.sum(lsum, axis=1, keepdims=True)
    head = lax.broadcasted_iota(I32, (nq, B_WIDTH), 1) // HEAD_DIM
    out = jnp.zeros((nq, B_WIDTH), F32)
    for h in range(B_HEADS):
        out = jnp.where(head == h, acc[h * nq:(h + 1) * nq], out)
    o_ref[0] = out.astype(BF16)


def _sample_attend(pt_flat, qbd, madd3, bfar, bnear, cache_k, cache_v, layer, kn, vn, npages):
    bd, hq, _ = qbd.shape
    nq = hq // B_HEADS
    width = (npages + 1) * PAGE_SIZE
    page_spec = lambda p: pl.BlockSpec((1, 1, B_WIDTH, PAGE_SIZE),
                                       lambda i, pt, p=p: (layer, pt[i * npages + p], 0, 0))
    per_b = lambda a: pl.BlockSpec((1,) + a.shape[1:], lambda i, pt: (i, 0, 0))
    const2 = lambda a: pl.BlockSpec(a.shape, lambda i, pt: (0, 0))
    pages = [page_spec(p) for p in range(npages)]
    return pl.pallas_call(
        functools.partial(_sattn_kernel, npages=npages, nq=nq),
        grid_spec=pltpu.PrefetchScalarGridSpec(
            num_scalar_prefetch=1, grid=(bd,),
            in_specs=[per_b(qbd), per_b(madd3), const2(bfar), const2(bnear)] + pages + pages + [per_b(kn), per_b(vn)],
            out_specs=pl.BlockSpec((1, nq, B_WIDTH), lambda i, pt: (i, 0, 0)),
            scratch_shapes=[pltpu.VMEM((hq, width), F32)]),
        out_shape=jax.ShapeDtypeStruct((bd, nq, B_WIDTH), BF16),
        compiler_params=_cparams(("arbitrary",), 56),
        name="sample_attend",
    )(pt_flat, qbd, madd3, bfar, bnear, *([cache_k] * npages), *([cache_v] * npages), kn, vn)


def _outproj_kernel(a_ref, b_ref, c_ref, wa_ref, wb_ref, wc_ref, x_ref, g_ref, bb_ref, o_ref):
    mix = _dot(a_ref[...], wa_ref[...]) + _dot(b_ref[...], wb_ref[...]) + _dot(c_ref[...], wc_ref[...])
    o_ref[...] = _layer_norm_rows(ALPHA * x_ref[...] + mix, g_ref[...], bb_ref[...])


def _outproj_ln(a, b, c, wa, wb, wc, x2, gain, bias):
    n = x2.shape[0]
    tm = 512
    rows = lambda a_: pl.BlockSpec((tm, a_.shape[1]), lambda i: (i, 0))
    full2 = lambda a_: pl.BlockSpec(a_.shape, lambda i: (0, 0))
    return pl.pallas_call(
        _outproj_kernel,
        grid=(n // tm,),
        in_specs=[rows(a), rows(b), rows(c), full2(wa), full2(wb), full2(wc), rows(x2), full2(gain), full2(bias)],
        out_specs=pl.BlockSpec((tm, D_MODEL), lambda i: (i, 0)),
        out_shape=jax.ShapeDtypeStruct((n, D_MODEL), F32),
        compiler_params=_cparams(("arbitrary",), 40),
        name="outproj_ln1",
    )(a, b, c, wa, wb, wc, x2, gain, bias)


def _router_kernel(x_ref, wh_ref, wl_ref, b_ref, o_ref):
    x = x_ref[...]
    xh = x.astype(BF16)
    xl = (x - xh.astype(F32)).astype(BF16)
    logits = _dot(xh, wh_ref[...]) + _dot(xh, wl_ref[...]) + _dot(xl, wh_ref[...]) + b_ref[...]
    lane = lax.broadcasted_iota(I32, logits.shape, 1).astype(F32)
    neg = -jnp.inf
    first_lane = lambda m: jnp.min(jnp.where(m, lane, float(LANES)), axis=1, keepdims=True)
    is_g = (lane >= N_EXPERTS) & (lane < N_EXPERTS + N_GROUPS)
    gmax = jnp.max(jnp.where(is_g, logits, neg), axis=1, keepdims=True)
    g_sel = first_lane(is_g & (logits == gmax)) - N_EXPERTS
    p_group = 1.0 / jnp.sum(jnp.where(is_g, jnp.exp(logits - gmax), 0.0), axis=1, keepdims=True)
    in_g = jnp.floor(lane * (1.0 / EXPERTS_PER_GROUP)) == g_sel
    ev = jnp.where(in_g, logits, neg)
    v1 = jnp.max(ev, axis=1, keepdims=True)
    i1 = first_lane(in_g & (logits == v1))
    ev2 = jnp.where(lane == i1, neg, ev)
    v2 = jnp.max(ev2, axis=1, keepdims=True)
    i2 = first_lane(in_g & (lane != i1) & (logits == v2))
    e2 = jnp.exp(v2 - v1)
    den = 1.0 + e2
    g1 = (1.0 / den) * p_group
    g2 = (e2 / den) * p_group
    o_ref[...] = jnp.where(lane == i1, g1, jnp.where(lane == i2, g2, 0.0))


def _router(x2, wh, wl, bias):
    n = x2.shape[0]
    tm = 512
    full2 = lambda a_: pl.BlockSpec(a_.shape, lambda i: (0, 0))
    return pl.pallas_call(
        _router_kernel,
        grid=(n // tm,),
        in_specs=[pl.BlockSpec((tm, D_MODEL), lambda i: (i, 0)), full2(wh), full2(wl), full2(bias)],
        out_specs=pl.BlockSpec((tm, LANES), lambda i: (i, 0)),
        out_shape=jax.ShapeDtypeStruct((n, LANES), F32),
        compiler_params=_cparams(("arbitrary",)),
        name="router",
    )(x2, wh, wl, bias)


MOE_EXPERTS_PER_STEP = 4


def _moe_kernel(x_ref, cmb_ref, wg_ref, wu_ref, wd_ref, g_ref, b_ref, o_ref, xb_scr, acc_scr):
    e = pl.program_id(1)

    @pl.when(e == 0)
    def _():
        xb_scr[...] = x_ref[...].astype(BF16)
        acc_scr[...] = jnp.zeros(acc_scr.shape, F32)

    cmb = cmb_ref[...]
    lane = lax.broadcasted_iota(I32, cmb.shape, 1)
    xb = xb_scr[...]
    part = None
    for u in range(MOE_EXPERTS_PER_STEP):
        ce = jnp.sum(jnp.where(lane == e * MOE_EXPERTS_PER_STEP + u, cmb, 0.0), axis=1, keepdims=True)
        hid = _silu(_dot(xb, wg_ref[u])) * _dot(xb, wu_ref[u]) * ce
        down = _dot(hid.astype(BF16), wd_ref[u])
        part = down if part is None else part + down
    acc_scr[...] += part

    @pl.when(e == pl.num_programs(1) - 1)
    def _():
        o_ref[...] = _layer_norm_rows(ALPHA * x_ref[...] + acc_scr[...], g_ref[...], b_ref[...])


def _moe_ln(x2, cmb, wg, wu, wd, gain, bias):
    n = x2.shape[0]
    tm = 1024 if n > 1024 else 512
    full2 = lambda a_: pl.BlockSpec(a_.shape, lambda i, e: (0, 0))
    up_blk = pl.BlockSpec((MOE_EXPERTS_PER_STEP, D_MODEL, EXPERT_FF), lambda i, e: (e, 0, 0))
    return pl.pallas_call(
        _moe_kernel,
        grid=(n // tm, N_EXPERTS // MOE_EXPERTS_PER_STEP),
        in_specs=[pl.BlockSpec((tm, D_MODEL), lambda i, e: (i, 0)),
                  pl.BlockSpec((tm, LANES), lambda i, e: (i, 0)),
                  up_blk, up_blk,
                  pl.BlockSpec((MOE_EXPERTS_PER_STEP, EXPERT_FF, D_MODEL), lambda i, e: (e, 0, 0)),
                  full2(gain), full2(bias)],
        out_specs=pl.BlockSpec((tm, D_MODEL), lambda i, e: (i, 0)),
        out_shape=jax.ShapeDtypeStruct((n, D_MODEL), F32),
        scratch_shapes=[pltpu.VMEM((tm, D_MODEL), BF16), pltpu.VMEM((tm, D_MODEL), F32)],
        compiler_params=_cparams(("arbitrary", "arbitrary"), 48),
        name="moe_ln2",
    )(x2, cmb, wg, wu, wd, gain, bias)


def _layer_weights(l, w_in, w_out, gv_gain, gv_bias, ws, bs, ret_gain, ret_bias, ln1_g, ln1_b, ln2_g, ln2_b,
                   rg_w, rg_b, re_w, re_b, e_gate, e_up, e_down):
    w = w_in[l]
    z = lambda k: jnp.zeros((D_MODEL, k), w.dtype)
    ik_end = 2560 + IDX_DIM
    iw_end = ik_end + IDX_HEADS
    w_pad = jnp.concatenate([w[:, :ik_end], z(COL_IW - COL_IK - IDX_DIM), w[:, ik_end:iw_end],
                             z(COL_CQ - COL_IW - IDX_HEADS), w[:, iw_end:]], axis=1).astype(BF16)
    wo = w_out[l].astype(BF16)
    wr = jnp.concatenate([re_w[l], rg_w[l], jnp.zeros((D_MODEL, LANES - N_EXPERTS - N_GROUPS), F32)], axis=1)
    wrh = wr.astype(BF16)
    wrl = (wr - wrh.astype(F32)).astype(BF16)
    br = jnp.concatenate([re_b[l], rg_b[l], jnp.zeros((LANES - N_EXPERTS - N_GROUPS,), F32)])[None, :]
    row = lambda a: a[l].reshape(1, -1)
    return dict(
        w_pad=w_pad, wa=wo[:A_WIDTH], wb=wo[A_WIDTH:A_WIDTH + B_WIDTH], wc=wo[A_WIDTH + B_WIDTH:],
        gv_gain=row(gv_gain), gv_bias=row(gv_bias), ws=ws[l], bs=bs[l],
        ret_gain=row(ret_gain), ret_bias=row(ret_bias),
        ln1_g=row(ln1_g), ln1_b=row(ln1_b), ln2_g=row(ln2_g), ln2_b=row(ln2_b),
        wrh=wrh, wrl=wrl, br=br,
        wg=e_gate[l].astype(BF16), wu=e_up[l].astype(BF16), wd=e_down[l].astype(BF16))


def _channel_mix(x2, a_out, b_out, c_out, lw):
    x1 = _outproj_ln(a_out, b_out, c_out, lw["wa"], lw["wb"], lw["wc"], x2, lw["ln1_g"], lw["ln1_b"])
    cmb = _router(x1, lw["wrh"], lw["wrl"], lw["br"])
    return _moe_ln(x1, cmb, lw["wg"], lw["wu"], lw["wd"], lw["ln2_g"], lw["ln2_b"])


def _heads(t2, b, l, nh):
    return t2.reshape(b, l, nh, t2.shape[-1] // nh)


def _prompt_layer(x3, lw, tt, cosf, sins):
    b, l, _ = x3.shape
    x2 = x3.reshape(b * l, D_MODEL)
    ha, kt, vt, vtb, ikt, kh, qt, iqt, ikb, iwt = _proj_prompt(x3, lw["w_pad"])
    bsb = jnp.repeat(lw["bs"].T, HEAD_DIM, axis=1)
    a_out, a_vn = _gmlp(ha, PCOL_AU, PCOL_AV, lw["gv_gain"], lw["gv_bias"], lw["ws"], bsb)
    topk = min(TOPK_MAX, l // 4)
    b_out = _dsa_prompt(iqt, iwt, ikb, qt, kh, vtb.reshape(b, B_HEADS, HEAD_DIM, l), tt, topk)
    s0e = jnp.zeros((b, C_WIDTH, C_WIDTH), F32)
    c_out, s_fin = _retention(ha.reshape(b, l, P_WIDTH), (PCOL_CQ, PCOL_CK, PCOL_CV, PCOL_CG), cosf, sins,
                              lw["ret_gain"], lw["ret_bias"], s0e, RET_CHUNK, 1)
    y = _channel_mix(x2, a_out, b_out.reshape(b * l, B_WIDTH), c_out.reshape(b * l, C_WIDTH), lw)
    last = ((l - 1) // CHUNK) * CHUNK
    to_rows = lambda t: jnp.transpose(t.reshape(b, B_HEADS, HEAD_DIM, l), (0, 3, 1, 2))
    state = (to_rows(kt), to_rows(vt), jnp.transpose(ikt, (0, 2, 1)), _state_extract(s_fin),
             a_vn.reshape(b, l, A_WIDTH)[:, last:])
    return y.reshape(b, l, D_MODEL), state


def _sample_layer(x3, lw, layer, cache_k, cache_v, cache_ik, state_l, pt_flat, npages, bfar, bnear, cosf, sins):
    bd, t, _ = x3.shape
    n = bd * t
    past = npages * PAGE_SIZE
    x2 = x3.reshape(n, D_MODEL)
    h2 = _proj(x2, lw["w_pad"])
    col = lambda c0, wdt: h2[:, c0:c0 + wdt]
    rep = CHUNK // t
    eye = jnp.eye(rep, dtype=F32)
    ws_t = lw["ws"][:, :t, :t]
    ws_bd = (eye[None, :, None, :, None] * ws_t[:, None, :, None, :]).reshape(A_GROUPS, CHUNK, CHUNK)
    bsb = jnp.repeat(jnp.tile(lw["bs"][:, :t], (1, rep)).T, HEAD_DIM, axis=1)
    a_out, a_vn = _gmlp(h2, COL_AU, COL_AV, lw["gv_gain"], lw["gv_bias"], ws_bd, bsb)
    k4 = _heads(col(COL_K, B_WIDTH), bd, t, B_HEADS)
    v4 = _heads(col(COL_V, B_WIDTH), bd, t, B_HEADS)
    ik3 = col(COL_IK, IDX_DIM).reshape(bd, t, IDX_DIM)
    feat_major = lambda a3: jnp.pad(jnp.transpose(a3, (0, 2, 1)), ((0, 0), (0, 0), (0, PAGE_SIZE - t)))
    iq4 = _heads(col(COL_IQ, IDX_HEADS * IDX_DIM), bd, t, IDX_HEADS)
    iqs = jnp.transpose(iq4, (0, 2, 1, 3)).reshape(bd, IDX_HEADS * t, IDX_DIM).astype(BF16)
    iw3 = jnp.transpose(col(COL_IW, IDX_HEADS).reshape(bd, t, IDX_HEADS), (0, 2, 1))
    iws = jnp.broadcast_to(iw3.reshape(bd, IDX_HEADS * t, 1), (bd, IDX_HEADS * t, LANES))
    scores = _sample_scores(pt_flat, iqs, iws, jnp.transpose(cache_ik, (0, 1, 3, 2)), layer, feat_major(ik3), npages)
    width = (npages + 1) * PAGE_SIZE
    topk = min(TOPK_MAX, (past + t) // 4)
    madd = _sample_mask(scores.reshape(n, width), topk, past, t).reshape(bd, t, width)
    q4 = _heads(col(COL_Q, B_WIDTH), bd, t, B_HEADS)
    eye_h = jnp.eye(B_HEADS, dtype=F32)
    qbd = (jnp.transpose(q4, (0, 2, 1, 3))[:, :, :, None, :] * eye_h[None, :, None, :, None])
    qbd = qbd.reshape(bd, B_HEADS * t, B_WIDTH).astype(BF16)
    page_view = lambda c: jnp.transpose(c, (0, 1, 3, 4, 2)).reshape(c.shape[0], c.shape[1], B_WIDTH, PAGE_SIZE)
    b_out = _sample_attend(pt_flat, qbd, madd, bfar, bnear, page_view(cache_k), page_view(cache_v), layer,
                           feat_major(k4.reshape(bd, t, B_WIDTH)), feat_major(v4.reshape(bd, t, B_WIDTH)), npages)
    c_out, s_fin = _retention(h2.reshape(bd, t, H_WIDTH), (COL_CQ, COL_CK, COL_CV, COL_CG), cosf, sins,
                              lw["ret_gain"], lw["ret_bias"], _state_embed(state_l), t, 8)
    y = _channel_mix(x2, a_out, b_out.reshape(n, B_WIDTH), c_out.reshape(n, C_WIDTH), lw)
    state = (k4, v4, ik3, _state_extract(s_fin), a_vn.reshape(bd, t, A_WIDTH))
    return y.reshape(bd, t, D_MODEL), state


def _distance_tables(rel_bias, t, past):
    r = jnp.arange(Q_BLOCK, dtype=I32)
    d_prompt = jnp.concatenate([off * Q_BLOCK + r[None, :] - r[:, None] for off in range(3)], axis=0)
    qpos = past + jnp.arange(t, dtype=I32)
    near0 = past - PAGE_SIZE
    d_near = [qpos[:, None] - (near0 + half * PAGE_SIZE + r[None, :]) for half in range(2)]
    d_far = jnp.full((t, LANES), MAX_DISTANCE * 2, I32)
    n0 = 3 * Q_BLOCK
    tables = _bias_tables(jnp.concatenate([d_prompt] + d_near + [d_far], axis=0), rel_bias, n0)
    tt = tables[:, :n0].reshape(B_HEADS, 3, Q_BLOCK, Q_BLOCK)
    bnear = jnp.concatenate([tables[:, n0:n0 + t], tables[:, n0 + t:n0 + 2 * t]], axis=2).reshape(B_HEADS * t, 2 * LANES)
    bfar = tables[:, n0 + 2 * t:n0 + 3 * t].reshape(B_HEADS * t, LANES)
    return tt, bnear, bfar


def kernel(x_prompt, x_sample, cache_k, cache_v, cache_idx_k, state_ret, page_table, w_in, w_out, gmlp_v_gain,
           gmlp_v_bias, gmlp_ws, gmlp_bs, rel_bias, ret_gn_gain, ret_gn_bias, ln1_gain, ln1_bias, ln2_gain, ln2_bias,
           router_group_w, router_group_b, router_expert_w, router_expert_b, expert_w_gate, expert_w_up,
           expert_w_down):
    depth = w_in.shape[0]
    seq = x_prompt.shape[1]
    bd, t, _ = x_sample.shape
    npages = page_table.shape[1]
    past = npages * PAGE_SIZE
    pt_flat = page_table.reshape(-1).astype(I32)
    tt, bnear, bfar = _distance_tables(rel_bias, t, past)
    cos_p, sin_p = _rope_tables(jnp.arange(seq, dtype=I32))
    cos_s, sin_s = _rope_tables(past + jnp.arange(t, dtype=I32))
    xp, xs = x_prompt, x_sample
    st_p, st_s = [], []
    for l in range(depth):
        lw = _layer_weights(l, w_in, w_out, gmlp_v_gain, gmlp_v_bias, gmlp_ws, gmlp_bs, ret_gn_gain, ret_gn_bias,
                            ln1_gain, ln1_bias, ln2_gain, ln2_bias, router_group_w, router_group_b,
                            router_expert_w, router_expert_b, expert_w_gate, expert_w_up, expert_w_down)
        xp, sp = _prompt_layer(xp, lw, tt, cos_p, sin_p)
        xs, ss = _sample_layer(xs, lw, l, cache_k, cache_v, cache_idx_k, state_ret[l], pt_flat, npages,
                               bfar, bnear, cos_s, sin_s)
        st_p.append(sp)
        st_s.append(ss)
    stk = lambda sts, i: jnp.stack([s[i] for s in sts], axis=0)
    return (xp, xs,
            stk(st_p, 0), stk(st_p, 1), stk(st_p, 2), stk(st_p, 3), stk(st_p, 4),
            stk(st_s, 0), stk(st_s, 1), stk(st_s, 2), stk(st_s, 3), stk(st_s, 4))
```

```python
import functools
import math

import jax
import jax.numpy as jnp
from jax import lax
from jax.experimental import pallas as pl
from jax.experimental.pallas import tpu as pltpu

D_MODEL = 1024
HEAD_DIM = 64
A_GROUPS = 4
A_WIDTH = 256
CHUNK = 128
B_HEADS = 8
B_WIDTH = 512
IDX_HEADS = 8
IDX_DIM = 64
TOPK_MAX = 256
Q_BLOCK = 128
KEY_TILE = 256
PHASE_UNROLL = 2
NUM_BUCKETS = 32
MAX_DISTANCE = 128
C_HEADS = 4
C_WIDTH = 256
RET_CHUNK = 128
ROPE_BASE = 10000.0
PAGE_SIZE = 128
N_GROUPS = 4
EXPERTS_PER_GROUP = 8
N_EXPERTS = 32
EXPERT_FF = 256
DEPTH = 2
ALPHA = (2 * DEPTH) ** 0.25
LN_EPS = 1e-5

F32 = jnp.float32
BF16 = jnp.bfloat16
I32 = jnp.int32
I16 = jnp.int16
LANES = 128
MASK_NEG = -1e30
INT_MIN = -2 ** 31

COL_AU, COL_AV, COL_Q, COL_K, COL_V, COL_IQ, COL_IK, COL_IW = 0, 256, 512, 1024, 1536, 2048, 2560, 2688
COL_CQ, COL_CK, COL_CV, COL_CG = 2816, 3072, 3328, 3584
H_WIDTH = 3840


def _cparams(sem, vmem_mb=None):
    kw = dict(dimension_semantics=sem)
    if vmem_mb is not None:
        kw["vmem_limit_bytes"] = vmem_mb << 20
    return pltpu.CompilerParams(**kw)


def _nt_dot(a, b):
    return lax.dot_general(a, b, (((1,), (1,)), ((), ())), preferred_element_type=F32)


def _dot(a, b):
    return jnp.dot(a, b, preferred_element_type=F32)


def _layer_norm_rows(x, gain, bias):
    mu = jnp.mean(x, axis=-1, keepdims=True)
    xc = x - mu
    var = jnp.mean(xc * xc, axis=-1, keepdims=True)
    return xc * lax.rsqrt(var + LN_EPS) * gain + bias


def _silu(x):
    return x * (1.0 / (1.0 + jnp.exp(-x)))


def _proj_kernel(x_ref, w_ref, o_ref):
    o_ref[...] = _dot(x_ref[...].astype(BF16), w_ref[...])


def _proj(x2, w_pad):
    n = x2.shape[0]
    tm = 1024 if n > 1024 else 512
    tn = 768
    return pl.pallas_call(
        _proj_kernel,
        grid=(n // tm, H_WIDTH // tn),
        in_specs=[pl.BlockSpec((tm, D_MODEL), lambda i, j: (i, 0)),
                  pl.BlockSpec((D_MODEL, tn), lambda i, j: (0, j))],
        out_specs=pl.BlockSpec((tm, tn), lambda i, j: (i, j)),
        out_shape=jax.ShapeDtypeStruct((n, H_WIDTH), F32),
        compiler_params=_cparams(("arbitrary", "arbitrary"), 40),
        name="in_proj",
    )(x2, w_pad)


PCOL_AU, PCOL_AV, PCOL_CQ, PCOL_CK, PCOL_CV, PCOL_CG = 0, 256, 512, 768, 1024, 1280
P_WIDTH = 1536


def _proj_prompt_kernel(x_ref, w_ref, ha_ref, kt_ref, vt_ref, vtb_ref, ikt_ref, kh_ref, qt_ref, iqt_ref, ikb_ref,
                        iwt_ref):
    x = x_ref[0].astype(BF16)
    piece = lambda c0, width: _dot(x, w_ref[:, c0:c0 + width])
    ha_ref[:, :COL_Q] = piece(COL_AU, COL_Q)
    ha_ref[:, COL_Q:] = piece(COL_CQ, H_WIDTH - COL_CQ)
    qt_ref[0] = jnp.transpose(piece(COL_Q, B_WIDTH) * (HEAD_DIM ** -0.5 * LOG2E)).astype(BF16)
    iqt_ref[0] = jnp.transpose(piece(COL_IQ, IDX_HEADS * IDX_DIM)).astype(BF16)
    k = piece(COL_K, B_WIDTH)
    for h in range(B_HEADS):
        kh_ref[0, h] = k[:, h * HEAD_DIM:(h + 1) * HEAD_DIM].astype(BF16)
    kt_ref[0] = jnp.transpose(k)
    vt = jnp.transpose(piece(COL_V, B_WIDTH))
    vt_ref[0] = vt
    vtb_ref[0] = vt.astype(BF16)
    ik = piece(COL_IK, LANES)
    ikb_ref[0] = ik[:, :IDX_DIM].astype(BF16)
    ikt_ref[0] = jnp.transpose(ik)[:IDX_DIM]
    iwt_ref[0] = jnp.transpose(piece(COL_IW, LANES))[:IDX_HEADS]


def _proj_prompt(x3, w_pad):
    b, l, _ = x3.shape
    tm = 512
    nt = l // tm
    f32s = lambda *shape: jax.ShapeDtypeStruct(shape, F32)
    bf16s = lambda *shape: jax.ShapeDtypeStruct(shape, BF16)
    feat = lambda rows: pl.BlockSpec((1, rows, tm), lambda i, j: (i, 0, j))
    hmaj = pl.BlockSpec((1, B_HEADS, tm, HEAD_DIM), lambda i, j: (i, 0, j, 0))
    return pl.pallas_call(
        _proj_prompt_kernel,
        grid=(b, nt),
        in_specs=[pl.BlockSpec((1, tm, D_MODEL), lambda i, j: (i, j, 0)),
                  pl.BlockSpec(w_pad.shape, lambda i, j: (0, 0), pipeline_mode=pl.Buffered(1))],
        out_specs=[pl.BlockSpec((tm, P_WIDTH), lambda i, j: (i * nt + j, 0)),
                   feat(B_WIDTH), feat(B_WIDTH), feat(B_WIDTH), feat(IDX_DIM),
                   hmaj, feat(B_WIDTH), feat(IDX_HEADS * IDX_DIM),
                   pl.BlockSpec((1, tm, IDX_DIM), lambda i, j: (i, j, 0)),
                   feat(IDX_HEADS)],
        out_shape=[f32s(b * l, P_WIDTH), f32s(b, B_WIDTH, l), f32s(b, B_WIDTH, l), bf16s(b, B_WIDTH, l),
                   f32s(b, IDX_DIM, l), bf16s(b, B_HEADS, l, HEAD_DIM), bf16s(b, B_WIDTH, l),
                   bf16s(b, IDX_HEADS * IDX_DIM, l), bf16s(b, l, IDX_DIM), f32s(b, IDX_HEADS, l)],
        compiler_params=_cparams(("arbitrary", "arbitrary"), 48),
        name="in_proj_prompt",
    )(x3, w_pad)


LOG2E = math.log2(math.e)


def _bias_kernel(rb_ref, d_ref, o_ref, *, log2_rows):
    n = jnp.maximum(d_ref[...], 0)
    max_exact = NUM_BUCKETS // 2
    nf = jnp.maximum(n, 1).astype(F32)
    large = max_exact + (jnp.log(nf / max_exact) / math.log(MAX_DISTANCE / max_exact)
                         * (NUM_BUCKETS - max_exact)).astype(I32)
    large = jnp.minimum(large, NUM_BUCKETS - 1)
    bucket = jnp.where(n < max_exact, n, large)
    for h in range(B_HEADS):
        acc = jnp.zeros(bucket.shape, F32)
        for bk in range(NUM_BUCKETS):
            acc = jnp.where(bucket == bk, rb_ref[bk * B_HEADS + h], acc)
        o_ref[h, :log2_rows] = acc[:log2_rows] * LOG2E
        o_ref[h, log2_rows:] = acc[log2_rows:]


def _bias_tables(dist, rel_bias, log2_rows):
    r = dist.shape[0]
    return pl.pallas_call(
        functools.partial(_bias_kernel, log2_rows=log2_rows),
        in_specs=[pl.BlockSpec(memory_space=pltpu.SMEM),
                  pl.BlockSpec((r, LANES), lambda: (0, 0))],
        out_specs=pl.BlockSpec((B_HEADS, r, LANES), lambda: (0, 0, 0)),
        out_shape=jax.ShapeDtypeStruct((B_HEADS, r, LANES), F32),
        name="bias_tables",
    )(rel_bias.reshape(-1), dist)


def _gmlp_kernel(u_ref, v_ref, g_ref, b_ref, ws_ref, bsb_ref, o_ref, vn_ref, *, nchunk):
    r = lax.broadcasted_iota(I32, (CHUNK, CHUNK), 0)
    c = lax.broadcasted_iota(I32, (CHUNK, CHUNK), 1)
    grp = lax.broadcasted_iota(I32, (CHUNK, A_WIDTH), 1) // HEAD_DIM
    wts = [jnp.where(r >= c, ws_ref[g], 0.0).astype(BF16) for g in range(A_GROUPS)]
    gain, bias, bsb = g_ref[...], b_ref[...], bsb_ref[...]
    for ci in range(nchunk):
        sl = pl.ds(ci * CHUNK, CHUNK)
        vn = _layer_norm_rows(v_ref[sl, :], gain, bias)
        vn_ref[sl, :] = vn
        vb = vn.astype(BF16)
        mixed = bsb
        for g in range(A_GROUPS):
            mixed = mixed + jnp.where(grp == g, _dot(wts[g], vb), 0.0)
        o_ref[sl, :] = (u_ref[sl, :] * mixed).astype(BF16)


def _gmlp(h2, col_u, col_v, gain, bias, ws, bsb):
    n = h2.shape[0]
    tm = min(n, 1024)
    blk = lambda col: pl.BlockSpec((tm, A_WIDTH), lambda i: (i, col // A_WIDTH))
    full2 = lambda a: pl.BlockSpec(a.shape, lambda i: (0, 0))
    return pl.pallas_call(
        functools.partial(_gmlp_kernel, nchunk=tm // CHUNK),
        grid=(n // tm,),
        in_specs=[blk(col_u), blk(col_v), full2(gain), full2(bias),
                  pl.BlockSpec(ws.shape, lambda i: (0, 0, 0)), full2(bsb)],
        out_specs=[pl.BlockSpec((tm, A_WIDTH), lambda i: (i, 0)),
                   pl.BlockSpec((tm, A_WIDTH), lambda i: (i, 0))],
        out_shape=[jax.ShapeDtypeStruct((n, A_WIDTH), BF16),
                   jax.ShapeDtypeStruct((n, A_WIDTH), F32)],
        compiler_params=_cparams(("arbitrary",)),
        name="gmlp",
    )(h2, h2, gain, bias, ws, bsb)


def _ret_tables(c):
    log_g = jnp.log(1.0 - 2.0 ** (-5.0 - jnp.arange(C_HEADS, dtype=F32)))
    i = jnp.arange(c, dtype=F32)
    diff = i[:, None] - i[None, :]
    dmat = jnp.where(diff >= 0, jnp.exp(log_g[:, None, None] * jnp.maximum(diff, 0.0)), 0.0)
    q_dec = jnp.exp(log_g[:, None] * (i + 1.0))
    k_dec = jnp.exp(log_g[:, None] * (c - 1.0 - i))
    s_dec = jnp.exp(log_g * c)
    qd = jnp.repeat(q_dec.T, HEAD_DIM, axis=1)
    kd = jnp.repeat(k_dec.T, HEAD_DIM, axis=1)
    hid = jnp.arange(C_WIDTH) // HEAD_DIM
    same = hid[:, None] == hid[None, :]
    sd = jnp.where(same, s_dec[hid][:, None], 0.0)
    return dmat, qd, kd, sd, same.astype(F32)


def _rope_tables(pos):
    half = HEAD_DIM // 2
    inv = ROPE_BASE ** (-jnp.arange(half, dtype=F32) / half)
    ang = pos.astype(F32)[:, None] * inv[None, :]
    cos, sin = jnp.cos(ang), jnp.sin(ang)
    cosf = jnp.tile(jnp.concatenate([cos, cos], axis=1), (1, C_HEADS))
    sins = jnp.tile(jnp.concatenate([-sin, sin], axis=1), (1, C_HEADS))
    return cosf, sins


def _ret_kernel(q_ref, k_ref, v_ref, g_ref, cos_ref, sin_ref, qd_ref, kd_ref, dm_ref, sd_ref, bd_ref,
                gg_ref, gb_ref, s0_ref, o_ref, sf_ref, s_scr, *, bt, c):
    ci = pl.program_id(1)

    @pl.when(ci == 0)
    def _():
        s_scr[...] = s0_ref[...]

    lane = lax.broadcasted_iota(I32, (c, C_WIDTH), 1)
    hid = lane // HEAD_DIM
    first = (lane % HEAD_DIM) < (HEAD_DIM // 2)
    cosf, sins = cos_ref[...], sin_ref[...]
    half = HEAD_DIM // 2

    def rot(x):
        partner = jnp.where(first, pltpu.roll(x, C_WIDTH - half, 1), pltpu.roll(x, half, 1))
        return x * cosf + partner * sins

    def seg_mean(x):
        out = jnp.zeros_like(x)
        for h in range(C_HEADS):
            hm = hid == h
            s = jnp.sum(jnp.where(hm, x, 0.0), axis=1, keepdims=True) * (1.0 / HEAD_DIM)
            out = jnp.where(hm, s, out)
        return out

    for bb in range(bt):
        q = rot(q_ref[bb])
        k = rot(k_ref[bb]) * (HEAD_DIM ** -0.5)
        v = v_ref[bb]
        kb, vb = k.astype(BF16), v.astype(BF16)
        s_old = s_scr[bb]
        o = _dot(q.astype(BF16), s_old.astype(BF16)) * qd_ref[...]
        for h in range(C_HEADS):
            hm = hid == h
            att = _nt_dot(jnp.where(hm, q, 0.0).astype(BF16), kb) * dm_ref[h]
            o = o + jnp.where(hm, _dot(att.astype(BF16), vb), 0.0)
        kdt = jnp.transpose(k * kd_ref[...]).astype(BF16)
        s_scr[bb] = s_old * sd_ref[...] + bd_ref[...] * _dot(kdt, vb)
        mu = seg_mean(o)
        oc = o - mu
        var = seg_mean(oc * oc)
        normed = oc * lax.rsqrt(var + LN_EPS) * gg_ref[...] + gb_ref[...]
        o_ref[bb] = (_silu(g_ref[bb]) * normed).astype(BF16)

    @pl.when(ci == pl.num_programs(1) - 1)
    def _():
        sf_ref[...] = s_scr[...]


def _retention(h3, cols, cosf, sins, gn_gain, gn_bias, s0e, c, bt):
    b, l, _ = h3.shape
    dmat, qd, kd, sd, bd = _ret_tables(c)
    blk = lambda col: pl.BlockSpec((bt, c, C_WIDTH), lambda i, j: (i, j, col // C_WIDTH))
    const2 = lambda a: pl.BlockSpec(a.shape, lambda i, j: (0, 0))
    pos_blk = pl.BlockSpec((c, C_WIDTH), lambda i, j: (j, 0))
    st_blk = pl.BlockSpec((bt, C_WIDTH, C_WIDTH), lambda i, j: (i, 0, 0))
    return pl.pallas_call(
        functools.partial(_ret_kernel, bt=bt, c=c),
        grid=(b // bt, l // c),
        in_specs=[blk(cols[0]), blk(cols[1]), blk(cols[2]), blk(cols[3]), pos_blk, pos_blk,
                  const2(qd), const2(kd), pl.BlockSpec(dmat.shape, lambda i, j: (0, 0, 0)),
                  const2(sd), const2(bd), const2(gn_gain), const2(gn_bias), st_blk],
        out_specs=[pl.BlockSpec((bt, c, C_WIDTH), lambda i, j: (i, j, 0)), st_blk],
        out_shape=[jax.ShapeDtypeStruct((b, l, C_WIDTH), BF16),
                   jax.ShapeDtypeStruct((b, C_WIDTH, C_WIDTH), F32)],
        scratch_shapes=[pltpu.VMEM((bt, C_WIDTH, C_WIDTH), F32)],
        compiler_params=_cparams(("arbitrary", "arbitrary")),
        name="retention",
    )(h3, h3, h3, h3, cosf, sins, qd, kd, dmat, sd, bd, gn_gain, gn_bias, s0e)


def _state_embed(s):
    b = s.shape[0]
    eye = jnp.eye(C_HEADS, dtype=s.dtype)
    return (s[:, :, :, None, :] * eye[None, :, None, :, None]).reshape(b, C_WIDTH, C_WIDTH)


def _state_extract(se):
    b = se.shape[0]
    s5 = se.reshape(b, C_HEADS, HEAD_DIM, C_HEADS, HEAD_DIM)
    return jnp.stack([s5[:, h, :, h, :] for h in range(C_HEADS)], axis=1)


def _sortable(score):
    bits = pltpu.bitcast(score, I32)
    return bits ^ ((bits >> 31) & 0x7FFFFFFF)


def _dsa_kernel(iq_ref, iw_ref, ik_ref, q_ref, k_ref, vt_ref, tt_ref, o_ref,
                sc_scr, hi_scr, lo_scr, s_scr, acc_scr, j_scr, *, topk):
    qb = pl.program_id(1)
    ntile = (qb + KEY_TILE // Q_BLOCK) // (KEY_TILE // Q_BLOCK)
    row = lax.broadcasted_iota(I32, (KEY_TILE, Q_BLOCK), 0)
    col = lax.broadcasted_iota(I32, (KEY_TILE, Q_BLOCK), 1)
    qpos = qb * Q_BLOCK + col
    head_rows = lambda ref, h: ref[0, h * HEAD_DIM:(h + 1) * HEAD_DIM, :]
    iq_all = jnp.concatenate([head_rows(iq_ref, h) for h in range(IDX_HEADS)], axis=1)
    iw = iw_ref[0]
    int_min = jnp.int32(INT_MIN)
    score_scale = IDX_HEADS ** -0.5

    def tile_rows(j):
        return pl.ds(pl.multiple_of(j * KEY_TILE, KEY_TILE), KEY_TILE)

    def tree_sum(parts):
        while len(parts) > 1:
            parts = [a + b for a, b in zip(parts[::2], parts[1::2])]
        return parts[0]

    def score_tile(j, carry):
        s = _dot(ik_ref[0, tile_rows(j), :], iq_all)
        terms = [jnp.maximum(s[:, h * Q_BLOCK:(h + 1) * Q_BLOCK], 0.0) * iw[h:h + 1, :] for h in range(IDX_HEADS)]
        acc = terms[0]
        for term in terms[1:]:
            acc = acc + term
        score = (acc * (IDX_DIM ** -0.5)) * score_scale
        kint = jnp.where(j * KEY_TILE + row <= qpos, _sortable(score), int_min)
        sc_scr[tile_rows(j), :] = kint
        hi_scr[tile_rows(j), :] = (kint >> 16).astype(I16)
        return carry

    nstep = (ntile + PHASE_UNROLL - 1) // PHASE_UNROLL

    def unrolled(tile_fn):
        def step(js, carry):
            for u in range(PHASE_UNROLL):
                carry = tile_fn(js * PHASE_UNROLL + u, carry)
            return carry
        return step

    lax.fori_loop(0, nstep, unrolled(score_tile), 0)

    def count(pred):
        def body(j, acc):
            m = jnp.where(pred(sc_scr[tile_rows(j), :], j), 1.0, 0.0)
            return acc + tree_sum([m[i * 8:(i + 1) * 8] for i in range(KEY_TILE // 8)])
        acc = lax.fori_loop(0, ntile, body, jnp.zeros((8, Q_BLOCK), F32))
        return jnp.sum(acc, axis=0, keepdims=True)

    half_min = -(1 << 15)
    ones16, zeros16 = jnp.ones((16, Q_BLOCK), I16), jnp.zeros((16, Q_BLOCK), I16)

    def count16(ref, cand, strict=False):
        cand16 = jnp.broadcast_to(cand, (16, Q_BLOCK)).astype(I16)

        def tile(j, acc):
            x = ref[tile_rows(j), :]
            hits = []
            for i in range(KEY_TILE // 16):
                xi = x[i * 16:(i + 1) * 16]
                hits.append(jnp.where((xi > cand16) if strict else (xi >= cand16), ones16, zeros16))
            return acc + tree_sum(hits)

        acc = lax.fori_loop(0, nstep, unrolled(tile), zeros16)
        return jnp.sum(acc.astype(F32), axis=0, keepdims=True)

    def high_bit(i, ans):
        cand = ans | lax.shift_left(jnp.int32(1), 15 - i)
        return jnp.where(count16(hi_scr, cand + half_min) >= topk, cand, ans)

    t_hi = lax.fori_loop(0, 16, high_bit, jnp.zeros((1, Q_BLOCK), I32)) + half_min
    n_above = count16(hi_scr, t_hi, strict=True)

    def low_halves(j, carry):
        t = sc_scr[tile_rows(j), :]
        lo_scr[tile_rows(j), :] = jnp.where((t >> 16) == t_hi, (t & 0xFFFF) + half_min, half_min).astype(I16)
        return carry

    lax.fori_loop(0, nstep, unrolled(low_halves), 0)

    def low_bit(i, ans):
        cand = ans | lax.shift_left(jnp.int32(1), 15 - i)
        return jnp.where(n_above + count16(lo_scr, cand + half_min) >= topk, cand, ans)

    t_lo = lax.fori_loop(0, 16, low_bit, jnp.zeros((1, Q_BLOCK), I32))
    thr = lax.shift_left(t_hi, 16) | t_lo
    need = topk - count(lambda t, j: t > thr)
    n_ge = count(lambda t, j: t >= thr)

    j_scr[...] = jnp.full(j_scr.shape, 1 << 20, I32)
    has_tie = jnp.max(jnp.where((n_ge > topk) & (thr != int_min), 1.0, 0.0)) > 0.0

    @pl.when(has_tie)
    def _():
        def jbit(i, jc):
            cand = jc | lax.shift_left(jnp.int32(1), 12 - i)
            cnt = count(lambda t, j: (t == thr) & (j * KEY_TILE + row < cand))
            return jnp.where(cnt <= need, cand, jc)
        jc = lax.fori_loop(0, 13, jbit, jnp.zeros((1, Q_BLOCK), I32))
        j_scr[...] = jnp.broadcast_to(jc, j_scr.shape)

    j_cut = jnp.where(thr == int_min, 0, j_scr[0:1, :])

    q_scaled = [head_rows(q_ref, h) for h in range(B_HEADS)]
    sub_tiles = KEY_TILE // Q_BLOCK
    groups = KEY_TILE // 8

    def tree_max(parts):
        while len(parts) > 1:
            parts = [jnp.maximum(a, b) for a, b in zip(parts[::2], parts[1::2])]
        return parts[0]

    def logits_tile(j, m_part):
        t = sc_scr[tile_rows(j), :]
        kidx = j * KEY_TILE + row
        madd = jnp.where(t > thr, 0.0, jnp.where(t == thr, jnp.where(kidx < j_cut, 0.0, MASK_NEG), MASK_NEG))
        offs = [jnp.clip(qb - (j * sub_tiles + i), 0, 2) for i in range(sub_tiles)]
        new = []
        for h in range(B_HEADS):
            bias = jnp.concatenate([tt_ref[h, off] for off in offs], axis=0)
            s = _dot(k_ref[0, h, tile_rows(j), :], q_scaled[h]) + bias + madd
            s_scr[h, tile_rows(j), :] = s
            new.append(jnp.maximum(m_part[h], tree_max([s[i * 8:(i + 1) * 8] for i in range(groups)])))
        return jnp.stack(new)

    m_part = lax.fori_loop(0, nstep, unrolled(logits_tile), jnp.full((B_HEADS, 8, Q_BLOCK), MASK_NEG, F32))
    m_rows = [jnp.max(m_part[h], axis=0, keepdims=True) for h in range(B_HEADS)]

    acc_scr[...] = jnp.zeros(acc_scr.shape, F32)

    def values_tile(j, l_part):
        new = []
        for h in range(B_HEADS):
            p = jnp.exp2(s_scr[h, tile_rows(j), :] - m_rows[h])
            new.append(l_part[h] + tree_sum([p[i * 8:(i + 1) * 8] for i in range(groups)]))
            acc_scr[h] += _dot(vt_ref[0, h, :, tile_rows(j)], p.astype(BF16))
        return jnp.stack(new)

    l_part = lax.fori_loop(0, nstep, unrolled(values_tile), jnp.zeros((B_HEADS, 8, Q_BLOCK), F32))
    out_t = jnp.concatenate([acc_scr[h] / jnp.sum(l_part[h], axis=0, keepdims=True) for h in range(B_HEADS)], axis=0)
    o_ref[0] = jnp.transpose(out_t).astype(BF16)


def _dsa_prompt(iqt, iwt, ik3, qt, kh, vt, tt, topk):
    b, _, l, _ = kh.shape
    assert l % (KEY_TILE * PHASE_UNROLL) == 0
    qblk = pl.BlockSpec((1, B_WIDTH, Q_BLOCK), lambda i, j: (i, 0, j))
    once = pl.Buffered(1)
    return pl.pallas_call(
        functools.partial(_dsa_kernel, topk=topk),
        grid=(b, l // Q_BLOCK),
        in_specs=[qblk,
                  pl.BlockSpec((1, IDX_HEADS, Q_BLOCK), lambda i, j: (i, 0, j)),
                  pl.BlockSpec((1, l, IDX_DIM), lambda i, j: (i, 0, 0), pipeline_mode=once),
                  qblk,
                  pl.BlockSpec((1, B_HEADS, l, HEAD_DIM), lambda i, j: (i, 0, 0, 0), pipeline_mode=once),
                  pl.BlockSpec((1, B_HEADS, HEAD_DIM, l), lambda i, j: (i, 0, 0, 0), pipeline_mode=once),
                  pl.BlockSpec(tt.shape, lambda i, j: (0, 0, 0, 0), pipeline_mode=once)],
        out_specs=pl.BlockSpec((1, Q_BLOCK, B_WIDTH), lambda i, j: (i, j, 0)),
        out_shape=jax.ShapeDtypeStruct((b, l, B_WIDTH), BF16),
        scratch_shapes=[pltpu.VMEM((l, Q_BLOCK), I32),
                        pltpu.VMEM((l, Q_BLOCK), I16),
                        pltpu.VMEM((l, Q_BLOCK), I16),
                        pltpu.VMEM((B_HEADS, l, Q_BLOCK), F32),
                        pltpu.VMEM((B_HEADS, HEAD_DIM, Q_BLOCK), F32),
                        pltpu.VMEM((8, Q_BLOCK), I32)],
        compiler_params=_cparams(("arbitrary", "arbitrary"), 56),
        name="dsa_prompt",
    )(iqt, iwt, ik3, qt, kh, vt, tt)


SCORE_BATCH = 2


def _sidx_kernel(pt_ref, iq_ref, iw_ref, *rest, npages):
    pages, ikn_ref, o_ref = rest[:SCORE_BATCH * npages], rest[SCORE_BATCH * npages], rest[SCORE_BATCH * npages + 1]
    nq = iq_ref.shape[1] // IDX_HEADS
    for g in range(SCORE_BATCH):
        iq, iw = iq_ref[g], iw_ref[g]
        for j in range(npages + 1):
            keys_t = (pages[g * npages + j][0, 0] if j < npages else ikn_ref[g]).astype(BF16)
            r = jnp.maximum(_dot(iq, keys_t), 0.0) * iw
            acc = r[0:nq]
            for h in range(1, IDX_HEADS):
                acc = acc + r[h * nq:(h + 1) * nq]
            o_ref[g, :, j * PAGE_SIZE:(j + 1) * PAGE_SIZE] = (acc * (IDX_DIM ** -0.5)) * (IDX_HEADS ** -0.5)


def _sample_scores(pt_flat, iqs, iws, cache_ik, layer, ikn, npages):
    bd, hq, _ = iqs.shape
    nq = hq // IDX_HEADS
    width = (npages + 1) * PAGE_SIZE
    assert bd % SCORE_BATCH == 0
    page_spec = lambda gp: pl.BlockSpec((1, 1, IDX_DIM, PAGE_SIZE),
                                        lambda i, pt, gp=gp: (layer, pt[i * SCORE_BATCH * npages + gp], 0, 0))
    per_b = lambda a: pl.BlockSpec((SCORE_BATCH,) + a.shape[1:], lambda i, pt: (i, 0, 0))
    n_ops = SCORE_BATCH * npages
    return pl.pallas_call(
        functools.partial(_sidx_kernel, npages=npages),
        grid_spec=pltpu.PrefetchScalarGridSpec(
            num_scalar_prefetch=1, grid=(bd // SCORE_BATCH,),
            in_specs=[per_b(iqs), per_b(iws)] + [page_spec(gp) for gp in range(n_ops)] + [per_b(ikn)],
            out_specs=pl.BlockSpec((SCORE_BATCH, nq, width), lambda i, pt: (i, 0, 0))),
        out_shape=jax.ShapeDtypeStruct((bd, nq, width), F32),
        compiler_params=_cparams(("arbitrary",)),
        name="sample_scores",
    )(pt_flat, iqs, iws, *([cache_ik] * n_ops), ikn)


def _sthr_kernel(s_ref, o_ref, k_scr, j_scr, *, topk, past, nq):
    tr, width = s_ref.shape
    colw = lax.broadcasted_iota(I32, (tr, width), 1)
    rowq = lax.broadcasted_iota(I32, (tr, width), 0) % nq
    vis = colw <= past + rowq
    int_min = jnp.int32(INT_MIN)
    k_scr[...] = jnp.where(vis, _sortable(s_ref[...]), int_min)

    def count(pred):
        return jnp.sum(jnp.where(pred(k_scr[...]), 1.0, 0.0), axis=1, keepdims=True)

    def bit_body(i, ans_u):
        cand_u = ans_u | lax.shift_left(jnp.int32(1), 31 - i)
        cand_s = cand_u ^ int_min
        return jnp.where(count(lambda t: t >= cand_s) >= topk, cand_u, ans_u)

    thr = lax.fori_loop(0, 32, bit_body, jnp.zeros((tr, 1), I32)) ^ int_min
    need = topk - count(lambda t: t > thr)
    n_ge = count(lambda t: t >= thr)

    j_scr[...] = jnp.full(j_scr.shape, 1 << 20, I32)
    has_tie = jnp.max(jnp.where((n_ge > topk) & (thr != int_min), 1.0, 0.0)) > 0.0

    @pl.when(has_tie)
    def _():
        def jbit(i, jc):
            cand = jc | lax.shift_left(jnp.int32(1), 12 - i)
            cnt = count(lambda t: (t == thr) & (colw < cand))
            return jnp.where(cnt <= need, cand, jc)
        jc = lax.fori_loop(0, 13, jbit, jnp.zeros((tr, 1), I32))
        j_scr[...] = jnp.broadcast_to(jc, j_scr.shape)

    j_cut = j_scr[:, 0:1]
    t = k_scr[...]
    sel = ((t > thr) | ((t == thr) & (colw < j_cut))) & vis
    o_ref[...] = jnp.where(sel, 0.0, MASK_NEG)


def _sample_mask(scores2, topk, past, nq):
    r, width = scores2.shape
    tr = min(r, 128)
    return pl.pallas_call(
        functools.partial(_sthr_kernel, topk=topk, past=past, nq=nq),
        grid=(r // tr,),
        in_specs=[pl.BlockSpec((tr, width), lambda i: (i, 0))],
        out_specs=pl.BlockSpec((tr, width), lambda i: (i, 0)),
        out_shape=jax.ShapeDtypeStruct((r, width), F32),
        scratch_shapes=[pltpu.VMEM((tr, width), I32), pltpu.VMEM((tr, LANES), I32)],
        compiler_params=_cparams(("arbitrary",)),
        name="sample_mask",
    )(scores2)


def _sattn_kernel(pt_ref, q_ref, m_ref, bfar_ref, bnear_ref, *rest, npages, nq):
    kpages, vpages = rest[:npages], rest[npages:2 * npages]
    kn_ref, vn_ref, o_ref, s_scr = rest[2 * npages:2 * npages + 4]
    q = q_ref[0]
    mx = jnp.full((B_HEADS * nq, PAGE_SIZE), MASK_NEG, F32)
    for j in range(npages + 1):
        kp = (kpages[j][0, 0] if j < npages else kn_ref[0]).astype(BF16)
        cols = slice(j * PAGE_SIZE, (j + 1) * PAGE_SIZE)
        bias = bfar_ref[...] if j < npages - 1 else bnear_ref[:, (j - npages + 1) * PAGE_SIZE:(j - npages + 2) * PAGE_SIZE]
        madd = jnp.concatenate([m_ref[0, :, cols]] * B_HEADS, axis=0)
        s = _dot(q, kp) * (HEAD_DIM ** -0.5) + bias + madd
        s_scr[:, cols] = s
        mx = jnp.maximum(mx, s)
    mrow = jnp.max(mx, axis=1, keepdims=True)
    lsum = jnp.zeros((B_HEADS * nq, PAGE_SIZE), F32)
    acc = jnp.zeros((B_HEADS * nq, B_WIDTH), F32)
    for j in range(npages + 1):
        vp = (vpages[j][0, 0] if j < npages else vn_ref[0]).astype(BF16)
        p = jnp.exp(s_scr[:, j * PAGE_SIZE:(j + 1) * PAGE_SIZE] - mrow)
        lsum = lsum + p
        acc = acc + _nt_dot(p.astype(BF16), vp)
    acc = acc / jnp.sum(lsum, axis=1, keepdims=True)
    head = lax.broadcasted_iota(I32, (nq, B_WIDTH), 1) // HEAD_DIM
    out = jnp.zeros((nq, B_WIDTH), F32)
    for h in range(B_HEADS):
        out = jnp.where(head == h, acc[h * nq:(h + 1) * nq], out)
    o_ref[0] = out.astype(BF16)


def _sample_attend(pt_flat, qbd, madd3, bfar, bnear, cache_k, cache_v, layer, kn, vn, npages):
    bd, hq, _ = qbd.shape
    nq = hq // B_HEADS
    width = (npages + 1) * PAGE_SIZE
    page_spec = lambda p: pl.BlockSpec((1, 1, B_WIDTH, PAGE_SIZE),
                                       lambda i, pt, p=p: (layer, pt[i * npages + p], 0, 0))
    per_b = lambda a: pl.BlockSpec((1,) + a.shape[1:], lambda i, pt: (i, 0, 0))
    const2 = lambda a: pl.BlockSpec(a.shape, lambda i, pt: (0, 0))
    pages = [page_spec(p) for p in range(npages)]
    return pl.pallas_call(
        functools.partial(_sattn_kernel, npages=npages, nq=nq),
        grid_spec=pltpu.PrefetchScalarGridSpec(
            num_scalar_prefetch=1, grid=(bd,),
            in_specs=[per_b(qbd), per_b(madd3), const2(bfar), const2(bnear)] + pages + pages + [per_b(kn), per_b(vn)],
            out_specs=pl.BlockSpec((1, nq, B_WIDTH), lambda i, pt: (i, 0, 0)),
            scratch_shapes=[pltpu.VMEM((hq, width), F32)]),
        out_shape=jax.ShapeDtypeStruct((bd, nq, B_WIDTH), BF16),
        compiler_params=_cparams(("arbitrary",), 56),
        name="sample_attend",
    )(pt_flat, qbd, madd3, bfar, bnear, *([cache_k] * npages), *([cache_v] * npages), kn, vn)


def _outproj_kernel(a_ref, b_ref, c_ref, wa_ref, wb_ref, wc_ref, x_ref, g_ref, bb_ref, o_ref):
    mix = _dot(a_ref[...], wa_ref[...]) + _dot(b_ref[...], wb_ref[...]) + _dot(c_ref[...], wc_ref[...])
    o_ref[...] = _layer_norm_rows(ALPHA * x_ref[...] + mix, g_ref[...], bb_ref[...])


def _outproj_ln(a, b, c, wa, wb, wc, x2, gain, bias):
    n = x2.shape[0]
    tm = 512
    rows = lambda a_: pl.BlockSpec((tm, a_.shape[1]), lambda i: (i, 0))
    full2 = lambda a_: pl.BlockSpec(a_.shape, lambda i: (0, 0))
    return pl.pallas_call(
        _outproj_kernel,
        grid=(n // tm,),
        in_specs=[rows(a), rows(b), rows(c), full2(wa), full2(wb), full2(wc), rows(x2), full2(gain), full2(bias)],
        out_specs=pl.BlockSpec((tm, D_MODEL), lambda i: (i, 0)),
        out_shape=jax.ShapeDtypeStruct((n, D_MODEL), F32),
        compiler_params=_cparams(("arbitrary",), 40),
        name="outproj_ln1",
    )(a, b, c, wa, wb, wc, x2, gain, bias)


def _router_kernel(x_ref, wh_ref, wl_ref, b_ref, o_ref):
    x = x_ref[...]
    xh = x.astype(BF16)
    xl = (x - xh.astype(F32)).astype(BF16)
    logits = _dot(xh, wh_ref[...]) + _dot(xh, wl_ref[...]) + _dot(xl, wh_ref[...]) + b_ref[...]
    lane = lax.broadcasted_iota(I32, logits.shape, 1).astype(F32)
    neg = -jnp.inf
    first_lane = lambda m: jnp.min(jnp.where(m, lane, float(LANES)), axis=1, keepdims=True)
    is_g = (lane >= N_EXPERTS) & (lane < N_EXPERTS + N_GROUPS)
    gmax = jnp.max(jnp.where(is_g, logits, neg), axis=1, keepdims=True)
    g_sel = first_lane(is_g & (logits == gmax)) - N_EXPERTS
    p_group = 1.0 / jnp.sum(jnp.where(is_g, jnp.exp(logits - gmax), 0.0), axis=1, keepdims=True)
    in_g = jnp.floor(lane * (1.0 / EXPERTS_PER_GROUP)) == g_sel
    ev = jnp.where(in_g, logits, neg)
    v1 = jnp.max(ev, axis=1, keepdims=True)
    i1 = first_lane(in_g & (logits == v1))
    ev2 = jnp.where(lane == i1, neg, ev)
    v2 = jnp.max(ev2, axis=1, keepdims=True)
    i2 = first_lane(in_g & (lane != i1) & (logits == v2))
    e2 = jnp.exp(v2 - v1)
    den = 1.0 + e2
    g1 = (1.0 / den) * p_group
    g2 = (e2 / den) * p_group
    o_ref[...] = jnp.where(lane == i1, g1, jnp.where(lane == i2, g2, 0.0))


def _router(x2, wh, wl, bias):
    n = x2.shape[0]
    tm = 512
    full2 = lambda a_: pl.BlockSpec(a_.shape, lambda i: (0, 0))
    return pl.pallas_call(
        _router_kernel,
        grid=(n // tm,),
        in_specs=[pl.BlockSpec((tm, D_MODEL), lambda i: (i, 0)), full2(wh), full2(wl), full2(bias)],
        out_specs=pl.BlockSpec((tm, LANES), lambda i: (i, 0)),
        out_shape=jax.ShapeDtypeStruct((n, LANES), F32),
        compiler_params=_cparams(("arbitrary",)),
        name="router",
    )(x2, wh, wl, bias)


MOE_EXPERTS_PER_STEP = 4


def _moe_kernel(x_ref, cmb_ref, wg_ref, wu_ref, wd_ref, g_ref, b_ref, o_ref, xb_scr, acc_scr):
    e = pl.program_id(1)

    @pl.when(e == 0)
    def _():
        xb_scr[...] = x_ref[...].astype(BF16)
        acc_scr[...] = jnp.zeros(acc_scr.shape, F32)

    cmb = cmb_ref[...]
    lane = lax.broadcasted_iota(I32, cmb.shape, 1)
    xb = xb_scr[...]
    part = None
    for u in range(MOE_EXPERTS_PER_STEP):
        ce = jnp.sum(jnp.where(lane == e * MOE_EXPERTS_PER_STEP + u, cmb, 0.0), axis=1, keepdims=True)
        hid = _silu(_dot(xb, wg_ref[u])) * _dot(xb, wu_ref[u]) * ce
        down = _dot(hid.astype(BF16), wd_ref[u])
        part = down if part is None else part + down
    acc_scr[...] += part

    @pl.when(e == pl.num_programs(1) - 1)
    def _():
        o_ref[...] = _layer_norm_rows(ALPHA * x_ref[...] + acc_scr[...], g_ref[...], b_ref[...])


def _moe_ln(x2, cmb, wg, wu, wd, gain, bias):
    n = x2.shape[0]
    tm = 1024 if n > 1024 else 512
    full2 = lambda a_: pl.BlockSpec(a_.shape, lambda i, e: (0, 0))
    up_blk = pl.BlockSpec((MOE_EXPERTS_PER_STEP, D_MODEL, EXPERT_FF), lambda i, e: (e, 0, 0))
    return pl.pallas_call(
        _moe_kernel,
        grid=(n // tm, N_EXPERTS // MOE_EXPERTS_PER_STEP),
        in_specs=[pl.BlockSpec((tm, D_MODEL), lambda i, e: (i, 0)),
                  pl.BlockSpec((tm, LANES), lambda i, e: (i, 0)),
                  up_blk, up_blk,
                  pl.BlockSpec((MOE_EXPERTS_PER_STEP, EXPERT_FF, D_MODEL), lambda i, e: (e, 0, 0)),
                  full2(gain), full2(bias)],
        out_specs=pl.BlockSpec((tm, D_MODEL), lambda i, e: (i, 0)),
        out_shape=jax.ShapeDtypeStruct((n, D_MODEL), F32),
        scratch_shapes=[pltpu.VMEM((tm, D_MODEL), BF16), pltpu.VMEM((tm, D_MODEL), F32)],
        compiler_params=_cparams(("arbitrary", "arbitrary"), 48),
        name="moe_ln2",
    )(x2, cmb, wg, wu, wd, gain, bias)


def _layer_weights(l, w_in, w_out, gv_gain, gv_bias, ws, bs, ret_gain, ret_bias, ln1_g, ln1_b, ln2_g, ln2_b,
                   rg_w, rg_b, re_w, re_b, e_gate, e_up, e_down):
    w = w_in[l]
    z = lambda k: jnp.zeros((D_MODEL, k), w.dtype)
    ik_end = 2560 + IDX_DIM
    iw_end = ik_end + IDX_HEADS
    w_pad = jnp.concatenate([w[:, :ik_end], z(COL_IW - COL_IK - IDX_DIM), w[:, ik_end:iw_end],
                             z(COL_CQ - COL_IW - IDX_HEADS), w[:, iw_end:]], axis=1).astype(BF16)
    wo = w_out[l].astype(BF16)
    wr = jnp.concatenate([re_w[l], rg_w[l], jnp.zeros((D_MODEL, LANES - N_EXPERTS - N_GROUPS), F32)], axis=1)
    wrh = wr.astype(BF16)
    wrl = (wr - wrh.astype(F32)).astype(BF16)
    br = jnp.concatenate([re_b[l], rg_b[l], jnp.zeros((LANES - N_EXPERTS - N_GROUPS,), F32)])[None, :]
    row = lambda a: a[l].reshape(1, -1)
    return dict(
        w_pad=w_pad, wa=wo[:A_WIDTH], wb=wo[A_WIDTH:A_WIDTH + B_WIDTH], wc=wo[A_WIDTH + B_WIDTH:],
        gv_gain=row(gv_gain), gv_bias=row(gv_bias), ws=ws[l], bs=bs[l],
        ret_gain=row(ret_gain), ret_bias=row(ret_bias),
        ln1_g=row(ln1_g), ln1_b=row(ln1_b), ln2_g=row(ln2_g), ln2_b=row(ln2_b),
        wrh=wrh, wrl=wrl, br=br,
        wg=e_gate[l].astype(BF16), wu=e_up[l].astype(BF16), wd=e_down[l].astype(BF16))


def _channel_mix(x2, a_out, b_out, c_out, lw):
    x1 = _outproj_ln(a_out, b_out, c_out, lw["wa"], lw["wb"], lw["wc"], x2, lw["ln1_g"], lw["ln1_b"])
    cmb = _router(x1, lw["wrh"], lw["wrl"], lw["br"])
    return _moe_ln(x1, cmb, lw["wg"], lw["wu"], lw["wd"], lw["ln2_g"], lw["ln2_b"])


def _heads(t2, b, l, nh):
    return t2.reshape(b, l, nh, t2.shape[-1] // nh)


def _prompt_layer(x3, lw, tt, cosf, sins):
    b, l, _ = x3.shape
    x2 = x3.reshape(b * l, D_MODEL)
    ha, kt, vt, vtb, ikt, kh, qt, iqt, ikb, iwt = _proj_prompt(x3, lw["w_pad"])
    bsb = jnp.repeat(lw["bs"].T, HEAD_DIM, axis=1)
    a_out, a_vn = _gmlp(ha, PCOL_AU, PCOL_AV, lw["gv_gain"], lw["gv_bias"], lw["ws"], bsb)
    topk = min(TOPK_MAX, l // 4)
    b_out = _dsa_prompt(iqt, iwt, ikb, qt, kh, vtb.reshape(b, B_HEADS, HEAD_DIM, l), tt, topk)
    s0e = jnp.zeros((b, C_WIDTH, C_WIDTH), F32)
    c_out, s_fin = _retention(ha.reshape(b, l, P_WIDTH), (PCOL_CQ, PCOL_CK, PCOL_CV, PCOL_CG), cosf, sins,
                              lw["ret_gain"], lw["ret_bias"], s0e, RET_CHUNK, 1)
    y = _channel_mix(x2, a_out, b_out.reshape(b * l, B_WIDTH), c_out.reshape(b * l, C_WIDTH), lw)
    last = ((l - 1) // CHUNK) * CHUNK
    to_rows = lambda t: jnp.transpose(t.reshape(b, B_HEADS, HEAD_DIM, l), (0, 3, 1, 2))
    state = (to_rows(kt), to_rows(vt), jnp.transpose(ikt, (0, 2, 1)), _state_extract(s_fin),
             a_vn.reshape(b, l, A_WIDTH)[:, last:])
    return y.reshape(b, l, D_MODEL), state


def _sample_layer(x3, lw, layer, cache_k, cache_v, cache_ik, state_l, pt_flat, npages, bfar, bnear, cosf, sins):
    bd, t, _ = x3.shape
    n = bd * t
    past = npages * PAGE_SIZE
    x2 = x3.reshape(n, D_MODEL)
    h2 = _proj(x2, lw["w_pad"])
    col = lambda c0, wdt: h2[:, c0:c0 + wdt]
    rep = CHUNK // t
    eye = jnp.eye(rep, dtype=F32)
    ws_t = lw["ws"][:, :t, :t]
    ws_bd = (eye[None, :, None, :, None] * ws_t[:, None, :, None, :]).reshape(A_GROUPS, CHUNK, CHUNK)
    bsb = jnp.repeat(jnp.tile(lw["bs"][:, :t], (1, rep)).T, HEAD_DIM, axis=1)
    a_out, a_vn = _gmlp(h2, COL_AU, COL_AV, lw["gv_gain"], lw["gv_bias"], ws_bd, bsb)
    k4 = _heads(col(COL_K, B_WIDTH), bd, t, B_HEADS)
    v4 = _heads(col(COL_V, B_WIDTH), bd, t, B_HEADS)
    ik3 = col(COL_IK, IDX_DIM).reshape(bd, t, IDX_DIM)
    feat_major = lambda a3: jnp.pad(jnp.transpose(a3, (0, 2, 1)), ((0, 0), (0, 0), (0, PAGE_SIZE - t)))
    iq4 = _heads(col(COL_IQ, IDX_HEADS * IDX_DIM), bd, t, IDX_HEADS)
    iqs = jnp.transpose(iq4, (0, 2, 1, 3)).reshape(bd, IDX_HEADS * t, IDX_DIM).astype(BF16)
    iw3 = jnp.transpose(col(COL_IW, IDX_HEADS).reshape(bd, t, IDX_HEADS), (0, 2, 1))
    iws = jnp.broadcast_to(iw3.reshape(bd, IDX_HEADS * t, 1), (bd, IDX_HEADS * t, LANES))
    scores = _sample_scores(pt_flat, iqs, iws, jnp.transpose(cache_ik, (0, 1, 3, 2)), layer, feat_major(ik3), npages)
    width = (npages + 1) * PAGE_SIZE
    topk = min(TOPK_MAX, (past + t) // 4)
    madd = _sample_mask(scores.reshape(n, width), topk, past, t).reshape(bd, t, width)
    q4 = _heads(col(COL_Q, B_WIDTH), bd, t, B_HEADS)
    eye_h = jnp.eye(B_HEADS, dtype=F32)
    qbd = (jnp.transpose(q4, (0, 2, 1, 3))[:, :, :, None, :] * eye_h[None, :, None, :, None])
    qbd = qbd.reshape(bd, B_HEADS * t, B_WIDTH).astype(BF16)
    page_view = lambda c: jnp.transpose(c, (0, 1, 3, 4, 2)).reshape(c.shape[0], c.shape[1], B_WIDTH, PAGE_SIZE)
    b_out = _sample_attend(pt_flat, qbd, madd, bfar, bnear, page_view(cache_k), page_view(cache_v), layer,
                           feat_major(k4.reshape(bd, t, B_WIDTH)), feat_major(v4.reshape(bd, t, B_WIDTH)), npages)
    c_out, s_fin = _retention(h2.reshape(bd, t, H_WIDTH), (COL_CQ, COL_CK, COL_CV, COL_CG), cosf, sins,
                              lw["ret_gain"], lw["ret_bias"], _state_embed(state_l), t, 8)
    y = _channel_mix(x2, a_out, b_out.reshape(n, B_WIDTH), c_out.reshape(n, C_WIDTH), lw)
    state = (k4, v4, ik3, _state_extract(s_fin), a_vn.reshape(bd, t, A_WIDTH))
    return y.reshape(bd, t, D_MODEL), state


def _distance_tables(rel_bias, t, past):
    r = jnp.arange(Q_BLOCK, dtype=I32)
    d_prompt = jnp.concatenate([off * Q_BLOCK + r[None, :] - r[:, None] for off in range(3)], axis=0)
    qpos = past + jnp.arange(t, dtype=I32)
    near0 = past - PAGE_SIZE
    d_near = [qpos[:, None] - (near0 + half * PAGE_SIZE + r[None, :]) for half in range(2)]
    d_far = jnp.full((t, LANES), MAX_DISTANCE * 2, I32)
    n0 = 3 * Q_BLOCK
    tables = _bias_tables(jnp.concatenate([d_prompt] + d_near + [d_far], axis=0), rel_bias, n0)
    tt = tables[:, :n0].reshape(B_HEADS, 3, Q_BLOCK, Q_BLOCK)
    bnear = jnp.concatenate([tables[:, n0:n0 + t], tables[:, n0 + t:n0 + 2 * t]], axis=2).reshape(B_HEADS * t, 2 * LANES)
    bfar = tables[:, n0 + 2 * t:n0 + 3 * t].reshape(B_HEADS * t, LANES)
    return tt, bnear, bfar


def kernel(x_prompt, x_sample, cache_k, cache_v, cache_idx_k, state_ret, page_table, w_in, w_out, gmlp_v_gain,
           gmlp_v_bias, gmlp_ws, gmlp_bs, rel_bias, ret_gn_gain, ret_gn_bias, ln1_gain, ln1_bias, ln2_gain, ln2_bias,
           router_group_w, router_group_b, router_expert_w, router_expert_b, expert_w_gate, expert_w_up,
           expert_w_down):
    depth = w_in.shape[0]
    seq = x_prompt.shape[1]
    bd, t, _ = x_sample.shape
    npages = page_table.shape[1]
    past = npages * PAGE_SIZE
    pt_flat = page_table.reshape(-1).astype(I32)
    tt, bnear, bfar = _distance_tables(rel_bias, t, past)
    cos_p, sin_p = _rope_tables(jnp.arange(seq, dtype=I32))
    cos_s, sin_s = _rope_tables(past + jnp.arange(t, dtype=I32))
    xp, xs = x_prompt, x_sample
    st_p, st_s = [], []
    for l in range(depth):
        lw = _layer_weights(l, w_in, w_out, gmlp_v_gain, gmlp_v_bias, gmlp_ws, gmlp_bs, ret_gn_gain, ret_gn_bias,
                            ln1_gain, ln1_bias, ln2_gain, ln2_bias, router_group_w, router_group_b,
                            router_expert_w, router_expert_b, expert_w_gate, expert_w_up, expert_w_down)
        xp, sp = _prompt_layer(xp, lw, tt, cos_p, sin_p)
        xs, ss = _sample_layer(xs, lw, l, cache_k, cache_v, cache_idx_k, state_ret[l], pt_flat, npages,
                               bfar, bnear, cos_s, sin_s)
        st_p.append(sp)
        st_s.append(ss)
    stk = lambda sts, i: jnp.stack([s[i] for s in sts], axis=0)
    return (xp, xs,
            stk(st_p, 0), stk(st_p, 1), stk(st_p, 2), stk(st_p, 3), stk(st_p, 4),
            stk(st_s, 0), stk(st_s, 1), stk(st_s, 2), stk(st_s, 3), stk(st_s, 4))
```

```python
import functools
import math

import jax
import jax.numpy as jnp
from jax import lax
from jax.experimental import pallas as pl
from jax.experimental.pallas import tpu as pltpu

D_MODEL = 1024
HEAD_DIM = 64
A_GROUPS = 4
A_WIDTH = 256
CHUNK = 128
B_HEADS = 8
B_WIDTH = 512
IDX_HEADS = 8
IDX_DIM = 64
TOPK_MAX = 256
Q_BLOCK = 128
KEY_TILE = 256
PHASE_UNROLL = 2
NUM_BUCKETS = 32
MAX_DISTANCE = 128
C_HEADS = 4
C_WIDTH = 256
RET_CHUNK = 128
ROPE_BASE = 10000.0
PAGE_SIZE = 128
N_GROUPS = 4
EXPERTS_PER_GROUP = 8
N_EXPERTS = 32
EXPERT_FF = 256
DEPTH = 2
ALPHA = (2 * DEPTH) ** 0.25
LN_EPS = 1e-5

F32 = jnp.float32
BF16 = jnp.bfloat16
I32 = jnp.int32
I16 = jnp.int16
LANES = 128
MASK_NEG = -1e30
INT_MIN = -2 ** 31

COL_AU, COL_AV, COL_Q, COL_K, COL_V, COL_IQ, COL_IK, COL_IW = 0, 256, 512, 1024, 1536, 2048, 2560, 2688
COL_CQ, COL_CK, COL_CV, COL_CG = 2816, 3072, 3328, 3584
H_WIDTH = 3840


def _cparams(sem, vmem_mb=None):
    kw = dict(dimension_semantics=sem)
    if vmem_mb is not None:
        kw["vmem_limit_bytes"] = vmem_mb << 20
    return pltpu.CompilerParams(**kw)


def _nt_dot(a, b):
    return lax.dot_general(a, b, (((1,), (1,)), ((), ())), preferred_element_type=F32)


def _dot(a, b):
    return jnp.dot(a, b, preferred_element_type=F32)


def _layer_norm_rows(x, gain, bias):
    mu = jnp.mean(x, axis=-1, keepdims=True)
    xc = x - mu
    var = jnp.mean(xc * xc, axis=-1, keepdims=True)
    return xc * lax.rsqrt(var + LN_EPS) * gain + bias


def _silu(x):
    return x * (1.0 / (1.0 + jnp.exp(-x)))


def _proj_kernel(x_ref, w_ref, o_ref):
    o_ref[...] = _dot(x_ref[...].astype(BF16), w_ref[...])


def _proj(x2, w_pad):
    n = x2.shape[0]
    tm = 1024 if n > 1024 else 512
    tn = 768
    return pl.pallas_call(
        _proj_kernel,
        grid=(n // tm, H_WIDTH // tn),
        in_specs=[pl.BlockSpec((tm, D_MODEL), lambda i, j: (i, 0)),
                  pl.BlockSpec((D_MODEL, tn), lambda i, j: (0, j))],
        out_specs=pl.BlockSpec((tm, tn), lambda i, j: (i, j)),
        out_shape=jax.ShapeDtypeStruct((n, H_WIDTH), F32),
        compiler_params=_cparams(("arbitrary", "arbitrary"), 40),
        name="in_proj",
    )(x2, w_pad)


PCOL_AU, PCOL_AV, PCOL_CQ, PCOL_CK, PCOL_CV, PCOL_CG = 0, 256, 512, 768, 1024, 1280
P_WIDTH = 1536


def _proj_prompt_kernel(x_ref, w_ref, ha_ref, kt_ref, vt_ref, vtb_ref, ikt_ref, kh_ref, qt_ref, iqt_ref, ikb_ref,
                        iwt_ref):
    x = x_ref[0].astype(BF16)
    piece = lambda c0, width: _dot(x, w_ref[:, c0:c0 + width])
    ha_ref[:, :COL_Q] = piece(COL_AU, COL_Q)
    ha_ref[:, COL_Q:] = piece(COL_CQ, H_WIDTH - COL_CQ)
    qt_ref[0] = jnp.transpose(piece(COL_Q, B_WIDTH) * (HEAD_DIM ** -0.5 * LOG2E)).astype(BF16)
    iqt_ref[0] = jnp.transpose(piece(COL_IQ, IDX_HEADS * IDX_DIM)).astype(BF16)
    k = piece(COL_K, B_WIDTH)
    for h in range(B_HEADS):
        kh_ref[0, h] = k[:, h * HEAD_DIM:(h + 1) * HEAD_DIM].astype(BF16)
    kt_ref[0] = jnp.transpose(k)
    vt = jnp.transpose(piece(COL_V, B_WIDTH))
    vt_ref[0] = vt
    vtb_ref[0] = vt.astype(BF16)
    ik = piece(COL_IK, LANES)
    ikb_ref[0] = ik[:, :IDX_DIM].astype(BF16)
    ikt_ref[0] = jnp.transpose(ik)[:IDX_DIM]
    iwt_ref[0] = jnp.transpose(piece(COL_IW, LANES))[:IDX_HEADS]


def _proj_prompt(x3, w_pad):
    b, l, _ = x3.shape
    tm = 512
    nt = l // tm
    f32s = lambda *shape: jax.ShapeDtypeStruct(shape, F32)
    bf16s = lambda *shape: jax.ShapeDtypeStruct(shape, BF16)
    feat = lambda rows: pl.BlockSpec((1, rows, tm), lambda i, j: (i, 0, j))
    hmaj = pl.BlockSpec((1, B_HEADS, tm, HEAD_DIM), lambda i, j: (i, 0, j, 0))
    return pl.pallas_call(
        _proj_prompt_kernel,
        grid=(b, nt),
        in_specs=[pl.BlockSpec((1, tm, D_MODEL), lambda i, j: (i, j, 0)),
                  pl.BlockSpec(w_pad.shape, lambda i, j: (0, 0), pipeline_mode=pl.Buffered(1))],
        out_specs=[pl.BlockSpec((tm, P_WIDTH), lambda i, j: (i * nt + j, 0)),
                   feat(B_WIDTH), feat(B_WIDTH), feat(B_WIDTH), feat(IDX_DIM),
                   hmaj, feat(B_WIDTH), feat(IDX_HEADS * IDX_DIM),
                   pl.BlockSpec((1, tm, IDX_DIM), lambda i, j: (i, j, 0)),
                   feat(IDX_HEADS)],
        out_shape=[f32s(b * l, P_WIDTH), f32s(b, B_WIDTH, l), f32s(b, B_WIDTH, l), bf16s(b, B_WIDTH, l),
                   f32s(b, IDX_DIM, l), bf16s(b, B_HEADS, l, HEAD_DIM), bf16s(b, B_WIDTH, l),
                   bf16s(b, IDX_HEADS * IDX_DIM, l), bf16s(b, l, IDX_DIM), f32s(b, IDX_HEADS, l)],
        compiler_params=_cparams(("arbitrary", "arbitrary"), 48),
        name="in_proj_prompt",
    )(x3, w_pad)


LOG2E = math.log2(math.e)


def _bias_kernel(rb_ref, d_ref, o_ref, *, log2_rows):
    n = jnp.maximum(d_ref[...], 0)
    max_exact = NUM_BUCKETS // 2
    nf = jnp.maximum(n, 1).astype(F32)
    large = max_exact + (jnp.log(nf / max_exact) / math.log(MAX_DISTANCE / max_exact)
                         * (NUM_BUCKETS - max_exact)).astype(I32)
    large = jnp.minimum(large, NUM_BUCKETS - 1)
    bucket = jnp.where(n < max_exact, n, large)
    for h in range(B_HEADS):
        acc = jnp.zeros(bucket.shape, F32)
        for bk in range(NUM_BUCKETS):
            acc = jnp.where(bucket == bk, rb_ref[bk * B_HEADS + h], acc)
        o_ref[h, :log2_rows] = acc[:log2_rows] * LOG2E
        o_ref[h, log2_rows:] = acc[log2_rows:]


def _bias_tables(dist, rel_bias, log2_rows):
    r = dist.shape[0]
    return pl.pallas_call(
        functools.partial(_bias_kernel, log2_rows=log2_rows),
        in_specs=[pl.BlockSpec(memory_space=pltpu.SMEM),
                  pl.BlockSpec((r, LANES), lambda: (0, 0))],
        out_specs=pl.BlockSpec((B_HEADS, r, LANES), lambda: (0, 0, 0)),
        out_shape=jax.ShapeDtypeStruct((B_HEADS, r, LANES), F32),
        name="bias_tables",
    )(rel_bias.reshape(-1), dist)


def _gmlp_kernel(u_ref, v_ref, g_ref, b_ref, ws_ref, bsb_ref, o_ref, vn_ref, *, nchunk):
    r = lax.broadcasted_iota(I32, (CHUNK, CHUNK), 0)
    c = lax.broadcasted_iota(I32, (CHUNK, CHUNK), 1)
    grp = lax.broadcasted_iota(I32, (CHUNK, A_WIDTH), 1) // HEAD_DIM
    wts = [jnp.where(r >= c, ws_ref[g], 0.0).astype(BF16) for g in range(A_GROUPS)]
    gain, bias, bsb = g_ref[...], b_ref[...], bsb_ref[...]
    for ci in range(nchunk):
        sl = pl.ds(ci * CHUNK, CHUNK)
        vn = _layer_norm_rows(v_ref[sl, :], gain, bias)
        vn_ref[sl, :] = vn
        vb = vn.astype(BF16)
        mixed = bsb
        for g in range(A_GROUPS):
            mixed = mixed + jnp.where(grp == g, _dot(wts[g], vb), 0.0)
        o_ref[sl, :] = (u_ref[sl, :] * mixed).astype(BF16)


def _gmlp(h2, col_u, col_v, gain, bias, ws, bsb):
    n = h2.shape[0]
    tm = min(n, 1024)
    blk = lambda col: pl.BlockSpec((tm, A_WIDTH), lambda i: (i, col // A_WIDTH))
    full2 = lambda a: pl.BlockSpec(a.shape, lambda i: (0, 0))
    return pl.pallas_call(
        functools.partial(_gmlp_kernel, nchunk=tm // CHUNK),
        grid=(n // tm,),
        in_specs=[blk(col_u), blk(col_v), full2(gain), full2(bias),
                  pl.BlockSpec(ws.shape, lambda i: (0, 0, 0)), full2(bsb)],
        out_specs=[pl.BlockSpec((tm, A_WIDTH), lambda i: (i, 0)),
                   pl.BlockSpec((tm, A_WIDTH), lambda i: (i, 0))],
        out_shape=[jax.ShapeDtypeStruct((n, A_WIDTH), BF16),
                   jax.ShapeDtypeStruct((n, A_WIDTH), F32)],
        compiler_params=_cparams(("arbitrary",)),
        name="gmlp",
    )(h2, h2, gain, bias, ws, bsb)


def _ret_tables(c):
    log_g = jnp.log(1.0 - 2.0 ** (-5.0 - jnp.arange(C_HEADS, dtype=F32)))
    i = jnp.arange(c, dtype=F32)
    diff = i[:, None] - i[None, :]
    dmat = jnp.where(diff >= 0, jnp.exp(log_g[:, None, None] * jnp.maximum(diff, 0.0)), 0.0)
    q_dec = jnp.exp(log_g[:, None] * (i + 1.0))
    k_dec = jnp.exp(log_g[:, None] * (c - 1.0 - i))
    s_dec = jnp.exp(log_g * c)
    qd = jnp.repeat(q_dec.T, HEAD_DIM, axis=1)
    kd = jnp.repeat(k_dec.T, HEAD_DIM, axis=1)
    hid = jnp.arange(C_WIDTH) // HEAD_DIM
    same = hid[:, None] == hid[None, :]
    sd = jnp.where(same, s_dec[hid][:, None], 0.0)
    return dmat, qd, kd, sd, same.astype(F32)


def _rope_tables(pos):
    half = HEAD_DIM // 2
    inv = ROPE_BASE ** (-jnp.arange(half, dtype=F32) / half)
    ang = pos.astype(F32)[:, None] * inv[None, :]
    cos, sin = jnp.cos(ang), jnp.sin(ang)
    cosf = jnp.tile(jnp.concatenate([cos, cos], axis=1), (1, C_HEADS))
    sins = jnp.tile(jnp.concatenate([-sin, sin], axis=1), (1, C_HEADS))
    return cosf, sins


def _ret_kernel(q_ref, k_ref, v_ref, g_ref, cos_ref, sin_ref, qd_ref, kd_ref, dm_ref, sd_ref, bd_ref,
                gg_ref, gb_ref, s0_ref, o_ref, sf_ref, s_scr, *, bt, c):
    ci = pl.program_id(1)

    @pl.when(ci == 0)
    def _():
        for bb in range(bt):
            rows = []
            for h in range(C_HEADS):
                left = [jnp.zeros((HEAD_DIM, h * HEAD_DIM), F32)] if h else []
                right = [jnp.zeros((HEAD_DIM, (C_HEADS - 1 - h) * HEAD_DIM), F32)] if h < C_HEADS - 1 else []
                rows.append(jnp.concatenate(left + [s0_ref[bb, h]] + right, axis=1))
            s_scr[bb] = jnp.concatenate(rows, axis=0)

    lane = lax.broadcasted_iota(I32, (c, C_WIDTH), 1)
    hid = lane // HEAD_DIM
    first = (lane % HEAD_DIM) < (HEAD_DIM // 2)
    cosf, sins = cos_ref[...], sin_ref[...]
    half = HEAD_DIM // 2

    def rot(x):
        partner = jnp.where(first, pltpu.roll(x, C_WIDTH - half, 1), pltpu.roll(x, half, 1))
        return x * cosf + partner * sins

    def seg_mean(x):
        out = jnp.zeros_like(x)
        for h in range(C_HEADS):
            hm = hid == h
            s = jnp.sum(jnp.where(hm, x, 0.0), axis=1, keepdims=True) * (1.0 / HEAD_DIM)
            out = jnp.where(hm, s, out)
        return out

    for bb in range(bt):
        q = rot(q_ref[bb])
        k = rot(k_ref[bb]) * (HEAD_DIM ** -0.5)
        v = v_ref[bb]
        kb, vb = k.astype(BF16), v.astype(BF16)
        s_old = s_scr[bb]
        o = _dot(q.astype(BF16), s_old.astype(BF16)) * qd_ref[...]
        for h in range(C_HEADS):
            hm = hid == h
            att = _nt_dot(jnp.where(hm, q, 0.0).astype(BF16), kb) * dm_ref[h]
            o = o + jnp.where(hm, _dot(att.astype(BF16), vb), 0.0)
        kdt = jnp.transpose(k * kd_ref[...]).astype(BF16)
        s_scr[bb] = s_old * sd_ref[...] + bd_ref[...] * _dot(kdt, vb)
        mu = seg_mean(o)
        oc = o - mu
        var = seg_mean(oc * oc)
        normed = oc * lax.rsqrt(var + LN_EPS) * gg_ref[...] + gb_ref[...]
        o_ref[bb] = (_silu(g_ref[bb]) * normed).astype(BF16)

    @pl.when(ci == pl.num_programs(1) - 1)
    def _():
        for bb in range(bt):
            for h in range(C_HEADS):
                sf_ref[bb, h] = s_scr[bb, h * HEAD_DIM:(h + 1) * HEAD_DIM, h * HEAD_DIM:(h + 1) * HEAD_DIM]


def _retention(h3, cols, cosf, sins, gn_gain, gn_bias, s0, c, bt):
    b, l, _ = h3.shape
    dmat, qd, kd, sd, bd = _ret_tables(c)
    blk = lambda col: pl.BlockSpec((bt, c, C_WIDTH), lambda i, j: (i, j, col // C_WIDTH))
    const2 = lambda a: pl.BlockSpec(a.shape, lambda i, j: (0, 0))
    pos_blk = pl.BlockSpec((c, C_WIDTH), lambda i, j: (j, 0))
    st_blk = pl.BlockSpec((bt, C_HEADS, HEAD_DIM, HEAD_DIM), lambda i, j: (i, 0, 0, 0))
    return pl.pallas_call(
        functools.partial(_ret_kernel, bt=bt, c=c),
        grid=(b // bt, l // c),
        in_specs=[blk(cols[0]), blk(cols[1]), blk(cols[2]), blk(cols[3]), pos_blk, pos_blk,
                  const2(qd), const2(kd), pl.BlockSpec(dmat.shape, lambda i, j: (0, 0, 0)),
                  const2(sd), const2(bd), const2(gn_gain), const2(gn_bias), st_blk],
        out_specs=[pl.BlockSpec((bt, c, C_WIDTH), lambda i, j: (i, j, 0)), st_blk],
        out_shape=[jax.ShapeDtypeStruct((b, l, C_WIDTH), BF16),
                   jax.ShapeDtypeStruct((b, C_HEADS, HEAD_DIM, HEAD_DIM), F32)],
        scratch_shapes=[pltpu.VMEM((bt, C_WIDTH, C_WIDTH), F32)],
        compiler_params=_cparams(("arbitrary", "arbitrary")),
        name="retention",
    )(h3, h3, h3, h3, cosf, sins, qd, kd, dmat, sd, bd, gn_gain, gn_bias, s0)


def _sortable(score):
    bits = pltpu.bitcast(score, I32)
    return bits ^ ((bits >> 31) & 0x7FFFFFFF)


def _dsa_kernel(iq_ref, iw_ref, ik_ref, q_ref, k_ref, vt_ref, tt_ref, o_ref,
                sc_scr, hi_scr, lo_scr, s_scr, acc_scr, j_scr, *, topk):
    qb = pl.program_id(1)
    ntile = (qb + KEY_TILE // Q_BLOCK) // (KEY_TILE // Q_BLOCK)
    row = lax.broadcasted_iota(I32, (KEY_TILE, Q_BLOCK), 0)
    col = lax.broadcasted_iota(I32, (KEY_TILE, Q_BLOCK), 1)
    qpos = qb * Q_BLOCK + col
    head_rows = lambda ref, h: ref[0, h * HEAD_DIM:(h + 1) * HEAD_DIM, :]
    iq_all = jnp.concatenate([head_rows(iq_ref, h) for h in range(IDX_HEADS)], axis=1)
    iw = iw_ref[0]
    int_min = jnp.int32(INT_MIN)
    score_scale = IDX_HEADS ** -0.5

    def tile_rows(j):
        return pl.ds(pl.multiple_of(j * KEY_TILE, KEY_TILE), KEY_TILE)

    def tree_sum(parts):
        while len(parts) > 1:
            parts = [a + b for a, b in zip(parts[::2], parts[1::2])]
        return parts[0]

    def score_tile(j, carry):
        s = _dot(ik_ref[0, tile_rows(j), :], iq_all)
        terms = [jnp.maximum(s[:, h * Q_BLOCK:(h + 1) * Q_BLOCK], 0.0) * iw[h:h + 1, :] for h in range(IDX_HEADS)]
        acc = terms[0]
        for term in terms[1:]:
            acc = acc + term
        score = (acc * (IDX_DIM ** -0.5)) * score_scale
        kint = jnp.where(j * KEY_TILE + row <= qpos, _sortable(score), int_min)
        sc_scr[tile_rows(j), :] = kint
        hi_scr[tile_rows(j), :] = (kint >> 16).astype(I16)
        return carry

    nstep = (ntile + PHASE_UNROLL - 1) // PHASE_UNROLL

    def unrolled(tile_fn):
        def step(js, carry):
            for u in range(PHASE_UNROLL):
                carry = tile_fn(js * PHASE_UNROLL + u, carry)
            return carry
        return step

    lax.fori_loop(0, nstep, unrolled(score_tile), 0)

    def count(pred):
        def body(j, acc):
            m = jnp.where(pred(sc_scr[tile_rows(j), :], j), 1.0, 0.0)
            return acc + tree_sum([m[i * 8:(i + 1) * 8] for i in range(KEY_TILE // 8)])
        acc = lax.fori_loop(0, ntile, body, jnp.zeros((8, Q_BLOCK), F32))
        return jnp.sum(acc, axis=0, keepdims=True)

    half_min = -(1 << 15)
    ones16, zeros16 = jnp.ones((16, Q_BLOCK), I16), jnp.zeros((16, Q_BLOCK), I16)

    def count16(ref, cand, strict=False):
        cand16 = jnp.broadcast_to(cand, (16, Q_BLOCK)).astype(I16)

        def tile(j, acc):
            x = ref[tile_rows(j), :]
            hits = []
            for i in range(KEY_TILE // 16):
                xi = x[i * 16:(i + 1) * 16]
                hits.append(jnp.where((xi > cand16) if strict else (xi >= cand16), ones16, zeros16))
            return acc + tree_sum(hits)

        acc = lax.fori_loop(0, nstep, unrolled(tile), zeros16)
        return jnp.sum(acc.astype(F32), axis=0, keepdims=True)

    def high_bit(i, ans):
        cand = ans | lax.shift_left(jnp.int32(1), 15 - i)
        return jnp.where(count16(hi_scr, cand + half_min) >= topk, cand, ans)

    t_hi = lax.fori_loop(0, 16, high_bit, jnp.zeros((1, Q_BLOCK), I32)) + half_min
    n_above = count16(hi_scr, t_hi, strict=True)

    def low_halves(j, carry):
        t = sc_scr[tile_rows(j), :]
        lo_scr[tile_rows(j), :] = jnp.where((t >> 16) == t_hi, (t & 0xFFFF) + half_min, half_min).astype(I16)
        return carry

    lax.fori_loop(0, nstep, unrolled(low_halves), 0)

    def low_bit(i, ans):
        cand = ans | lax.shift_left(jnp.int32(1), 15 - i)
        return jnp.where(n_above + count16(lo_scr, cand + half_min) >= topk, cand, ans)

    t_lo = lax.fori_loop(0, 16, low_bit, jnp.zeros((1, Q_BLOCK), I32))
    thr = lax.shift_left(t_hi, 16) | t_lo
    need = topk - count(lambda t, j: t > thr)
    n_ge = count(lambda t, j: t >= thr)

    j_scr[...] = jnp.full(j_scr.shape, 1 << 20, I32)
    has_tie = jnp.max(jnp.where((n_ge > topk) & (thr != int_min), 1.0, 0.0)) > 0.0

    @pl.when(has_tie)
    def _():
        def jbit(i, jc):
            cand = jc | lax.shift_left(jnp.int32(1), 12 - i)
            cnt = count(lambda t, j: (t == thr) & (j * KEY_TILE + row < cand))
            return jnp.where(cnt <= need, cand, jc)
        jc = lax.fori_loop(0, 13, jbit, jnp.zeros((1, Q_BLOCK), I32))
        j_scr[...] = jnp.broadcast_to(jc, j_scr.shape)

    j_cut = jnp.where(thr == int_min, 0, j_scr[0:1, :])

    q_scaled = [head_rows(q_ref, h) for h in range(B_HEADS)]
    sub_tiles = KEY_TILE // Q_BLOCK
    groups = KEY_TILE // 8

    def tree_max(parts):
        while len(parts) > 1:
            parts = [jnp.maximum(a, b) for a, b in zip(parts[::2], parts[1::2])]
        return parts[0]

    def logits_tile(j, m_part):
        t = sc_scr[tile_rows(j), :]
        kidx = j * KEY_TILE + row
        madd = jnp.where(t > thr, 0.0, jnp.where(t == thr, jnp.where(kidx < j_cut, 0.0, MASK_NEG), MASK_NEG))
        offs = [jnp.clip(qb - (j * sub_tiles + i), 0, 2) for i in range(sub_tiles)]
        new = []
        for h in range(B_HEADS):
            bias = jnp.concatenate([tt_ref[h, off] for off in offs], axis=0)
            s = _dot(k_ref[0, h, tile_rows(j), :], q_scaled[h]) + bias + madd
            s_scr[h, tile_rows(j), :] = s
            new.append(jnp.maximum(m_part[h], tree_max([s[i * 8:(i + 1) * 8] for i in range(groups)])))
        return jnp.stack(new)

    m_part = lax.fori_loop(0, nstep, unrolled(logits_tile), jnp.full((B_HEADS, 8, Q_BLOCK), MASK_NEG, F32))
    m_rows = [jnp.max(m_part[h], axis=0, keepdims=True) for h in range(B_HEADS)]

    acc_scr[...] = jnp.zeros(acc_scr.shape, F32)

    def values_tile(j, l_part):
        new = []
        for h in range(B_HEADS):
            p = jnp.exp2(s_scr[h, tile_rows(j), :] - m_rows[h])
            new.append(l_part[h] + tree_sum([p[i * 8:(i + 1) * 8] for i in range(groups)]))
            acc_scr[h] += _dot(vt_ref[0, h, :, tile_rows(j)], p.astype(BF16))
        return jnp.stack(new)

    l_part = lax.fori_loop(0, nstep, unrolled(values_tile), jnp.zeros((B_HEADS, 8, Q_BLOCK), F32))
    out_t = jnp.concatenate([acc_scr[h] / jnp.sum(l_part[h], axis=0, keepdims=True) for h in range(B_HEADS)], axis=0)
    o_ref[0] = jnp.transpose(out_t).astype(BF16)


def _dsa_prompt(iqt, iwt, ik3, qt, kh, vt, tt, topk):
    b, _, l, _ = kh.shape
    assert l % (KEY_TILE * PHASE_UNROLL) == 0
    qblk = pl.BlockSpec((1, B_WIDTH, Q_BLOCK), lambda i, j: (i, 0, j))
    once = pl.Buffered(1)
    return pl.pallas_call(
        functools.partial(_dsa_kernel, topk=topk),
        grid=(b, l // Q_BLOCK),
        in_specs=[qblk,
                  pl.BlockSpec((1, IDX_HEADS, Q_BLOCK), lambda i, j: (i, 0, j)),
                  pl.BlockSpec((1, l, IDX_DIM), lambda i, j: (i, 0, 0), pipeline_mode=once),
                  qblk,
                  pl.BlockSpec((1, B_HEADS, l, HEAD_DIM), lambda i, j: (i, 0, 0, 0), pipeline_mode=once),
                  pl.BlockSpec((1, B_HEADS, HEAD_DIM, l), lambda i, j: (i, 0, 0, 0), pipeline_mode=once),
                  pl.BlockSpec(tt.shape, lambda i, j: (0, 0, 0, 0), pipeline_mode=once)],
        out_specs=pl.BlockSpec((1, Q_BLOCK, B_WIDTH), lambda i, j: (i, j, 0)),
        out_shape=jax.ShapeDtypeStruct((b, l, B_WIDTH), BF16),
        scratch_shapes=[pltpu.VMEM((l, Q_BLOCK), I32),
                        pltpu.VMEM((l, Q_BLOCK), I16),
                        pltpu.VMEM((l, Q_BLOCK), I16),
                        pltpu.VMEM((B_HEADS, l, Q_BLOCK), F32),
                        pltpu.VMEM((B_HEADS, HEAD_DIM, Q_BLOCK), F32),
                        pltpu.VMEM((8, Q_BLOCK), I32)],
        compiler_params=_cparams(("arbitrary", "arbitrary"), 56),
        name="dsa_prompt",
    )(iqt, iwt, ik3, qt, kh, vt, tt)


SCORE_BATCH = 2


def _sidx_kernel(pt_ref, iq_ref, iw_ref, *rest, npages):
    pages, ikn_ref, o_ref = rest[:SCORE_BATCH * npages], rest[SCORE_BATCH * npages], rest[SCORE_BATCH * npages + 1]
    nq = iq_ref.shape[1] // IDX_HEADS
    for g in range(SCORE_BATCH):
        iq, iw = iq_ref[g], iw_ref[g]
        for j in range(npages + 1):
            keys_t = (pages[g * npages + j][0, 0] if j < npages else ikn_ref[g]).astype(BF16)
            r = jnp.maximum(_dot(iq, keys_t), 0.0) * iw
            acc = r[0:nq]
            for h in range(1, IDX_HEADS):
                acc = acc + r[h * nq:(h + 1) * nq]
            o_ref[g, :, j * PAGE_SIZE:(j + 1) * PAGE_SIZE] = (acc * (IDX_DIM ** -0.5)) * (IDX_HEADS ** -0.5)


def _sample_scores(pt_flat, iqs, iws, cache_ik, layer, ikn, npages):
    bd, hq, _ = iqs.shape
    nq = hq // IDX_HEADS
    width = (npages + 1) * PAGE_SIZE
    assert bd % SCORE_BATCH == 0
    page_spec = lambda gp: pl.BlockSpec((1, 1, IDX_DIM, PAGE_SIZE),
                                        lambda i, pt, gp=gp: (layer, pt[i * SCORE_BATCH * npages + gp], 0, 0))
    per_b = lambda a: pl.BlockSpec((SCORE_BATCH,) + a.shape[1:], lambda i, pt: (i, 0, 0))
    n_ops = SCORE_BATCH * npages
    return pl.pallas_call(
        functools.partial(_sidx_kernel, npages=npages),
        grid_spec=pltpu.PrefetchScalarGridSpec(
            num_scalar_prefetch=1, grid=(bd // SCORE_BATCH,),
            in_specs=[per_b(iqs), per_b(iws)] + [page_spec(gp) for gp in range(n_ops)] + [per_b(ikn)],
            out_specs=pl.BlockSpec((SCORE_BATCH, nq, width), lambda i, pt: (i, 0, 0))),
        out_shape=jax.ShapeDtypeStruct((bd, nq, width), F32),
        compiler_params=_cparams(("arbitrary",)),
        name="sample_scores",
    )(pt_flat, iqs, iws, *([cache_ik] * n_ops), ikn)


def _sthr_kernel(s_ref, o_ref, k_scr, j_scr, *, topk, past, nq):
    tr, width = s_ref.shape
    colw = lax.broadcasted_iota(I32, (tr, width), 1)
    rowq = lax.broadcasted_iota(I32, (tr, width), 0) % nq
    vis = colw <= past + rowq
    int_min = jnp.int32(INT_MIN)
    k_scr[...] = jnp.where(vis, _sortable(s_ref[...]), int_min)

    def count(pred):
        return jnp.sum(jnp.where(pred(k_scr[...]), 1.0, 0.0), axis=1, keepdims=True)

    def bit_body(i, ans_u):
        cand_u = ans_u | lax.shift_left(jnp.int32(1), 31 - i)
        cand_s = cand_u ^ int_min
        return jnp.where(count(lambda t: t >= cand_s) >= topk, cand_u, ans_u)

    thr = lax.fori_loop(0, 32, bit_body, jnp.zeros((tr, 1), I32)) ^ int_min
    need = topk - count(lambda t: t > thr)
    n_ge = count(lambda t: t >= thr)

    j_scr[...] = jnp.full(j_scr.shape, 1 << 20, I32)
    has_tie = jnp.max(jnp.where((n_ge > topk) & (thr != int_min), 1.0, 0.0)) > 0.0

    @pl.when(has_tie)
    def _():
        def jbit(i, jc):
            cand = jc | lax.shift_left(jnp.int32(1), 12 - i)
            cnt = count(lambda t: (t == thr) & (colw < cand))
            return jnp.where(cnt <= need, cand, jc)
        jc = lax.fori_loop(0, 13, jbit, jnp.zeros((tr, 1), I32))
        j_scr[...] = jnp.broadcast_to(jc, j_scr.shape)

    j_cut = j_scr[:, 0:1]
    t = k_scr[...]
    sel = ((t > thr) | ((t == thr) & (colw < j_cut))) & vis
    o_ref[...] = jnp.where(sel, 0.0, MASK_NEG)


def _sample_mask(scores2, topk, past, nq):
    r, width = scores2.shape
    tr = min(r, 128)
    return pl.pallas_call(
        functools.partial(_sthr_kernel, topk=topk, past=past, nq=nq),
        grid=(r // tr,),
        in_specs=[pl.BlockSpec((tr, width), lambda i: (i, 0))],
        out_specs=pl.BlockSpec((tr, width), lambda i: (i, 0)),
        out_shape=jax.ShapeDtypeStruct((r, width), F32),
        scratch_shapes=[pltpu.VMEM((tr, width), I32), pltpu.VMEM((tr, LANES), I32)],
        compiler_params=_cparams(("arbitrary",)),
        name="sample_mask",
    )(scores2)


def _sattn_kernel(pt_ref, q_ref, m_ref, bfar_ref, bnear_ref, *rest, npages, nq):
    kpages, vpages = rest[:npages], rest[npages:2 * npages]
    kn_ref, vn_ref, o_ref, s_scr = rest[2 * npages:2 * npages + 4]
    q = q_ref[0]
    heads = lambda m: jnp.concatenate([m] * B_HEADS, axis=0)
    scale = HEAD_DIM ** -0.5
    mx = jnp.full((B_HEADS * nq, PAGE_SIZE), MASK_NEG, F32)
    for j in range(npages):
        cols = slice(j * PAGE_SIZE, (j + 1) * PAGE_SIZE)
        bias = bfar_ref[...] if j < npages - 1 else bnear_ref[:, :PAGE_SIZE]
        s = _dot(q, kpages[j][0, 0].astype(BF16)) * scale + bias + heads(m_ref[0, :, cols])
        s_scr[:, cols] = s
        mx = jnp.maximum(mx, s)
    t_new = kn_ref.shape[2]
    new = slice(npages * PAGE_SIZE, npages * PAGE_SIZE + t_new)
    s_new = (_dot(q, kn_ref[0].astype(BF16)) * scale + bnear_ref[:, PAGE_SIZE:PAGE_SIZE + t_new]
             + heads(m_ref[0, :, new]))
    mrow = jnp.maximum(jnp.max(mx, axis=1, keepdims=True), jnp.max(s_new, axis=1, keepdims=True))
    lsum = jnp.zeros((B_HEADS * nq, PAGE_SIZE), F32)
    acc = jnp.zeros((B_HEADS * nq, B_WIDTH), F32)
    for j in range(npages):
        p = jnp.exp(s_scr[:, j * PAGE_SIZE:(j + 1) * PAGE_SIZE] - mrow)
        lsum = lsum + p
        acc = acc + _nt_dot(p.astype(BF16), vpages[j][0, 0].astype(BF16))
    p_new = jnp.exp(s_new - mrow)
    acc = acc + _nt_dot(p_new.astype(BF16), vn_ref[0].astype(BF16))
    acc = acc / (jnp.sum(lsum, axis=1, keepdims=True) + jnp.sum(p_new, axis=1, keepdims=True))
    head = lax.broadcasted_iota(I32, (nq, B_WIDTH), 1) // HEAD_DIM
    out = jnp.zeros((nq, B_WIDTH), F32)
    for h in range(B_HEADS):
        out = jnp.where(head == h, acc[h * nq:(h + 1) * nq], out)
    o_ref[0] = out.astype(BF16)


def _sample_attend(pt_flat, qbd, madd3, bfar, bnear, cache_k, cache_v, layer, kn, vn, npages):
    bd, hq, _ = qbd.shape
    nq = hq // B_HEADS
    width = (npages + 1) * PAGE_SIZE
    page_spec = lambda p: pl.BlockSpec((1, 1, B_WIDTH, PAGE_SIZE),
                                       lambda i, pt, p=p: (layer, pt[i * npages + p], 0, 0))
    per_b = lambda a: pl.BlockSpec((1,) + a.shape[1:], lambda i, pt: (i, 0, 0))
    const2 = lambda a: pl.BlockSpec(a.shape, lambda i, pt: (0, 0))
    pages = [page_spec(p) for p in range(npages)]
    return pl.pallas_call(
        functools.partial(_sattn_kernel, npages=npages, nq=nq),
        grid_spec=pltpu.PrefetchScalarGridSpec(
            num_scalar_prefetch=1, grid=(bd,),
            in_specs=[per_b(qbd), per_b(madd3), const2(bfar), const2(bnear)] + pages + pages + [per_b(kn), per_b(vn)],
            out_specs=pl.BlockSpec((1, nq, B_WIDTH), lambda i, pt: (i, 0, 0)),
            scratch_shapes=[pltpu.VMEM((hq, width), F32)]),
        out_shape=jax.ShapeDtypeStruct((bd, nq, B_WIDTH), BF16),
        compiler_params=_cparams(("arbitrary",), 56),
        name="sample_attend",
    )(pt_flat, qbd, madd3, bfar, bnear, *([cache_k] * npages), *([cache_v] * npages), kn, vn)


def _outproj_kernel(a_ref, b_ref, c_ref, wa_ref, wb_ref, wc_ref, x_ref, g_ref, bb_ref, o_ref):
    mix = _dot(a_ref[...], wa_ref[...]) + _dot(b_ref[...], wb_ref[...]) + _dot(c_ref[...], wc_ref[...])
    o_ref[...] = _layer_norm_rows(ALPHA * x_ref[...] + mix, g_ref[...], bb_ref[...])


def _outproj_ln(a, b, c, wa, wb, wc, x2, gain, bias):
    n = x2.shape[0]
    tm = 512
    rows = lambda a_: pl.BlockSpec((tm, a_.shape[1]), lambda i: (i, 0))
    full2 = lambda a_: pl.BlockSpec(a_.shape, lambda i: (0, 0))
    return pl.pallas_call(
        _outproj_kernel,
        grid=(n // tm,),
        in_specs=[rows(a), rows(b), rows(c), full2(wa), full2(wb), full2(wc), rows(x2), full2(gain), full2(bias)],
        out_specs=pl.BlockSpec((tm, D_MODEL), lambda i: (i, 0)),
        out_shape=jax.ShapeDtypeStruct((n, D_MODEL), F32),
        compiler_params=_cparams(("arbitrary",), 40),
        name="outproj_ln1",
    )(a, b, c, wa, wb, wc, x2, gain, bias)


def _router_kernel(x_ref, wh_ref, wl_ref, b_ref, o_ref):
    x = x_ref[...]
    xh = x.astype(BF16)
    xl = (x - xh.astype(F32)).astype(BF16)
    logits = _dot(xh, wh_ref[...]) + _dot(xh, wl_ref[...]) + _dot(xl, wh_ref[...]) + b_ref[...]
    lane = lax.broadcasted_iota(I32, logits.shape, 1).astype(F32)
    neg = -jnp.inf
    first_lane = lambda m: jnp.min(jnp.where(m, lane, float(LANES)), axis=1, keepdims=True)
    is_g = (lane >= N_EXPERTS) & (lane < N_EXPERTS + N_GROUPS)
    gmax = jnp.max(jnp.where(is_g, logits, neg), axis=1, keepdims=True)
    g_sel = first_lane(is_g & (logits == gmax)) - N_EXPERTS
    p_group = 1.0 / jnp.sum(jnp.where(is_g, jnp.exp(logits - gmax), 0.0), axis=1, keepdims=True)
    in_g = jnp.floor(lane * (1.0 / EXPERTS_PER_GROUP)) == g_sel
    ev = jnp.where(in_g, logits, neg)
    v1 = jnp.max(ev, axis=1, keepdims=True)
    i1 = first_lane(in_g & (logits == v1))
    ev2 = jnp.where(lane == i1, neg, ev)
    v2 = jnp.max(ev2, axis=1, keepdims=True)
    i2 = first_lane(in_g & (lane != i1) & (logits == v2))
    e2 = jnp.exp(v2 - v1)
    den = 1.0 + e2
    g1 = (1.0 / den) * p_group
    g2 = (e2 / den) * p_group
    o_ref[...] = jnp.where(lane == i1, g1, jnp.where(lane == i2, g2, 0.0))


def _router(x2, wh, wl, bias):
    n = x2.shape[0]
    tm = 512
    full2 = lambda a_: pl.BlockSpec(a_.shape, lambda i: (0, 0))
    return pl.pallas_call(
        _router_kernel,
        grid=(n // tm,),
        in_specs=[pl.BlockSpec((tm, D_MODEL), lambda i: (i, 0)), full2(wh), full2(wl), full2(bias)],
        out_specs=pl.BlockSpec((tm, LANES), lambda i: (i, 0)),
        out_shape=jax.ShapeDtypeStruct((n, LANES), F32),
        compiler_params=_cparams(("arbitrary",)),
        name="router",
    )(x2, wh, wl, bias)


MOE_EXPERTS_PER_STEP = 4


def _moe_kernel(x_ref, cmb_ref, wg_ref, wu_ref, wd_ref, g_ref, b_ref, o_ref, xb_scr, acc_scr):
    e = pl.program_id(1)

    @pl.when(e == 0)
    def _():
        xb_scr[...] = x_ref[...].astype(BF16)
        acc_scr[...] = jnp.zeros(acc_scr.shape, F32)

    cmb = cmb_ref[...]
    lane = lax.broadcasted_iota(I32, cmb.shape, 1)
    xb = xb_scr[...]
    part = None
    for u in range(MOE_EXPERTS_PER_STEP):
        ce = jnp.sum(jnp.where(lane == e * MOE_EXPERTS_PER_STEP + u, cmb, 0.0), axis=1, keepdims=True)
        hid = _silu(_dot(xb, wg_ref[u])) * _dot(xb, wu_ref[u]) * ce
        down = _dot(hid.astype(BF16), wd_ref[u])
        part = down if part is None else part + down
    acc_scr[...] += part

    @pl.when(e == pl.num_programs(1) - 1)
    def _():
        o_ref[...] = _layer_norm_rows(ALPHA * x_ref[...] + acc_scr[...], g_ref[...], b_ref[...])


def _moe_ln(x2, cmb, wg, wu, wd, gain, bias):
    n = x2.shape[0]
    tm = 1024 if n > 1024 else 512
    full2 = lambda a_: pl.BlockSpec(a_.shape, lambda i, e: (0, 0))
    up_blk = pl.BlockSpec((MOE_EXPERTS_PER_STEP, D_MODEL, EXPERT_FF), lambda i, e: (e, 0, 0))
    return pl.pallas_call(
        _moe_kernel,
        grid=(n // tm, N_EXPERTS // MOE_EXPERTS_PER_STEP),
        in_specs=[pl.BlockSpec((tm, D_MODEL), lambda i, e: (i, 0)),
                  pl.BlockSpec((tm, LANES), lambda i, e: (i, 0)),
                  up_blk, up_blk,
                  pl.BlockSpec((MOE_EXPERTS_PER_STEP, EXPERT_FF, D_MODEL), lambda i, e: (e, 0, 0)),
                  full2(gain), full2(bias)],
        out_specs=pl.BlockSpec((tm, D_MODEL), lambda i, e: (i, 0)),
        out_shape=jax.ShapeDtypeStruct((n, D_MODEL), F32),
        scratch_shapes=[pltpu.VMEM((tm, D_MODEL), BF16), pltpu.VMEM((tm, D_MODEL), F32)],
        compiler_params=_cparams(("arbitrary", "arbitrary"), 48),
        name="moe_ln2",
    )(x2, cmb, wg, wu, wd, gain, bias)


def _layer_weights(l, w_in, w_out, gv_gain, gv_bias, ws, bs, ret_gain, ret_bias, ln1_g, ln1_b, ln2_g, ln2_b,
                   rg_w, rg_b, re_w, re_b, e_gate, e_up, e_down):
    w = w_in[l]
    z = lambda k: jnp.zeros((D_MODEL, k), w.dtype)
    ik_end = 2560 + IDX_DIM
    iw_end = ik_end + IDX_HEADS
    w_pad = jnp.concatenate([w[:, :ik_end], z(COL_IW - COL_IK - IDX_DIM), w[:, ik_end:iw_end],
                             z(COL_CQ - COL_IW - IDX_HEADS), w[:, iw_end:]], axis=1).astype(BF16)
    wo = w_out[l].astype(BF16)
    wr = jnp.concatenate([re_w[l], rg_w[l], jnp.zeros((D_MODEL, LANES - N_EXPERTS - N_GROUPS), F32)], axis=1)
    wrh = wr.astype(BF16)
    wrl = (wr - wrh.astype(F32)).astype(BF16)
    br = jnp.concatenate([re_b[l], rg_b[l], jnp.zeros((LANES - N_EXPERTS - N_GROUPS,), F32)])[None, :]
    row = lambda a: a[l].reshape(1, -1)
    return dict(
        w_pad=w_pad, wa=wo[:A_WIDTH], wb=wo[A_WIDTH:A_WIDTH + B_WIDTH], wc=wo[A_WIDTH + B_WIDTH:],
        gv_gain=row(gv_gain), gv_bias=row(gv_bias), ws=ws[l], bs=bs[l],
        ret_gain=row(ret_gain), ret_bias=row(ret_bias),
        ln1_g=row(ln1_g), ln1_b=row(ln1_b), ln2_g=row(ln2_g), ln2_b=row(ln2_b),
        wrh=wrh, wrl=wrl, br=br,
        wg=e_gate[l].astype(BF16), wu=e_up[l].astype(BF16), wd=e_down[l].astype(BF16))


def _channel_mix(x2, a_out, b_out, c_out, lw):
    x1 = _outproj_ln(a_out, b_out, c_out, lw["wa"], lw["wb"], lw["wc"], x2, lw["ln1_g"], lw["ln1_b"])
    cmb = _router(x1, lw["wrh"], lw["wrl"], lw["br"])
    return _moe_ln(x1, cmb, lw["wg"], lw["wu"], lw["wd"], lw["ln2_g"], lw["ln2_b"])


def _heads(t2, b, l, nh):
    return t2.reshape(b, l, nh, t2.shape[-1] // nh)


def _prompt_layer(x3, lw, tt, cosf, sins):
    b, l, _ = x3.shape
    x2 = x3.reshape(b * l, D_MODEL)
    ha, kt, vt, vtb, ikt, kh, qt, iqt, ikb, iwt = _proj_prompt(x3, lw["w_pad"])
    bsb = jnp.repeat(lw["bs"].T, HEAD_DIM, axis=1)
    a_out, a_vn = _gmlp(ha, PCOL_AU, PCOL_AV, lw["gv_gain"], lw["gv_bias"], lw["ws"], bsb)
    topk = min(TOPK_MAX, l // 4)
    b_out = _dsa_prompt(iqt, iwt, ikb, qt, kh, vtb.reshape(b, B_HEADS, HEAD_DIM, l), tt, topk)
    s0e = jnp.zeros((b, C_HEADS, HEAD_DIM, HEAD_DIM), F32)
    c_out, s_fin = _retention(ha.reshape(b, l, P_WIDTH), (PCOL_CQ, PCOL_CK, PCOL_CV, PCOL_CG), cosf, sins,
                              lw["ret_gain"], lw["ret_bias"], s0e, RET_CHUNK, 1)
    y = _channel_mix(x2, a_out, b_out.reshape(b * l, B_WIDTH), c_out.reshape(b * l, C_WIDTH), lw)
    last = ((l - 1) // CHUNK) * CHUNK
    to_rows = lambda t: jnp.transpose(t.reshape(b, B_HEADS, HEAD_DIM, l), (0, 3, 1, 2))
    state = (to_rows(kt), to_rows(vt), jnp.transpose(ikt, (0, 2, 1)), s_fin,
             a_vn.reshape(b, l, A_WIDTH)[:, last:])
    return y.reshape(b, l, D_MODEL), state


def _sample_layer(x3, lw, layer, cache_k, cache_v, cache_ik, state_l, pt_flat, npages, bfar, bnear, cosf, sins):
    bd, t, _ = x3.shape
    n = bd * t
    past = npages * PAGE_SIZE
    x2 = x3.reshape(n, D_MODEL)
    h2 = _proj(x2, lw["w_pad"])
    col = lambda c0, wdt: h2[:, c0:c0 + wdt]
    rep = CHUNK // t
    eye = jnp.eye(rep, dtype=F32)
    ws_t = lw["ws"][:, :t, :t]
    ws_bd = (eye[None, :, None, :, None] * ws_t[:, None, :, None, :]).reshape(A_GROUPS, CHUNK, CHUNK)
    bsb = jnp.repeat(jnp.tile(lw["bs"][:, :t], (1, rep)).T, HEAD_DIM, axis=1)
    a_out, a_vn = _gmlp(h2, COL_AU, COL_AV, lw["gv_gain"], lw["gv_bias"], ws_bd, bsb)
    k4 = _heads(col(COL_K, B_WIDTH), bd, t, B_HEADS)
    v4 = _heads(col(COL_V, B_WIDTH), bd, t, B_HEADS)
    ik3 = col(COL_IK, IDX_DIM).reshape(bd, t, IDX_DIM)
    feat_major = lambda a3: jnp.pad(jnp.transpose(a3, (0, 2, 1)), ((0, 0), (0, 0), (0, PAGE_SIZE - t)))
    iq4 = _heads(col(COL_IQ, IDX_HEADS * IDX_DIM), bd, t, IDX_HEADS)
    iqs = jnp.transpose(iq4, (0, 2, 1, 3)).reshape(bd, IDX_HEADS * t, IDX_DIM).astype(BF16)
    iw3 = jnp.transpose(col(COL_IW, IDX_HEADS).reshape(bd, t, IDX_HEADS), (0, 2, 1))
    iws = jnp.broadcast_to(iw3.reshape(bd, IDX_HEADS * t, 1), (bd, IDX_HEADS * t, LANES))
    scores = _sample_scores(pt_flat, iqs, iws, jnp.transpose(cache_ik, (0, 1, 3, 2)), layer, feat_major(ik3), npages)
    width = (npages + 1) * PAGE_SIZE
    topk = min(TOPK_MAX, (past + t) // 4)
    madd = _sample_mask(scores.reshape(n, width), topk, past, t).reshape(bd, t, width)
    q4 = _heads(col(COL_Q, B_WIDTH), bd, t, B_HEADS)
    eye_h = jnp.eye(B_HEADS, dtype=F32)
    qbd = (jnp.transpose(q4, (0, 2, 1, 3))[:, :, :, None, :] * eye_h[None, :, None, :, None])
    qbd = qbd.reshape(bd, B_HEADS * t, B_WIDTH).astype(BF16)
    page_view = lambda c: jnp.transpose(c, (0, 1, 3, 4, 2)).reshape(c.shape[0], c.shape[1], B_WIDTH, PAGE_SIZE)
    new_rows = lambda a4: jnp.transpose(a4.reshape(bd, t, B_WIDTH), (0, 2, 1))
    b_out = _sample_attend(pt_flat, qbd, madd, bfar, bnear, page_view(cache_k), page_view(cache_v), layer,
                           new_rows(k4), new_rows(v4), npages)
    c_out, s_fin = _retention(h2.reshape(bd, t, H_WIDTH), (COL_CQ, COL_CK, COL_CV, COL_CG), cosf, sins,
                              lw["ret_gain"], lw["ret_bias"], state_l, t, 8)
    y = _channel_mix(x2, a_out, b_out.reshape(n, B_WIDTH), c_out.reshape(n, C_WIDTH), lw)
    state = (k4, v4, ik3, s_fin, a_vn.reshape(bd, t, A_WIDTH))
    return y.reshape(bd, t, D_MODEL), state


def _distance_tables(rel_bias, t, past):
    r = jnp.arange(Q_BLOCK, dtype=I32)
    d_prompt = jnp.concatenate([off * Q_BLOCK + r[None, :] - r[:, None] for off in range(3)], axis=0)
    qpos = past + jnp.arange(t, dtype=I32)
    near0 = past - PAGE_SIZE
    d_near = [qpos[:, None] - (near0 + half * PAGE_SIZE + r[None, :]) for half in range(2)]
    d_far = jnp.full((t, LANES), MAX_DISTANCE * 2, I32)
    n0 = 3 * Q_BLOCK
    tables = _bias_tables(jnp.concatenate([d_prompt] + d_near + [d_far], axis=0), rel_bias, n0)
    tt = tables[:, :n0].reshape(B_HEADS, 3, Q_BLOCK, Q_BLOCK)
    bnear = jnp.concatenate([tables[:, n0:n0 + t], tables[:, n0 + t:n0 + 2 * t]], axis=2).reshape(B_HEADS * t, 2 * LANES)
    bfar = tables[:, n0 + 2 * t:n0 + 3 * t].reshape(B_HEADS * t, LANES)
    return tt, bnear, bfar


def kernel(x_prompt, x_sample, cache_k, cache_v, cache_idx_k, state_ret, page_table, w_in, w_out, gmlp_v_gain,
           gmlp_v_bias, gmlp_ws, gmlp_bs, rel_bias, ret_gn_gain, ret_gn_bias, ln1_gain, ln1_bias, ln2_gain, ln2_bias,
           router_group_w, router_group_b, router_expert_w, router_expert_b, expert_w_gate, expert_w_up,
           expert_w_down):
    depth = w_in.shape[0]
    seq = x_prompt.shape[1]
    bd, t, _ = x_sample.shape
    npages = page_table.shape[1]
    past = npages * PAGE_SIZE
    pt_flat = page_table.reshape(-1).astype(I32)
    tt, bnear, bfar = _distance_tables(rel_bias, t, past)
    cos_p, sin_p = _rope_tables(jnp.arange(seq, dtype=I32))
    cos_s, sin_s = _rope_tables(past + jnp.arange(t, dtype=I32))
    xp, xs = x_prompt, x_sample
    st_p, st_s = [], []
    for l in range(depth):
        lw = _layer_weights(l, w_in, w_out, gmlp_v_gain, gmlp_v_bias, gmlp_ws, gmlp_bs, ret_gn_gain, ret_gn_bias,
                            ln1_gain, ln1_bias, ln2_gain, ln2_bias, router_group_w, router_group_b,
                            router_expert_w, router_expert_b, expert_w_gate, expert_w_up, expert_w_down)
        xp, sp = _prompt_layer(xp, lw, tt, cos_p, sin_p)
        xs, ss = _sample_layer(xs, lw, l, cache_k, cache_v, cache_idx_k, state_ret[l], pt_flat, npages,
                               bfar, bnear, cos_s, sin_s)
        st_p.append(sp)
        st_s.append(ss)
    stk = lambda sts, i: jnp.stack([s[i] for s in sts], axis=0)
    return (xp, xs,
            stk(st_p, 0), stk(st_p, 1), stk(st_p, 2), stk(st_p, 3), stk(st_p, 4),
            stk(st_s, 0), stk(st_s, 1), stk(st_s, 2), stk(st_s, 3), stk(st_s, 4))
```

```python
import functools
import math

import jax
import jax.numpy as jnp
from jax import lax
from jax.experimental import pallas as pl
from jax.experimental.pallas import tpu as pltpu

D_MODEL = 1024
HEAD_DIM = 64
A_GROUPS = 4
A_WIDTH = 256
CHUNK = 128
B_HEADS = 8
B_WIDTH = 512
IDX_HEADS = 8
IDX_DIM = 64
TOPK_MAX = 256
Q_BLOCK = 128
KEY_TILE = 256
PHASE_UNROLL = 2
NUM_BUCKETS = 32
MAX_DISTANCE = 128
C_HEADS = 4
C_WIDTH = 256
RET_CHUNK = 128
ROPE_BASE = 10000.0
PAGE_SIZE = 128
N_GROUPS = 4
EXPERTS_PER_GROUP = 8
N_EXPERTS = 32
EXPERT_FF = 256
DEPTH = 2
ALPHA = (2 * DEPTH) ** 0.25
LN_EPS = 1e-5

F32 = jnp.float32
BF16 = jnp.bfloat16
I32 = jnp.int32
I16 = jnp.int16
LANES = 128
MASK_NEG = -1e30
INT_MIN = -2 ** 31

COL_AU, COL_AV, COL_Q, COL_K, COL_V, COL_IQ, COL_IK, COL_IW = 0, 256, 512, 1024, 1536, 2048, 2560, 2688
COL_CQ, COL_CK, COL_CV, COL_CG = 2816, 3072, 3328, 3584
H_WIDTH = 3840


def _cparams(sem, vmem_mb=None):
    kw = dict(dimension_semantics=sem)
    if vmem_mb is not None:
        kw["vmem_limit_bytes"] = vmem_mb << 20
    return pltpu.CompilerParams(**kw)


def _nt_dot(a, b):
    return lax.dot_general(a, b, (((1,), (1,)), ((), ())), preferred_element_type=F32)


def _dot(a, b):
    return jnp.dot(a, b, preferred_element_type=F32)


def _layer_norm_rows(x, gain, bias):
    mu = jnp.mean(x, axis=-1, keepdims=True)
    xc = x - mu
    var = jnp.mean(xc * xc, axis=-1, keepdims=True)
    return xc * lax.rsqrt(var + LN_EPS) * gain + bias


def _silu(x):
    return x * (1.0 / (1.0 + jnp.exp(-x)))


def _proj_kernel(x_ref, w_ref, o_ref):
    o_ref[...] = _dot(x_ref[...].astype(BF16), w_ref[...])


def _proj(x2, w_pad):
    n = x2.shape[0]
    tm = 1024 if n > 1024 else 512
    tn = 768
    return pl.pallas_call(
        _proj_kernel,
        grid=(n // tm, H_WIDTH // tn),
        in_specs=[pl.BlockSpec((tm, D_MODEL), lambda i, j: (i, 0)),
                  pl.BlockSpec((D_MODEL, tn), lambda i, j: (0, j))],
        out_specs=pl.BlockSpec((tm, tn), lambda i, j: (i, j)),
        out_shape=jax.ShapeDtypeStruct((n, H_WIDTH), F32),
        compiler_params=_cparams(("arbitrary", "arbitrary"), 40),
        name="in_proj",
    )(x2, w_pad)


PCOL_AU, PCOL_AV, PCOL_CQ, PCOL_CK, PCOL_CV, PCOL_CG = 0, 256, 512, 768, 1024, 1280
P_WIDTH = 1536


def _proj_prompt_kernel(x_ref, w_ref, ha_ref, kt_ref, vt_ref, vtb_ref, ikt_ref, kh_ref, qt_ref, iqt_ref, ikb_ref,
                        iwt_ref):
    x = x_ref[0].astype(BF16)
    piece = lambda c0, width: _dot(x, w_ref[:, c0:c0 + width])
    ha_ref[:, :COL_Q] = piece(COL_AU, COL_Q)
    ha_ref[:, COL_Q:] = piece(COL_CQ, H_WIDTH - COL_CQ)
    qt_ref[0] = jnp.transpose(piece(COL_Q, B_WIDTH) * (HEAD_DIM ** -0.5 * LOG2E)).astype(BF16)
    iqt_ref[0] = jnp.transpose(piece(COL_IQ, IDX_HEADS * IDX_DIM)).astype(BF16)
    k = piece(COL_K, B_WIDTH)
    for h in range(B_HEADS):
        kh_ref[0, h] = k[:, h * HEAD_DIM:(h + 1) * HEAD_DIM].astype(BF16)
    kt_ref[0] = jnp.transpose(k)
    vt = jnp.transpose(piece(COL_V, B_WIDTH))
    vt_ref[0] = vt
    vtb_ref[0] = vt.astype(BF16)
    ik = piece(COL_IK, LANES)
    ikb_ref[0] = ik[:, :IDX_DIM].astype(BF16)
    ikt_ref[0] = jnp.transpose(ik)[:IDX_DIM]
    iwt_ref[0] = jnp.transpose(piece(COL_IW, LANES))[:IDX_HEADS]


def _proj_prompt(x3, w_pad):
    b, l, _ = x3.shape
    tm = 512
    nt = l // tm
    f32s = lambda *shape: jax.ShapeDtypeStruct(shape, F32)
    bf16s = lambda *shape: jax.ShapeDtypeStruct(shape, BF16)
    feat = lambda rows: pl.BlockSpec((1, rows, tm), lambda i, j: (i, 0, j))
    hmaj = pl.BlockSpec((1, B_HEADS, tm, HEAD_DIM), lambda i, j: (i, 0, j, 0))
    return pl.pallas_call(
        _proj_prompt_kernel,
        grid=(b, nt),
        in_specs=[pl.BlockSpec((1, tm, D_MODEL), lambda i, j: (i, j, 0)),
                  pl.BlockSpec(w_pad.shape, lambda i, j: (0, 0), pipeline_mode=pl.Buffered(1))],
        out_specs=[pl.BlockSpec((tm, P_WIDTH), lambda i, j: (i * nt + j, 0)),
                   feat(B_WIDTH), feat(B_WIDTH), feat(B_WIDTH), feat(IDX_DIM),
                   hmaj, feat(B_WIDTH), feat(IDX_HEADS * IDX_DIM),
                   pl.BlockSpec((1, tm, IDX_DIM), lambda i, j: (i, j, 0)),
                   feat(IDX_HEADS)],
        out_shape=[f32s(b * l, P_WIDTH), f32s(b, B_WIDTH, l), f32s(b, B_WIDTH, l), bf16s(b, B_WIDTH, l),
                   f32s(b, IDX_DIM, l), bf16s(b, B_HEADS, l, HEAD_DIM), bf16s(b, B_WIDTH, l),
                   bf16s(b, IDX_HEADS * IDX_DIM, l), bf16s(b, l, IDX_DIM), f32s(b, IDX_HEADS, l)],
        compiler_params=_cparams(("arbitrary", "arbitrary"), 48),
        name="in_proj_prompt",
    )(x3, w_pad)


LOG2E = math.log2(math.e)


def _bias_kernel(rb_ref, d_ref, o_ref, *, log2_rows):
    n = jnp.maximum(d_ref[...], 0)
    max_exact = NUM_BUCKETS // 2
    nf = jnp.maximum(n, 1).astype(F32)
    large = max_exact + (jnp.log(nf / max_exact) / math.log(MAX_DISTANCE / max_exact)
                         * (NUM_BUCKETS - max_exact)).astype(I32)
    large = jnp.minimum(large, NUM_BUCKETS - 1)
    bucket = jnp.where(n < max_exact, n, large)
    for h in range(B_HEADS):
        acc = jnp.zeros(bucket.shape, F32)
        for bk in range(NUM_BUCKETS):
            acc = jnp.where(bucket == bk, rb_ref[bk * B_HEADS + h], acc)
        o_ref[h, :log2_rows] = acc[:log2_rows] * LOG2E
        o_ref[h, log2_rows:] = acc[log2_rows:]


def _bias_tables(dist, rel_bias, log2_rows):
    r = dist.shape[0]
    return pl.pallas_call(
        functools.partial(_bias_kernel, log2_rows=log2_rows),
        in_specs=[pl.BlockSpec(memory_space=pltpu.SMEM),
                  pl.BlockSpec((r, LANES), lambda: (0, 0))],
        out_specs=pl.BlockSpec((B_HEADS, r, LANES), lambda: (0, 0, 0)),
        out_shape=jax.ShapeDtypeStruct((B_HEADS, r, LANES), F32),
        name="bias_tables",
    )(rel_bias.reshape(-1), dist)


def _gmlp_kernel(u_ref, v_ref, g_ref, b_ref, ws_ref, bsb_ref, o_ref, vn_ref, *, nchunk):
    r = lax.broadcasted_iota(I32, (CHUNK, CHUNK), 0)
    c = lax.broadcasted_iota(I32, (CHUNK, CHUNK), 1)
    grp = lax.broadcasted_iota(I32, (CHUNK, A_WIDTH), 1) // HEAD_DIM
    wts = [jnp.where(r >= c, ws_ref[g], 0.0).astype(BF16) for g in range(A_GROUPS)]
    gain, bias, bsb = g_ref[...], b_ref[...], bsb_ref[...]
    for ci in range(nchunk):
        sl = pl.ds(ci * CHUNK, CHUNK)
        vn = _layer_norm_rows(v_ref[sl, :], gain, bias)
        vn_ref[sl, :] = vn
        vb = vn.astype(BF16)
        mixed = bsb
        for g in range(A_GROUPS):
            mixed = mixed + jnp.where(grp == g, _dot(wts[g], vb), 0.0)
        o_ref[sl, :] = (u_ref[sl, :] * mixed).astype(BF16)


def _gmlp(h2, col_u, col_v, gain, bias, ws, bsb):
    n = h2.shape[0]
    tm = min(n, 1024)
    blk = lambda col: pl.BlockSpec((tm, A_WIDTH), lambda i: (i, col // A_WIDTH))
    full2 = lambda a: pl.BlockSpec(a.shape, lambda i: (0, 0))
    return pl.pallas_call(
        functools.partial(_gmlp_kernel, nchunk=tm // CHUNK),
        grid=(n // tm,),
        in_specs=[blk(col_u), blk(col_v), full2(gain), full2(bias),
                  pl.BlockSpec(ws.shape, lambda i: (0, 0, 0)), full2(bsb)],
        out_specs=[pl.BlockSpec((tm, A_WIDTH), lambda i: (i, 0)),
                   pl.BlockSpec((tm, A_WIDTH), lambda i: (i, 0))],
        out_shape=[jax.ShapeDtypeStruct((n, A_WIDTH), BF16),
                   jax.ShapeDtypeStruct((n, A_WIDTH), F32)],
        compiler_params=_cparams(("arbitrary",)),
        name="gmlp",
    )(h2, h2, gain, bias, ws, bsb)


def _ret_tables(c):
    log_g = jnp.log(1.0 - 2.0 ** (-5.0 - jnp.arange(C_HEADS, dtype=F32)))
    i = jnp.arange(c, dtype=F32)
    diff = i[:, None] - i[None, :]
    dmat = jnp.where(diff >= 0, jnp.exp(log_g[:, None, None] * jnp.maximum(diff, 0.0)), 0.0)
    q_dec = jnp.exp(log_g[:, None] * (i + 1.0))
    k_dec = jnp.exp(log_g[:, None] * (c - 1.0 - i))
    s_dec = jnp.exp(log_g * c)
    qd = jnp.repeat(q_dec.T, HEAD_DIM, axis=1)
    kd = jnp.repeat(k_dec.T, HEAD_DIM, axis=1)
    hid = jnp.arange(C_WIDTH) // HEAD_DIM
    same = hid[:, None] == hid[None, :]
    sd = jnp.where(same, s_dec[hid][:, None], 0.0)
    return dmat, qd, kd, sd, same.astype(F32)


def _rope_tables(pos):
    half = HEAD_DIM // 2
    inv = ROPE_BASE ** (-jnp.arange(half, dtype=F32) / half)
    ang = pos.astype(F32)[:, None] * inv[None, :]
    cos, sin = jnp.cos(ang), jnp.sin(ang)
    cosf = jnp.tile(jnp.concatenate([cos, cos], axis=1), (1, C_HEADS))
    sins = jnp.tile(jnp.concatenate([-sin, sin], axis=1), (1, C_HEADS))
    return cosf, sins


def _ret_kernel(q_ref, k_ref, v_ref, g_ref, cos_ref, sin_ref, qd_ref, kd_ref, dm_ref, sd_ref, bd_ref,
                gg_ref, gb_ref, s0_ref, o_ref, sf_ref, s_scr, *, bt, c):
    ci = pl.program_id(1)

    @pl.when(ci == 0)
    def _():
        for bb in range(bt):
            rows = []
            for h in range(C_HEADS):
                left = [jnp.zeros((HEAD_DIM, h * HEAD_DIM), F32)] if h else []
                right = [jnp.zeros((HEAD_DIM, (C_HEADS - 1 - h) * HEAD_DIM), F32)] if h < C_HEADS - 1 else []
                rows.append(jnp.concatenate(left + [s0_ref[bb, h]] + right, axis=1))
            s_scr[bb] = jnp.concatenate(rows, axis=0)

    lane = lax.broadcasted_iota(I32, (c, C_WIDTH), 1)
    hid = lane // HEAD_DIM
    first = (lane % HEAD_DIM) < (HEAD_DIM // 2)
    cosf, sins = cos_ref[...], sin_ref[...]
    half = HEAD_DIM // 2

    def rot(x):
        partner = jnp.where(first, pltpu.roll(x, C_WIDTH - half, 1), pltpu.roll(x, half, 1))
        return x * cosf + partner * sins

    def seg_mean(x):
        out = jnp.zeros_like(x)
        for h in range(C_HEADS):
            hm = hid == h
            s = jnp.sum(jnp.where(hm, x, 0.0), axis=1, keepdims=True) * (1.0 / HEAD_DIM)
            out = jnp.where(hm, s, out)
        return out

    for bb in range(bt):
        q = rot(q_ref[bb])
        k = rot(k_ref[bb]) * (HEAD_DIM ** -0.5)
        v = v_ref[bb]
        kb, vb = k.astype(BF16), v.astype(BF16)
        s_old = s_scr[bb]
        o = _dot(q.astype(BF16), s_old.astype(BF16)) * qd_ref[...]
        for h in range(C_HEADS):
            hm = hid == h
            att = _nt_dot(jnp.where(hm, q, 0.0).astype(BF16), kb) * dm_ref[h]
            o = o + jnp.where(hm, _dot(att.astype(BF16), vb), 0.0)
        kdt = jnp.transpose(k * kd_ref[...]).astype(BF16)
        s_scr[bb] = s_old * sd_ref[...] + bd_ref[...] * _dot(kdt, vb)
        mu = seg_mean(o)
        oc = o - mu
        var = seg_mean(oc * oc)
        normed = oc * lax.rsqrt(var + LN_EPS) * gg_ref[...] + gb_ref[...]
        o_ref[bb] = (_silu(g_ref[bb]) * normed).astype(BF16)

    @pl.when(ci == pl.num_programs(1) - 1)
    def _():
        for bb in range(bt):
            for h in range(C_HEADS):
                sf_ref[bb, h] = s_scr[bb, h * HEAD_DIM:(h + 1) * HEAD_DIM, h * HEAD_DIM:(h + 1) * HEAD_DIM]


def _retention(h3, cols, cosf, sins, gn_gain, gn_bias, s0, c, bt):
    b, l, _ = h3.shape
    dmat, qd, kd, sd, bd = _ret_tables(c)
    blk = lambda col: pl.BlockSpec((bt, c, C_WIDTH), lambda i, j: (i, j, col // C_WIDTH))
    const2 = lambda a: pl.BlockSpec(a.shape, lambda i, j: (0, 0))
    pos_blk = pl.BlockSpec((c, C_WIDTH), lambda i, j: (j, 0))
    st_blk = pl.BlockSpec((bt, C_HEADS, HEAD_DIM, HEAD_DIM), lambda i, j: (i, 0, 0, 0))
    return pl.pallas_call(
        functools.partial(_ret_kernel, bt=bt, c=c),
        grid=(b // bt, l // c),
        in_specs=[blk(cols[0]), blk(cols[1]), blk(cols[2]), blk(cols[3]), pos_blk, pos_blk,
                  const2(qd), const2(kd), pl.BlockSpec(dmat.shape, lambda i, j: (0, 0, 0)),
                  const2(sd), const2(bd), const2(gn_gain), const2(gn_bias), st_blk],
        out_specs=[pl.BlockSpec((bt, c, C_WIDTH), lambda i, j: (i, j, 0)), st_blk],
        out_shape=[jax.ShapeDtypeStruct((b, l, C_WIDTH), BF16),
                   jax.ShapeDtypeStruct((b, C_HEADS, HEAD_DIM, HEAD_DIM), F32)],
        scratch_shapes=[pltpu.VMEM((bt, C_WIDTH, C_WIDTH), F32)],
        compiler_params=_cparams(("arbitrary", "arbitrary")),
        name="retention",
    )(h3, h3, h3, h3, cosf, sins, qd, kd, dmat, sd, bd, gn_gain, gn_bias, s0)


def _sortable(score):
    bits = pltpu.bitcast(score, I32)
    return bits ^ ((bits >> 31) & 0x7FFFFFFF)


def _dsa_kernel(iq_ref, iw_ref, ik_ref, q_ref, k_ref, vt_ref, tt_ref, o_ref,
                sc_scr, hi_scr, lo_scr, s_scr, acc_scr, j_scr, *, topk):
    qb = pl.program_id(1)
    ntile = (qb + KEY_TILE // Q_BLOCK) // (KEY_TILE // Q_BLOCK)
    row = lax.broadcasted_iota(I32, (KEY_TILE, Q_BLOCK), 0)
    col = lax.broadcasted_iota(I32, (KEY_TILE, Q_BLOCK), 1)
    qpos = qb * Q_BLOCK + col
    head_rows = lambda ref, h: ref[0, h * HEAD_DIM:(h + 1) * HEAD_DIM, :]
    iq_all = jnp.concatenate([head_rows(iq_ref, h) for h in range(IDX_HEADS)], axis=1)
    iw = iw_ref[0]
    int_min = jnp.int32(INT_MIN)
    score_scale = IDX_HEADS ** -0.5

    def tile_rows(j):
        return pl.ds(pl.multiple_of(j * KEY_TILE, KEY_TILE), KEY_TILE)

    def tree_sum(parts):
        while len(parts) > 1:
            parts = [a + b for a, b in zip(parts[::2], parts[1::2])]
        return parts[0]

    def score_tile(j, carry):
        s = _dot(ik_ref[0, tile_rows(j), :], iq_all)
        terms = [jnp.maximum(s[:, h * Q_BLOCK:(h + 1) * Q_BLOCK], 0.0) * iw[h:h + 1, :] for h in range(IDX_HEADS)]
        acc = terms[0]
        for term in terms[1:]:
            acc = acc + term
        score = (acc * (IDX_DIM ** -0.5)) * score_scale
        kint = jnp.where(j * KEY_TILE + row <= qpos, _sortable(score), int_min)
        sc_scr[tile_rows(j), :] = kint
        hi_scr[tile_rows(j), :] = (kint >> 16).astype(I16)
        return carry

    nstep = (ntile + PHASE_UNROLL - 1) // PHASE_UNROLL

    def unrolled(tile_fn):
        def step(js, carry):
            for u in range(PHASE_UNROLL):
                carry = tile_fn(js * PHASE_UNROLL + u, carry)
            return carry
        return step

    lax.fori_loop(0, nstep, unrolled(score_tile), 0)

    def count(pred):
        def body(j, acc):
            m = jnp.where(pred(sc_scr[tile_rows(j), :], j), 1.0, 0.0)
            return acc + tree_sum([m[i * 8:(i + 1) * 8] for i in range(KEY_TILE // 8)])
        acc = lax.fori_loop(0, ntile, body, jnp.zeros((8, Q_BLOCK), F32))
        return jnp.sum(acc, axis=0, keepdims=True)

    half_min = -(1 << 15)
    ones16, zeros16 = jnp.ones((16, Q_BLOCK), I16), jnp.zeros((16, Q_BLOCK), I16)

    def count16(ref, cand, strict=False):
        cand16 = jnp.broadcast_to(cand, (16, Q_BLOCK)).astype(I16)

        def tile(j, acc):
            x = ref[tile_rows(j), :]
            hits = []
            for i in range(KEY_TILE // 16):
                xi = x[i * 16:(i + 1) * 16]
                hits.append(jnp.where((xi > cand16) if strict else (xi >= cand16), ones16, zeros16))
            return acc + tree_sum(hits)

        acc = lax.fori_loop(0, nstep, unrolled(tile), zeros16)
        return jnp.sum(acc.astype(F32), axis=0, keepdims=True)

    def high_bit(i, ans):
        cand = ans | lax.shift_left(jnp.int32(1), 15 - i)
        return jnp.where(count16(hi_scr, cand + half_min) >= topk, cand, ans)

    t_hi = lax.fori_loop(0, 16, high_bit, jnp.zeros((1, Q_BLOCK), I32)) + half_min
    n_above = count16(hi_scr, t_hi, strict=True)

    def low_halves(j, carry):
        t = sc_scr[tile_rows(j), :]
        lo_scr[tile_rows(j), :] = jnp.where((t >> 16) == t_hi, (t & 0xFFFF) + half_min, half_min).astype(I16)
        return carry

    lax.fori_loop(0, nstep, unrolled(low_halves), 0)

    def low_bit(i, ans):
        cand = ans | lax.shift_left(jnp.int32(1), 15 - i)
        return jnp.where(n_above + count16(lo_scr, cand + half_min) >= topk, cand, ans)

    t_lo = lax.fori_loop(0, 16, low_bit, jnp.zeros((1, Q_BLOCK), I32))
    thr = lax.shift_left(t_hi, 16) | t_lo
    need = topk - count(lambda t, j: t > thr)
    n_ge = count(lambda t, j: t >= thr)

    j_scr[...] = jnp.full(j_scr.shape, 1 << 20, I32)
    has_tie = jnp.max(jnp.where((n_ge > topk) & (thr != int_min), 1.0, 0.0)) > 0.0

    @pl.when(has_tie)
    def _():
        def jbit(i, jc):
            cand = jc | lax.shift_left(jnp.int32(1), 12 - i)
            cnt = count(lambda t, j: (t == thr) & (j * KEY_TILE + row < cand))
            return jnp.where(cnt <= need, cand, jc)
        jc = lax.fori_loop(0, 13, jbit, jnp.zeros((1, Q_BLOCK), I32))
        j_scr[...] = jnp.broadcast_to(jc, j_scr.shape)

    j_cut = jnp.where(thr == int_min, 0, j_scr[0:1, :])

    q_scaled = [head_rows(q_ref, h) for h in range(B_HEADS)]
    sub_tiles = KEY_TILE // Q_BLOCK
    groups = KEY_TILE // 8

    def tree_max(parts):
        while len(parts) > 1:
            parts = [jnp.maximum(a, b) for a, b in zip(parts[::2], parts[1::2])]
        return parts[0]

    def logits_tile(j, m_part):
        t = sc_scr[tile_rows(j), :]
        kidx = j * KEY_TILE + row
        madd = jnp.where(t > thr, 0.0, jnp.where(t == thr, jnp.where(kidx < j_cut, 0.0, MASK_NEG), MASK_NEG))
        offs = [jnp.clip(qb - (j * sub_tiles + i), 0, 2) for i in range(sub_tiles)]
        new = []
        for h in range(B_HEADS):
            bias = jnp.concatenate([tt_ref[h, off] for off in offs], axis=0)
            s = _dot(k_ref[0, h, tile_rows(j), :], q_scaled[h]) + bias + madd
            s_scr[h, tile_rows(j), :] = s
            new.append(jnp.maximum(m_part[h], tree_max([s[i * 8:(i + 1) * 8] for i in range(groups)])))
        return jnp.stack(new)

    m_part = lax.fori_loop(0, nstep, unrolled(logits_tile), jnp.full((B_HEADS, 8, Q_BLOCK), MASK_NEG, F32))
    m_rows = [jnp.max(m_part[h], axis=0, keepdims=True) for h in range(B_HEADS)]

    acc_scr[...] = jnp.zeros(acc_scr.shape, F32)

    def values_tile(j, l_part):
        new = []
        for h in range(B_HEADS):
            p = jnp.exp2(s_scr[h, tile_rows(j), :] - m_rows[h])
            new.append(l_part[h] + tree_sum([p[i * 8:(i + 1) * 8] for i in range(groups)]))
            acc_scr[h] += _dot(vt_ref[0, h, :, tile_rows(j)], p.astype(BF16))
        return jnp.stack(new)

    l_part = lax.fori_loop(0, nstep, unrolled(values_tile), jnp.zeros((B_HEADS, 8, Q_BLOCK), F32))
    out_t = jnp.concatenate([acc_scr[h] / jnp.sum(l_part[h], axis=0, keepdims=True) for h in range(B_HEADS)], axis=0)
    o_ref[0] = jnp.transpose(out_t).astype(BF16)


def _dsa_prompt(iqt, iwt, ik3, qt, kh, vt, tt, topk):
    b, _, l, _ = kh.shape
    assert l % (KEY_TILE * PHASE_UNROLL) == 0
    qblk = pl.BlockSpec((1, B_WIDTH, Q_BLOCK), lambda i, j: (i, 0, j))
    once = pl.Buffered(1)
    return pl.pallas_call(
        functools.partial(_dsa_kernel, topk=topk),
        grid=(b, l // Q_BLOCK),
        in_specs=[qblk,
                  pl.BlockSpec((1, IDX_HEADS, Q_BLOCK), lambda i, j: (i, 0, j)),
                  pl.BlockSpec((1, l, IDX_DIM), lambda i, j: (i, 0, 0), pipeline_mode=once),
                  qblk,
                  pl.BlockSpec((1, B_HEADS, l, HEAD_DIM), lambda i, j: (i, 0, 0, 0), pipeline_mode=once),
                  pl.BlockSpec((1, B_HEADS, HEAD_DIM, l), lambda i, j: (i, 0, 0, 0), pipeline_mode=once),
                  pl.BlockSpec(tt.shape, lambda i, j: (0, 0, 0, 0), pipeline_mode=once)],
        out_specs=pl.BlockSpec((1, Q_BLOCK, B_WIDTH), lambda i, j: (i, j, 0)),
        out_shape=jax.ShapeDtypeStruct((b, l, B_WIDTH), BF16),
        scratch_shapes=[pltpu.VMEM((l, Q_BLOCK), I32),
                        pltpu.VMEM((l, Q_BLOCK), I16),
                        pltpu.VMEM((l, Q_BLOCK), I16),
                        pltpu.VMEM((B_HEADS, l, Q_BLOCK), F32),
                        pltpu.VMEM((B_HEADS, HEAD_DIM, Q_BLOCK), F32),
                        pltpu.VMEM((8, Q_BLOCK), I32)],
        compiler_params=_cparams(("arbitrary", "arbitrary"), 56),
        name="dsa_prompt",
    )(iqt, iwt, ik3, qt, kh, vt, tt)


SCORE_BATCH = 2


def _sidx_kernel(pt_ref, iq_ref, iw_ref, *rest, npages):
    pages, ikn_ref, o_ref = rest[:SCORE_BATCH * npages], rest[SCORE_BATCH * npages], rest[SCORE_BATCH * npages + 1]
    nq = iq_ref.shape[1] // IDX_HEADS
    for g in range(SCORE_BATCH):
        iq, iw = iq_ref[g], iw_ref[g]
        for j in range(npages + 1):
            keys_t = (pages[g * npages + j][0, 0] if j < npages else ikn_ref[g]).astype(BF16)
            r = jnp.maximum(_dot(iq, keys_t), 0.0) * iw
            acc = r[0:nq]
            for h in range(1, IDX_HEADS):
                acc = acc + r[h * nq:(h + 1) * nq]
            o_ref[g, :, j * PAGE_SIZE:(j + 1) * PAGE_SIZE] = (acc * (IDX_DIM ** -0.5)) * (IDX_HEADS ** -0.5)


def _sample_scores(pt_flat, iqs, iws, cache_ik, layer, ikn, npages):
    bd, hq, _ = iqs.shape
    nq = hq // IDX_HEADS
    width = (npages + 1) * PAGE_SIZE
    assert bd % SCORE_BATCH == 0
    page_spec = lambda gp: pl.BlockSpec((1, 1, IDX_DIM, PAGE_SIZE),
                                        lambda i, pt, gp=gp: (layer, pt[i * SCORE_BATCH * npages + gp], 0, 0))
    per_b = lambda a: pl.BlockSpec((SCORE_BATCH,) + a.shape[1:], lambda i, pt: (i, 0, 0))
    n_ops = SCORE_BATCH * npages
    return pl.pallas_call(
        functools.partial(_sidx_kernel, npages=npages),
        grid_spec=pltpu.PrefetchScalarGridSpec(
            num_scalar_prefetch=1, grid=(bd // SCORE_BATCH,),
            in_specs=[per_b(iqs), per_b(iws)] + [page_spec(gp) for gp in range(n_ops)] + [per_b(ikn)],
            out_specs=pl.BlockSpec((SCORE_BATCH, nq, width), lambda i, pt: (i, 0, 0))),
        out_shape=jax.ShapeDtypeStruct((bd, nq, width), F32),
        compiler_params=_cparams(("arbitrary",)),
        name="sample_scores",
    )(pt_flat, iqs, iws, *([cache_ik] * n_ops), ikn)


def _sthr_kernel(s_ref, o_ref, k_scr, j_scr, *, topk, past, nq):
    tr, width = s_ref.shape
    colw = lax.broadcasted_iota(I32, (tr, width), 1)
    rowq = lax.broadcasted_iota(I32, (tr, width), 0) % nq
    vis = colw <= past + rowq
    int_min = jnp.int32(INT_MIN)
    k_scr[...] = jnp.where(vis, _sortable(s_ref[...]), int_min)

    def count(pred):
        return jnp.sum(jnp.where(pred(k_scr[...]), 1.0, 0.0), axis=1, keepdims=True)

    def bit_body(i, ans_u):
        cand_u = ans_u | lax.shift_left(jnp.int32(1), 31 - i)
        cand_s = cand_u ^ int_min
        return jnp.where(count(lambda t: t >= cand_s) >= topk, cand_u, ans_u)

    thr = lax.fori_loop(0, 32, bit_body, jnp.zeros((tr, 1), I32)) ^ int_min
    need = topk - count(lambda t: t > thr)
    n_ge = count(lambda t: t >= thr)

    j_scr[...] = jnp.full(j_scr.shape, 1 << 20, I32)
    has_tie = jnp.max(jnp.where((n_ge > topk) & (thr != int_min), 1.0, 0.0)) > 0.0

    @pl.when(has_tie)
    def _():
        def jbit(i, jc):
            cand = jc | lax.shift_left(jnp.int32(1), 12 - i)
            cnt = count(lambda t: (t == thr) & (colw < cand))
            return jnp.where(cnt <= need, cand, jc)
        jc = lax.fori_loop(0, 13, jbit, jnp.zeros((tr, 1), I32))
        j_scr[...] = jnp.broadcast_to(jc, j_scr.shape)

    j_cut = j_scr[:, 0:1]
    t = k_scr[...]
    sel = ((t > thr) | ((t == thr) & (colw < j_cut))) & vis
    o_ref[...] = jnp.where(sel, 0.0, MASK_NEG)


def _sample_mask(scores2, topk, past, nq):
    r, width = scores2.shape
    tr = min(r, 128)
    return pl.pallas_call(
        functools.partial(_sthr_kernel, topk=topk, past=past, nq=nq),
        grid=(r // tr,),
        in_specs=[pl.BlockSpec((tr, width), lambda i: (i, 0))],
        out_specs=pl.BlockSpec((tr, width), lambda i: (i, 0)),
        out_shape=jax.ShapeDtypeStruct((r, width), F32),
        scratch_shapes=[pltpu.VMEM((tr, width), I32), pltpu.VMEM((tr, LANES), I32)],
        compiler_params=_cparams(("arbitrary",)),
        name="sample_mask",
    )(scores2)


def _sattn_kernel(pt_ref, q_ref, m_ref, bfar_ref, bnear_ref, *rest, npages, nq):
    kpages, vpages = rest[:npages], rest[npages:2 * npages]
    kn_ref, vn_ref, o_ref, s_scr = rest[2 * npages:2 * npages + 4]
    q = q_ref[0]
    heads = lambda m: jnp.concatenate([m] * B_HEADS, axis=0)
    scale = HEAD_DIM ** -0.5
    mx = jnp.full((B_HEADS * nq, PAGE_SIZE), MASK_NEG, F32)
    for j in range(npages):
        cols = slice(j * PAGE_SIZE, (j + 1) * PAGE_SIZE)
        bias = bfar_ref[...] if j < npages - 1 else bnear_ref[:, :PAGE_SIZE]
        s = _dot(q, kpages[j][0, 0].astype(BF16)) * scale + bias + heads(m_ref[0, :, cols])
        s_scr[:, cols] = s
        mx = jnp.maximum(mx, s)
    t_new = kn_ref.shape[2]
    new = slice(npages * PAGE_SIZE, npages * PAGE_SIZE + t_new)
    s_new = (_dot(q, kn_ref[0].astype(BF16)) * scale + bnear_ref[:, PAGE_SIZE:PAGE_SIZE + t_new]
             + heads(m_ref[0, :, new]))
    mrow = jnp.maximum(jnp.max(mx, axis=1, keepdims=True), jnp.max(s_new, axis=1, keepdims=True))
    lsum = jnp.zeros((B_HEADS * nq, PAGE_SIZE), F32)
    acc = jnp.zeros((B_HEADS * nq, B_WIDTH), F32)
    for j in range(npages):
        p = jnp.exp(s_scr[:, j * PAGE_SIZE:(j + 1) * PAGE_SIZE] - mrow)
        lsum = lsum + p
        acc = acc + _nt_dot(p.astype(BF16), vpages[j][0, 0].astype(BF16))
    p_new = jnp.exp(s_new - mrow)
    acc = acc + _nt_dot(p_new.astype(BF16), vn_ref[0].astype(BF16))
    acc = acc / (jnp.sum(lsum, axis=1, keepdims=True) + jnp.sum(p_new, axis=1, keepdims=True))
    head = lax.broadcasted_iota(I32, (nq, B_WIDTH), 1) // HEAD_DIM
    out = jnp.zeros((nq, B_WIDTH), F32)
    for h in range(B_HEADS):
        out = jnp.where(head == h, acc[h * nq:(h + 1) * nq], out)
    o_ref[0] = out.astype(BF16)


def _sample_attend(pt_flat, qbd, madd3, bfar, bnear, cache_k, cache_v, layer, kn, vn, npages):
    bd, hq, _ = qbd.shape
    nq = hq // B_HEADS
    width = (npages + 1) * PAGE_SIZE
    page_spec = lambda p: pl.BlockSpec((1, 1, B_WIDTH, PAGE_SIZE),
                                       lambda i, pt, p=p: (layer, pt[i * npages + p], 0, 0))
    per_b = lambda a: pl.BlockSpec((1,) + a.shape[1:], lambda i, pt: (i, 0, 0))
    const2 = lambda a: pl.BlockSpec(a.shape, lambda i, pt: (0, 0))
    pages = [page_spec(p) for p in range(npages)]
    return pl.pallas_call(
        functools.partial(_sattn_kernel, npages=npages, nq=nq),
        grid_spec=pltpu.PrefetchScalarGridSpec(
            num_scalar_prefetch=1, grid=(bd,),
            in_specs=[per_b(qbd), per_b(madd3), const2(bfar), const2(bnear)] + pages + pages + [per_b(kn), per_b(vn)],
            out_specs=pl.BlockSpec((1, nq, B_WIDTH), lambda i, pt: (i, 0, 0)),
            scratch_shapes=[pltpu.VMEM((hq, width), F32)]),
        out_shape=jax.ShapeDtypeStruct((bd, nq, B_WIDTH), BF16),
        compiler_params=_cparams(("arbitrary",), 56),
        name="sample_attend",
    )(pt_flat, qbd, madd3, bfar, bnear, *([cache_k] * npages), *([cache_v] * npages), kn, vn)


def _outproj_kernel(a_ref, b_ref, c_ref, wa_ref, wb_ref, wc_ref, x_ref, g_ref, bb_ref, o_ref):
    mix = _dot(a_ref[...], wa_ref[...]) + _dot(b_ref[...], wb_ref[...]) + _dot(c_ref[...], wc_ref[...])
    o_ref[...] = _layer_norm_rows(ALPHA * x_ref[...] + mix, g_ref[...], bb_ref[...])


def _outproj_ln(a, b, c, wa, wb, wc, x2, gain, bias):
    n = x2.shape[0]
    tm = 512
    rows = lambda a_: pl.BlockSpec((tm, a_.shape[1]), lambda i: (i, 0))
    full2 = lambda a_: pl.BlockSpec(a_.shape, lambda i: (0, 0))
    return pl.pallas_call(
        _outproj_kernel,
        grid=(n // tm,),
        in_specs=[rows(a), rows(b), rows(c), full2(wa), full2(wb), full2(wc), rows(x2), full2(gain), full2(bias)],
        out_specs=pl.BlockSpec((tm, D_MODEL), lambda i: (i, 0)),
        out_shape=jax.ShapeDtypeStruct((n, D_MODEL), F32),
        compiler_params=_cparams(("arbitrary",), 40),
        name="outproj_ln1",
    )(a, b, c, wa, wb, wc, x2, gain, bias)


def _router_kernel(x_ref, wh_ref, wl_ref, b_ref, o_ref):
    x = x_ref[...]
    xh = x.astype(BF16)
    xl = (x - xh.astype(F32)).astype(BF16)
    logits = _dot(xh, wh_ref[...]) + _dot(xh, wl_ref[...]) + _dot(xl, wh_ref[...]) + b_ref[...]
    lane = lax.broadcasted_iota(I32, logits.shape, 1).astype(F32)
    neg = -jnp.inf
    first_lane = lambda m: jnp.min(jnp.where(m, lane, float(LANES)), axis=1, keepdims=True)
    is_g = (lane >= N_EXPERTS) & (lane < N_EXPERTS + N_GROUPS)
    gmax = jnp.max(jnp.where(is_g, logits, neg), axis=1, keepdims=True)
    g_sel = first_lane(is_g & (logits == gmax)) - N_EXPERTS
    p_group = 1.0 / jnp.sum(jnp.where(is_g, jnp.exp(logits - gmax), 0.0), axis=1, keepdims=True)
    in_g = jnp.floor(lane * (1.0 / EXPERTS_PER_GROUP)) == g_sel
    ev = jnp.where(in_g, logits, neg)
    v1 = jnp.max(ev, axis=1, keepdims=True)
    i1 = first_lane(in_g & (logits == v1))
    ev2 = jnp.where(lane == i1, neg, ev)
    v2 = jnp.max(ev2, axis=1, keepdims=True)
    i2 = first_lane(in_g & (lane != i1) & (logits == v2))
    e2 = jnp.exp(v2 - v1)
    den = 1.0 + e2
    g1 = (1.0 / den) * p_group
    g2 = (e2 / den) * p_group
    o_ref[...] = jnp.where(lane == i1, g1, jnp.where(lane == i2, g2, 0.0))


def _router(x2, wh, wl, bias):
    n = x2.shape[0]
    tm = 512
    full2 = lambda a_: pl.BlockSpec(a_.shape, lambda i: (0, 0))
    return pl.pallas_call(
        _router_kernel,
        grid=(n // tm,),
        in_specs=[pl.BlockSpec((tm, D_MODEL), lambda i: (i, 0)), full2(wh), full2(wl), full2(bias)],
        out_specs=pl.BlockSpec((tm, LANES), lambda i: (i, 0)),
        out_shape=jax.ShapeDtypeStruct((n, LANES), F32),
        compiler_params=_cparams(("arbitrary",)),
        name="router",
    )(x2, wh, wl, bias)


MOE_EXPERTS_PER_STEP = 8


def _moe_kernel(x_ref, cmb_ref, wg_ref, wu_ref, wd_ref, g_ref, b_ref, o_ref, xb_scr, acc_scr):
    e = pl.program_id(1)

    @pl.when(e == 0)
    def _():
        xb_scr[...] = x_ref[...].astype(BF16)
        acc_scr[...] = jnp.zeros(acc_scr.shape, F32)

    cmb = cmb_ref[...]
    lane = lax.broadcasted_iota(I32, cmb.shape, 1)
    xb = xb_scr[...]
    part = None
    for u in range(MOE_EXPERTS_PER_STEP):
        ce = jnp.sum(jnp.where(lane == e * MOE_EXPERTS_PER_STEP + u, cmb, 0.0), axis=1, keepdims=True)
        hid = _silu(_dot(xb, wg_ref[u])) * _dot(xb, wu_ref[u]) * ce
        down = _dot(hid.astype(BF16), wd_ref[u])
        part = down if part is None else part + down
    acc_scr[...] += part

    @pl.when(e == pl.num_programs(1) - 1)
    def _():
        o_ref[...] = _layer_norm_rows(ALPHA * x_ref[...] + acc_scr[...], g_ref[...], b_ref[...])


def _moe_ln(x2, cmb, wg, wu, wd, gain, bias):
    n = x2.shape[0]
    tm = 1024 if n > 1024 else 512
    full2 = lambda a_: pl.BlockSpec(a_.shape, lambda i, e: (0, 0))
    up_blk = pl.BlockSpec((MOE_EXPERTS_PER_STEP, D_MODEL, EXPERT_FF), lambda i, e: (e, 0, 0))
    return pl.pallas_call(
        _moe_kernel,
        grid=(n // tm, N_EXPERTS // MOE_EXPERTS_PER_STEP),
        in_specs=[pl.BlockSpec((tm, D_MODEL), lambda i, e: (i, 0), pipeline_mode=pl.Buffered(1)),
                  pl.BlockSpec((tm, LANES), lambda i, e: (i, 0)),
                  up_blk, up_blk,
                  pl.BlockSpec((MOE_EXPERTS_PER_STEP, EXPERT_FF, D_MODEL), lambda i, e: (e, 0, 0)),
                  full2(gain), full2(bias)],
        out_specs=pl.BlockSpec((tm, D_MODEL), lambda i, e: (i, 0)),
        out_shape=jax.ShapeDtypeStruct((n, D_MODEL), F32),
        scratch_shapes=[pltpu.VMEM((tm, D_MODEL), BF16), pltpu.VMEM((tm, D_MODEL), F32)],
        compiler_params=_cparams(("arbitrary", "arbitrary"), 56),
        name="moe_ln2",
    )(x2, cmb, wg, wu, wd, gain, bias)


def _layer_weights(l, w_in, w_out, gv_gain, gv_bias, ws, bs, ret_gain, ret_bias, ln1_g, ln1_b, ln2_g, ln2_b,
                   rg_w, rg_b, re_w, re_b, e_gate, e_up, e_down):
    w = w_in[l]
    z = lambda k: jnp.zeros((D_MODEL, k), w.dtype)
    ik_end = 2560 + IDX_DIM
    iw_end = ik_end + IDX_HEADS
    w_pad = jnp.concatenate([w[:, :ik_end], z(COL_IW - COL_IK - IDX_DIM), w[:, ik_end:iw_end],
                             z(COL_CQ - COL_IW - IDX_HEADS), w[:, iw_end:]], axis=1).astype(BF16)
    wo = w_out[l].astype(BF16)
    wr = jnp.concatenate([re_w[l], rg_w[l], jnp.zeros((D_MODEL, LANES - N_EXPERTS - N_GROUPS), F32)], axis=1)
    wrh = wr.astype(BF16)
    wrl = (wr - wrh.astype(F32)).astype(BF16)
    br = jnp.concatenate([re_b[l], rg_b[l], jnp.zeros((LANES - N_EXPERTS - N_GROUPS,), F32)])[None, :]
    row = lambda a: a[l].reshape(1, -1)
    return dict(
        w_pad=w_pad, wa=wo[:A_WIDTH], wb=wo[A_WIDTH:A_WIDTH + B_WIDTH], wc=wo[A_WIDTH + B_WIDTH:],
        gv_gain=row(gv_gain), gv_bias=row(gv_bias), ws=ws[l], bs=bs[l],
        ret_gain=row(ret_gain), ret_bias=row(ret_bias),
        ln1_g=row(ln1_g), ln1_b=row(ln1_b), ln2_g=row(ln2_g), ln2_b=row(ln2_b),
        wrh=wrh, wrl=wrl, br=br,
        wg=e_gate[l].astype(BF16), wu=e_up[l].astype(BF16), wd=e_down[l].astype(BF16))


def _channel_mix(x2, a_out, b_out, c_out, lw):
    x1 = _outproj_ln(a_out, b_out, c_out, lw["wa"], lw["wb"], lw["wc"], x2, lw["ln1_g"], lw["ln1_b"])
    cmb = _router(x1, lw["wrh"], lw["wrl"], lw["br"])
    return _moe_ln(x1, cmb, lw["wg"], lw["wu"], lw["wd"], lw["ln2_g"], lw["ln2_b"])


def _heads(t2, b, l, nh):
    return t2.reshape(b, l, nh, t2.shape[-1] // nh)


def _prompt_layer(x3, lw, tt, cosf, sins):
    b, l, _ = x3.shape
    x2 = x3.reshape(b * l, D_MODEL)
    ha, kt, vt, vtb, ikt, kh, qt, iqt, ikb, iwt = _proj_prompt(x3, lw["w_pad"])
    bsb = jnp.repeat(lw["bs"].T, HEAD_DIM, axis=1)
    a_out, a_vn = _gmlp(ha, PCOL_AU, PCOL_AV, lw["gv_gain"], lw["gv_bias"], lw["ws"], bsb)
    topk = min(TOPK_MAX, l // 4)
    b_out = _dsa_prompt(iqt, iwt, ikb, qt, kh, vtb.reshape(b, B_HEADS, HEAD_DIM, l), tt, topk)
    s0e = jnp.zeros((b, C_HEADS, HEAD_DIM, HEAD_DIM), F32)
    c_out, s_fin = _retention(ha.reshape(b, l, P_WIDTH), (PCOL_CQ, PCOL_CK, PCOL_CV, PCOL_CG), cosf, sins,
                              lw["ret_gain"], lw["ret_bias"], s0e, RET_CHUNK, 1)
    y = _channel_mix(x2, a_out, b_out.reshape(b * l, B_WIDTH), c_out.reshape(b * l, C_WIDTH), lw)
    last = ((l - 1) // CHUNK) * CHUNK
    to_rows = lambda t: jnp.transpose(t.reshape(b, B_HEADS, HEAD_DIM, l), (0, 3, 1, 2))
    state = (to_rows(kt), to_rows(vt), jnp.transpose(ikt, (0, 2, 1)), s_fin,
             a_vn.reshape(b, l, A_WIDTH)[:, last:])
    return y.reshape(b, l, D_MODEL), state


def _sample_layer(x3, lw, layer, cache_k, cache_v, cache_ik, state_l, pt_flat, npages, bfar, bnear, cosf, sins):
    bd, t, _ = x3.shape
    n = bd * t
    past = npages * PAGE_SIZE
    x2 = x3.reshape(n, D_MODEL)
    h2 = _proj(x2, lw["w_pad"])
    col = lambda c0, wdt: h2[:, c0:c0 + wdt]
    rep = CHUNK // t
    eye = jnp.eye(rep, dtype=F32)
    ws_t = lw["ws"][:, :t, :t]
    ws_bd = (eye[None, :, None, :, None] * ws_t[:, None, :, None, :]).reshape(A_GROUPS, CHUNK, CHUNK)
    bsb = jnp.repeat(jnp.tile(lw["bs"][:, :t], (1, rep)).T, HEAD_DIM, axis=1)
    a_out, a_vn = _gmlp(h2, COL_AU, COL_AV, lw["gv_gain"], lw["gv_bias"], ws_bd, bsb)
    k4 = _heads(col(COL_K, B_WIDTH), bd, t, B_HEADS)
    v4 = _heads(col(COL_V, B_WIDTH), bd, t, B_HEADS)
    ik3 = col(COL_IK, IDX_DIM).reshape(bd, t, IDX_DIM)
    feat_major = lambda a3: jnp.pad(jnp.transpose(a3, (0, 2, 1)), ((0, 0), (0, 0), (0, PAGE_SIZE - t)))
    iq4 = _heads(col(COL_IQ, IDX_HEADS * IDX_DIM), bd, t, IDX_HEADS)
    iqs = jnp.transpose(iq4, (0, 2, 1, 3)).reshape(bd, IDX_HEADS * t, IDX_DIM).astype(BF16)
    iw3 = jnp.transpose(col(COL_IW, IDX_HEADS).reshape(bd, t, IDX_HEADS), (0, 2, 1))
    iws = jnp.broadcast_to(iw3.reshape(bd, IDX_HEADS * t, 1), (bd, IDX_HEADS * t, LANES))
    scores = _sample_scores(pt_flat, iqs, iws, jnp.transpose(cache_ik, (0, 1, 3, 2)), layer, feat_major(ik3), npages)
    width = (npages + 1) * PAGE_SIZE
    topk = min(TOPK_MAX, (past + t) // 4)
    madd = _sample_mask(scores.reshape(n, width), topk, past, t).reshape(bd, t, width)
    q4 = _heads(col(COL_Q, B_WIDTH), bd, t, B_HEADS)
    eye_h = jnp.eye(B_HEADS, dtype=F32)
    qbd = (jnp.transpose(q4, (0, 2, 1, 3))[:, :, :, None, :] * eye_h[None, :, None, :, None])
    qbd = qbd.reshape(bd, B_HEADS * t, B_WIDTH).astype(BF16)
    page_view = lambda c: jnp.transpose(c, (0, 1, 3, 4, 2)).reshape(c.shape[0], c.shape[1], B_WIDTH, PAGE_SIZE)
    new_rows = lambda a4: jnp.transpose(a4.reshape(bd, t, B_WIDTH), (0, 2, 1))
    b_out = _sample_attend(pt_flat, qbd, madd, bfar, bnear, page_view(cache_k), page_view(cache_v), layer,
                           new_rows(k4), new_rows(v4), npages)
    c_out, s_fin = _retention(h2.reshape(bd, t, H_WIDTH), (COL_CQ, COL_CK, COL_CV, COL_CG), cosf, sins,
                              lw["ret_gain"], lw["ret_bias"], state_l, t, 8)
    y = _channel_mix(x2, a_out, b_out.reshape(n, B_WIDTH), c_out.reshape(n, C_WIDTH), lw)
    state = (k4, v4, ik3, s_fin, a_vn.reshape(bd, t, A_WIDTH))
    return y.reshape(bd, t, D_MODEL), state


def _distance_tables(rel_bias, t, past):
    r = jnp.arange(Q_BLOCK, dtype=I32)
    d_prompt = jnp.concatenate([off * Q_BLOCK + r[None, :] - r[:, None] for off in range(3)], axis=0)
    qpos = past + jnp.arange(t, dtype=I32)
    near0 = past - PAGE_SIZE
    d_near = [qpos[:, None] - (near0 + half * PAGE_SIZE + r[None, :]) for half in range(2)]
    d_far = jnp.full((t, LANES), MAX_DISTANCE * 2, I32)
    n0 = 3 * Q_BLOCK
    tables = _bias_tables(jnp.concatenate([d_prompt] + d_near + [d_far], axis=0), rel_bias, n0)
    tt = tables[:, :n0].reshape(B_HEADS, 3, Q_BLOCK, Q_BLOCK)
    bnear = jnp.concatenate([tables[:, n0:n0 + t], tables[:, n0 + t:n0 + 2 * t]], axis=2).reshape(B_HEADS * t, 2 * LANES)
    bfar = tables[:, n0 + 2 * t:n0 + 3 * t].reshape(B_HEADS * t, LANES)
    return tt, bnear, bfar


def kernel(x_prompt, x_sample, cache_k, cache_v, cache_idx_k, state_ret, page_table, w_in, w_out, gmlp_v_gain,
           gmlp_v_bias, gmlp_ws, gmlp_bs, rel_bias, ret_gn_gain, ret_gn_bias, ln1_gain, ln1_bias, ln2_gain, ln2_bias,
           router_group_w, router_group_b, router_expert_w, router_expert_b, expert_w_gate, expert_w_up,
           expert_w_down):
    depth = w_in.shape[0]
    seq = x_prompt.shape[1]
    bd, t, _ = x_sample.shape
    npages = page_table.shape[1]
    past = npages * PAGE_SIZE
    pt_flat = page_table.reshape(-1).astype(I32)
    tt, bnear, bfar = _distance_tables(rel_bias, t, past)
    cos_p, sin_p = _rope_tables(jnp.arange(seq, dtype=I32))
    cos_s, sin_s = _rope_tables(past + jnp.arange(t, dtype=I32))
    xp, xs = x_prompt, x_sample
    st_p, st_s = [], []
    for l in range(depth):
        lw = _layer_weights(l, w_in, w_out, gmlp_v_gain, gmlp_v_bias, gmlp_ws, gmlp_bs, ret_gn_gain, ret_gn_bias,
                            ln1_gain, ln1_bias, ln2_gain, ln2_bias, router_group_w, router_group_b,
                            router_expert_w, router_expert_b, expert_w_gate, expert_w_up, expert_w_down)
        xp, sp = _prompt_layer(xp, lw, tt, cos_p, sin_p)
        xs, ss = _sample_layer(xs, lw, l, cache_k, cache_v, cache_idx_k, state_ret[l], pt_flat, npages,
                               bfar, bnear, cos_s, sin_s)
        st_p.append(sp)
        st_s.append(ss)
    stk = lambda sts, i: jnp.stack([s[i] for s in sts], axis=0)
    return (xp, xs,
            stk(st_p, 0), stk(st_p, 1), stk(st_p, 2), stk(st_p, 3), stk(st_p, 4),
            stk(st_s, 0), stk(st_s, 1), stk(st_s, 2), stk(st_s, 3), stk(st_s, 4))
```

```python
import functools
import math

import jax
import jax.numpy as jnp
from jax import lax
from jax.experimental import pallas as pl
from jax.experimental.pallas import tpu as pltpu

D_MODEL = 1024
HEAD_DIM = 64
A_GROUPS = 4
A_WIDTH = 256
CHUNK = 128
B_HEADS = 8
B_WIDTH = 512
IDX_HEADS = 8
IDX_DIM = 64
TOPK_MAX = 256
Q_BLOCK = 128
KEY_TILE = 256
PHASE_UNROLL = 4
NUM_BUCKETS = 32
MAX_DISTANCE = 128
C_HEADS = 4
C_WIDTH = 256
RET_CHUNK = 128
ROPE_BASE = 10000.0
PAGE_SIZE = 128
N_GROUPS = 4
EXPERTS_PER_GROUP = 8
N_EXPERTS = 32
EXPERT_FF = 256
DEPTH = 2
ALPHA = (2 * DEPTH) ** 0.25
LN_EPS = 1e-5

F32 = jnp.float32
BF16 = jnp.bfloat16
I32 = jnp.int32
I16 = jnp.int16
LANES = 128
MASK_NEG = -1e30
INT_MIN = -2 ** 31

COL_AU, COL_AV, COL_Q, COL_K, COL_V, COL_IQ, COL_IK, COL_IW = 0, 256, 512, 1024, 1536, 2048, 2560, 2688
COL_CQ, COL_CK, COL_CV, COL_CG = 2816, 3072, 3328, 3584
H_WIDTH = 3840


def _cparams(sem, vmem_mb=None):
    kw = dict(dimension_semantics=sem)
    if vmem_mb is not None:
        kw["vmem_limit_bytes"] = vmem_mb << 20
    return pltpu.CompilerParams(**kw)


def _nt_dot(a, b):
    return lax.dot_general(a, b, (((1,), (1,)), ((), ())), preferred_element_type=F32)


def _dot(a, b):
    return jnp.dot(a, b, preferred_element_type=F32)


def _layer_norm_rows(x, gain, bias):
    mu = jnp.mean(x, axis=-1, keepdims=True)
    xc = x - mu
    var = jnp.mean(xc * xc, axis=-1, keepdims=True)
    return xc * lax.rsqrt(var + LN_EPS) * gain + bias


def _silu(x):
    return x * (1.0 / (1.0 + jnp.exp(-x)))


def _proj_kernel(x_ref, w_ref, o_ref):
    o_ref[...] = _dot(x_ref[...].astype(BF16), w_ref[...])


def _proj(x2, w_pad):
    n = x2.shape[0]
    tm = 1024 if n > 1024 else 512
    tn = 768
    return pl.pallas_call(
        _proj_kernel,
        grid=(n // tm, H_WIDTH // tn),
        in_specs=[pl.BlockSpec((tm, D_MODEL), lambda i, j: (i, 0)),
                  pl.BlockSpec((D_MODEL, tn), lambda i, j: (0, j))],
        out_specs=pl.BlockSpec((tm, tn), lambda i, j: (i, j)),
        out_shape=jax.ShapeDtypeStruct((n, H_WIDTH), F32),
        compiler_params=_cparams(("arbitrary", "arbitrary"), 40),
        name="in_proj",
    )(x2, w_pad)


PCOL_AU, PCOL_AV, PCOL_CQ, PCOL_CK, PCOL_CV, PCOL_CG = 0, 256, 512, 768, 1024, 1280
P_WIDTH = 1536


def _proj_prompt_kernel(x_ref, w_ref, ha_ref, kt_ref, vt_ref, vtb_ref, ikt_ref, kh_ref, qt_ref, iqt_ref, ikb_ref,
                        iwt_ref):
    x = x_ref[0].astype(BF16)
    piece = lambda c0, width: _dot(x, w_ref[:, c0:c0 + width])
    ha_ref[:, :COL_Q] = piece(COL_AU, COL_Q)
    ha_ref[:, COL_Q:] = piece(COL_CQ, H_WIDTH - COL_CQ)
    qt_ref[0] = jnp.transpose(piece(COL_Q, B_WIDTH) * (HEAD_DIM ** -0.5 * LOG2E)).astype(BF16)
    iqt_ref[0] = jnp.transpose(piece(COL_IQ, IDX_HEADS * IDX_DIM)).astype(BF16)
    k = piece(COL_K, B_WIDTH)
    for h in range(B_HEADS):
        kh_ref[0, h] = k[:, h * HEAD_DIM:(h + 1) * HEAD_DIM].astype(BF16)
    kt_ref[0] = jnp.transpose(k)
    vt = jnp.transpose(piece(COL_V, B_WIDTH))
    vt_ref[0] = vt
    vtb_ref[0] = vt.astype(BF16)
    ik = piece(COL_IK, LANES)
    ikb_ref[0] = ik[:, :IDX_DIM].astype(BF16)
    ikt_ref[0] = jnp.transpose(ik)[:IDX_DIM]
    iwt_ref[0] = jnp.transpose(piece(COL_IW, LANES))[:IDX_HEADS]


def _proj_prompt(x3, w_pad):
    b, l, _ = x3.shape
    tm = 512
    nt = l // tm
    f32s = lambda *shape: jax.ShapeDtypeStruct(shape, F32)
    bf16s = lambda *shape: jax.ShapeDtypeStruct(shape, BF16)
    feat = lambda rows: pl.BlockSpec((1, rows, tm), lambda i, j: (i, 0, j))
    hmaj = pl.BlockSpec((1, B_HEADS, tm, HEAD_DIM), lambda i, j: (i, 0, j, 0))
    return pl.pallas_call(
        _proj_prompt_kernel,
        grid=(b, nt),
        in_specs=[pl.BlockSpec((1, tm, D_MODEL), lambda i, j: (i, j, 0)),
                  pl.BlockSpec(w_pad.shape, lambda i, j: (0, 0), pipeline_mode=pl.Buffered(1))],
        out_specs=[pl.BlockSpec((tm, P_WIDTH), lambda i, j: (i * nt + j, 0)),
                   feat(B_WIDTH), feat(B_WIDTH), feat(B_WIDTH), feat(IDX_DIM),
                   hmaj, feat(B_WIDTH), feat(IDX_HEADS * IDX_DIM),
                   pl.BlockSpec((1, tm, IDX_DIM), lambda i, j: (i, j, 0)),
                   feat(IDX_HEADS)],
        out_shape=[f32s(b * l, P_WIDTH), f32s(b, B_WIDTH, l), f32s(b, B_WIDTH, l), bf16s(b, B_WIDTH, l),
                   f32s(b, IDX_DIM, l), bf16s(b, B_HEADS, l, HEAD_DIM), bf16s(b, B_WIDTH, l),
                   bf16s(b, IDX_HEADS * IDX_DIM, l), bf16s(b, l, IDX_DIM), f32s(b, IDX_HEADS, l)],
        compiler_params=_cparams(("arbitrary", "arbitrary"), 48),
        name="in_proj_prompt",
    )(x3, w_pad)


LOG2E = math.log2(math.e)


def _bias_kernel(rb_ref, d_ref, o_ref, *, log2_rows):
    n = jnp.maximum(d_ref[...], 0)
    max_exact = NUM_BUCKETS // 2
    nf = jnp.maximum(n, 1).astype(F32)
    large = max_exact + (jnp.log(nf / max_exact) / math.log(MAX_DISTANCE / max_exact)
                         * (NUM_BUCKETS - max_exact)).astype(I32)
    large = jnp.minimum(large, NUM_BUCKETS - 1)
    bucket = jnp.where(n < max_exact, n, large)
    for h in range(B_HEADS):
        acc = jnp.zeros(bucket.shape, F32)
        for bk in range(NUM_BUCKETS):
            acc = jnp.where(bucket == bk, rb_ref[bk * B_HEADS + h], acc)
        o_ref[h, :log2_rows] = acc[:log2_rows] * LOG2E
        o_ref[h, log2_rows:] = acc[log2_rows:]


def _bias_tables(dist, rel_bias, log2_rows):
    r = dist.shape[0]
    return pl.pallas_call(
        functools.partial(_bias_kernel, log2_rows=log2_rows),
        in_specs=[pl.BlockSpec(memory_space=pltpu.SMEM),
                  pl.BlockSpec((r, LANES), lambda: (0, 0))],
        out_specs=pl.BlockSpec((B_HEADS, r, LANES), lambda: (0, 0, 0)),
        out_shape=jax.ShapeDtypeStruct((B_HEADS, r, LANES), F32),
        name="bias_tables",
    )(rel_bias.reshape(-1), dist)


def _gmlp_kernel(u_ref, v_ref, g_ref, b_ref, ws_ref, bsb_ref, o_ref, vn_ref, *, nchunk):
    r = lax.broadcasted_iota(I32, (CHUNK, CHUNK), 0)
    c = lax.broadcasted_iota(I32, (CHUNK, CHUNK), 1)
    grp = lax.broadcasted_iota(I32, (CHUNK, A_WIDTH), 1) // HEAD_DIM
    wts = [jnp.where(r >= c, ws_ref[g], 0.0).astype(BF16) for g in range(A_GROUPS)]
    gain, bias, bsb = g_ref[...], b_ref[...], bsb_ref[...]
    for ci in range(nchunk):
        sl = pl.ds(ci * CHUNK, CHUNK)
        vn = _layer_norm_rows(v_ref[sl, :], gain, bias)
        vn_ref[sl, :] = vn
        vb = vn.astype(BF16)
        mixed = bsb
        for g in range(A_GROUPS):
            mixed = mixed + jnp.where(grp == g, _dot(wts[g], vb), 0.0)
        o_ref[sl, :] = (u_ref[sl, :] * mixed).astype(BF16)


def _gmlp(h2, col_u, col_v, gain, bias, ws, bsb):
    n = h2.shape[0]
    tm = min(n, 1024)
    blk = lambda col: pl.BlockSpec((tm, A_WIDTH), lambda i: (i, col // A_WIDTH))
    full2 = lambda a: pl.BlockSpec(a.shape, lambda i: (0, 0))
    return pl.pallas_call(
        functools.partial(_gmlp_kernel, nchunk=tm // CHUNK),
        grid=(n // tm,),
        in_specs=[blk(col_u), blk(col_v), full2(gain), full2(bias),
                  pl.BlockSpec(ws.shape, lambda i: (0, 0, 0)), full2(bsb)],
        out_specs=[pl.BlockSpec((tm, A_WIDTH), lambda i: (i, 0)),
                   pl.BlockSpec((tm, A_WIDTH), lambda i: (i, 0))],
        out_shape=[jax.ShapeDtypeStruct((n, A_WIDTH), BF16),
                   jax.ShapeDtypeStruct((n, A_WIDTH), F32)],
        compiler_params=_cparams(("arbitrary",)),
        name="gmlp",
    )(h2, h2, gain, bias, ws, bsb)


def _ret_tables(c):
    log_g = jnp.log(1.0 - 2.0 ** (-5.0 - jnp.arange(C_HEADS, dtype=F32)))
    i = jnp.arange(c, dtype=F32)
    diff = i[:, None] - i[None, :]
    dmat = jnp.where(diff >= 0, jnp.exp(log_g[:, None, None] * jnp.maximum(diff, 0.0)), 0.0)
    q_dec = jnp.exp(log_g[:, None] * (i + 1.0))
    k_dec = jnp.exp(log_g[:, None] * (c - 1.0 - i))
    s_dec = jnp.exp(log_g * c)
    qd = jnp.repeat(q_dec.T, HEAD_DIM, axis=1)
    kd = jnp.repeat(k_dec.T, HEAD_DIM, axis=1)
    hid = jnp.arange(C_WIDTH) // HEAD_DIM
    same = hid[:, None] == hid[None, :]
    sd = jnp.where(same, s_dec[hid][:, None], 0.0)
    return dmat, qd, kd, sd, same.astype(F32)


def _rope_tables(pos):
    half = HEAD_DIM // 2
    inv = ROPE_BASE ** (-jnp.arange(half, dtype=F32) / half)
    ang = pos.astype(F32)[:, None] * inv[None, :]
    cos, sin = jnp.cos(ang), jnp.sin(ang)
    cosf = jnp.tile(jnp.concatenate([cos, cos], axis=1), (1, C_HEADS))
    sins = jnp.tile(jnp.concatenate([-sin, sin], axis=1), (1, C_HEADS))
    return cosf, sins


def _ret_kernel(q_ref, k_ref, v_ref, g_ref, cos_ref, sin_ref, qd_ref, kd_ref, dm_ref, sd_ref, bd_ref,
                gg_ref, gb_ref, s0_ref, o_ref, sf_ref, s_scr, *, bt, c):
    ci = pl.program_id(1)

    @pl.when(ci == 0)
    def _():
        for bb in range(bt):
            rows = []
            for h in range(C_HEADS):
                left = [jnp.zeros((HEAD_DIM, h * HEAD_DIM), F32)] if h else []
                right = [jnp.zeros((HEAD_DIM, (C_HEADS - 1 - h) * HEAD_DIM), F32)] if h < C_HEADS - 1 else []
                rows.append(jnp.concatenate(left + [s0_ref[bb, h]] + right, axis=1))
            s_scr[bb] = jnp.concatenate(rows, axis=0)

    lane = lax.broadcasted_iota(I32, (c, C_WIDTH), 1)
    hid = lane // HEAD_DIM
    first = (lane % HEAD_DIM) < (HEAD_DIM // 2)
    cosf, sins = cos_ref[...], sin_ref[...]
    half = HEAD_DIM // 2

    def rot(x):
        partner = jnp.where(first, pltpu.roll(x, C_WIDTH - half, 1), pltpu.roll(x, half, 1))
        return x * cosf + partner * sins

    def seg_mean(x):
        out = jnp.zeros_like(x)
        for h in range(C_HEADS):
            hm = hid == h
            s = jnp.sum(jnp.where(hm, x, 0.0), axis=1, keepdims=True) * (1.0 / HEAD_DIM)
            out = jnp.where(hm, s, out)
        return out

    for bb in range(bt):
        q = rot(q_ref[bb])
        k = rot(k_ref[bb]) * (HEAD_DIM ** -0.5)
        v = v_ref[bb]
        kb, vb = k.astype(BF16), v.astype(BF16)
        s_old = s_scr[bb]
        o = _dot(q.astype(BF16), s_old.astype(BF16)) * qd_ref[...]
        for h in range(C_HEADS):
            hm = hid == h
            att = _nt_dot(jnp.where(hm, q, 0.0).astype(BF16), kb) * dm_ref[h]
            o = o + jnp.where(hm, _dot(att.astype(BF16), vb), 0.0)
        kdt = jnp.transpose(k * kd_ref[...]).astype(BF16)
        s_scr[bb] = s_old * sd_ref[...] + bd_ref[...] * _dot(kdt, vb)
        mu = seg_mean(o)
        oc = o - mu
        var = seg_mean(oc * oc)
        normed = oc * lax.rsqrt(var + LN_EPS) * gg_ref[...] + gb_ref[...]
        o_ref[bb] = (_silu(g_ref[bb]) * normed).astype(BF16)

    @pl.when(ci == pl.num_programs(1) - 1)
    def _():
        for bb in range(bt):
            for h in range(C_HEADS):
                sf_ref[bb, h] = s_scr[bb, h * HEAD_DIM:(h + 1) * HEAD_DIM, h * HEAD_DIM:(h + 1) * HEAD_DIM]


def _retention(h3, cols, cosf, sins, gn_gain, gn_bias, s0, c, bt):
    b, l, _ = h3.shape
    dmat, qd, kd, sd, bd = _ret_tables(c)
    blk = lambda col: pl.BlockSpec((bt, c, C_WIDTH), lambda i, j: (i, j, col // C_WIDTH))
    const2 = lambda a: pl.BlockSpec(a.shape, lambda i, j: (0, 0))
    pos_blk = pl.BlockSpec((c, C_WIDTH), lambda i, j: (j, 0))
    st_blk = pl.BlockSpec((bt, C_HEADS, HEAD_DIM, HEAD_DIM), lambda i, j: (i, 0, 0, 0))
    return pl.pallas_call(
        functools.partial(_ret_kernel, bt=bt, c=c),
        grid=(b // bt, l // c),
        in_specs=[blk(cols[0]), blk(cols[1]), blk(cols[2]), blk(cols[3]), pos_blk, pos_blk,
                  const2(qd), const2(kd), pl.BlockSpec(dmat.shape, lambda i, j: (0, 0, 0)),
                  const2(sd), const2(bd), const2(gn_gain), const2(gn_bias), st_blk],
        out_specs=[pl.BlockSpec((bt, c, C_WIDTH), lambda i, j: (i, j, 0)), st_blk],
        out_shape=[jax.ShapeDtypeStruct((b, l, C_WIDTH), BF16),
                   jax.ShapeDtypeStruct((b, C_HEADS, HEAD_DIM, HEAD_DIM), F32)],
        scratch_shapes=[pltpu.VMEM((bt, C_WIDTH, C_WIDTH), F32)],
        compiler_params=_cparams(("arbitrary", "arbitrary")),
        name="retention",
    )(h3, h3, h3, h3, cosf, sins, qd, kd, dmat, sd, bd, gn_gain, gn_bias, s0)


def _sortable(score):
    bits = pltpu.bitcast(score, I32)
    return bits ^ ((bits >> 31) & 0x7FFFFFFF)


def _dsa_kernel(iq_ref, iw_ref, ik_ref, q_ref, k_ref, vt_ref, tt_ref, o_ref,
                sc_scr, hi_scr, lo_scr, s_scr, acc_scr, j_scr, *, topk):
    qb = pl.program_id(1)
    ntile = (qb + KEY_TILE // Q_BLOCK) // (KEY_TILE // Q_BLOCK)
    row = lax.broadcasted_iota(I32, (KEY_TILE, Q_BLOCK), 0)
    col = lax.broadcasted_iota(I32, (KEY_TILE, Q_BLOCK), 1)
    qpos = qb * Q_BLOCK + col
    head_rows = lambda ref, h: ref[0, h * HEAD_DIM:(h + 1) * HEAD_DIM, :]
    iq_all = jnp.concatenate([head_rows(iq_ref, h) for h in range(IDX_HEADS)], axis=1)
    iw = iw_ref[0]
    int_min = jnp.int32(INT_MIN)
    score_scale = IDX_HEADS ** -0.5

    def tile_rows(j):
        return pl.ds(pl.multiple_of(j * KEY_TILE, KEY_TILE), KEY_TILE)

    def tree_sum(parts):
        while len(parts) > 1:
            parts = [a + b for a, b in zip(parts[::2], parts[1::2])]
        return parts[0]

    def score_tile(j, carry):
        s = _dot(ik_ref[0, tile_rows(j), :], iq_all)
        terms = [jnp.maximum(s[:, h * Q_BLOCK:(h + 1) * Q_BLOCK], 0.0) * iw[h:h + 1, :] for h in range(IDX_HEADS)]
        acc = terms[0]
        for term in terms[1:]:
            acc = acc + term
        score = (acc * (IDX_DIM ** -0.5)) * score_scale
        kint = jnp.where(j * KEY_TILE + row <= qpos, _sortable(score), int_min)
        sc_scr[tile_rows(j), :] = kint
        hi_scr[tile_rows(j), :] = (kint >> 16).astype(I16)
        return carry

    nstep = (ntile + PHASE_UNROLL - 1) // PHASE_UNROLL

    def unrolled(tile_fn):
        def step(js, carry):
            for u in range(PHASE_UNROLL):
                carry = tile_fn(js * PHASE_UNROLL + u, carry)
            return carry
        return step

    lax.fori_loop(0, nstep, unrolled(score_tile), 0)

    def count(pred):
        def body(j, acc):
            m = jnp.where(pred(sc_scr[tile_rows(j), :], j), 1.0, 0.0)
            return acc + tree_sum([m[i * 8:(i + 1) * 8] for i in range(KEY_TILE // 8)])
        acc = lax.fori_loop(0, ntile, body, jnp.zeros((8, Q_BLOCK), F32))
        return jnp.sum(acc, axis=0, keepdims=True)

    half_min = -(1 << 15)
    ones16, zeros16 = jnp.ones((16, Q_BLOCK), I16), jnp.zeros((16, Q_BLOCK), I16)

    def count16(ref, cand, strict=False):
        cand16 = jnp.broadcast_to(cand, (16, Q_BLOCK)).astype(I16)

        def tile(j, acc):
            x = ref[tile_rows(j), :]
            hits = []
            for i in range(KEY_TILE // 16):
                xi = x[i * 16:(i + 1) * 16]
                hits.append(jnp.where((xi > cand16) if strict else (xi >= cand16), ones16, zeros16))
            return acc + tree_sum(hits)

        acc = lax.fori_loop(0, nstep, unrolled(tile), zeros16)
        return jnp.sum(acc.astype(F32), axis=0, keepdims=True)

    def high_bit(i, ans):
        cand = ans | lax.shift_left(jnp.int32(1), 15 - i)
        return jnp.where(count16(hi_scr, cand + half_min) >= topk, cand, ans)

    t_hi = lax.fori_loop(0, 16, high_bit, jnp.zeros((1, Q_BLOCK), I32)) + half_min
    n_above = count16(hi_scr, t_hi, strict=True)

    def low_halves(j, carry):
        t = sc_scr[tile_rows(j), :]
        lo_scr[tile_rows(j), :] = jnp.where((t >> 16) == t_hi, (t & 0xFFFF) + half_min, half_min).astype(I16)
        return carry

    lax.fori_loop(0, nstep, unrolled(low_halves), 0)

    def low_bit(i, ans):
        cand = ans | lax.shift_left(jnp.int32(1), 15 - i)
        return jnp.where(n_above + count16(lo_scr, cand + half_min) >= topk, cand, ans)

    t_lo = lax.fori_loop(0, 16, low_bit, jnp.zeros((1, Q_BLOCK), I32))
    thr = lax.shift_left(t_hi, 16) | t_lo
    need = topk - count(lambda t, j: t > thr)
    n_ge = count(lambda t, j: t >= thr)

    j_scr[...] = jnp.full(j_scr.shape, 1 << 20, I32)
    has_tie = jnp.max(jnp.where((n_ge > topk) & (thr != int_min), 1.0, 0.0)) > 0.0

    @pl.when(has_tie)
    def _():
        def jbit(i, jc):
            cand = jc | lax.shift_left(jnp.int32(1), 12 - i)
            cnt = count(lambda t, j: (t == thr) & (j * KEY_TILE + row < cand))
            return jnp.where(cnt <= need, cand, jc)
        jc = lax.fori_loop(0, 13, jbit, jnp.zeros((1, Q_BLOCK), I32))
        j_scr[...] = jnp.broadcast_to(jc, j_scr.shape)

    j_cut = jnp.where(thr == int_min, 0, j_scr[0:1, :])

    q_scaled = [head_rows(q_ref, h) for h in range(B_HEADS)]
    sub_tiles = KEY_TILE // Q_BLOCK
    groups = KEY_TILE // 8

    def tree_max(parts):
        while len(parts) > 1:
            parts = [jnp.maximum(a, b) for a, b in zip(parts[::2], parts[1::2])]
        return parts[0]

    def logits_tile(j, m_part):
        t = sc_scr[tile_rows(j), :]
        kidx = j * KEY_TILE + row
        madd = jnp.where(t > thr, 0.0, jnp.where(t == thr, jnp.where(kidx < j_cut, 0.0, MASK_NEG), MASK_NEG))
        offs = [jnp.clip(qb - (j * sub_tiles + i), 0, 2) for i in range(sub_tiles)]
        new = []
        for h in range(B_HEADS):
            bias = jnp.concatenate([tt_ref[h, off] for off in offs], axis=0)
            s = _dot(k_ref[0, h, tile_rows(j), :], q_scaled[h]) + bias + madd
            s_scr[h, tile_rows(j), :] = s
            new.append(jnp.maximum(m_part[h], tree_max([s[i * 8:(i + 1) * 8] for i in range(groups)])))
        return jnp.stack(new)

    m_part = lax.fori_loop(0, nstep, unrolled(logits_tile), jnp.full((B_HEADS, 8, Q_BLOCK), MASK_NEG, F32))
    m_rows = [jnp.max(m_part[h], axis=0, keepdims=True) for h in range(B_HEADS)]

    acc_scr[...] = jnp.zeros(acc_scr.shape, F32)

    def values_tile(j, l_part):
        new = []
        for h in range(B_HEADS):
            p = jnp.exp2(s_scr[h, tile_rows(j), :] - m_rows[h])
            new.append(l_part[h] + tree_sum([p[i * 8:(i + 1) * 8] for i in range(groups)]))
            acc_scr[h] += _dot(vt_ref[0, h, :, tile_rows(j)], p.astype(BF16))
        return jnp.stack(new)

    l_part = lax.fori_loop(0, nstep, unrolled(values_tile), jnp.zeros((B_HEADS, 8, Q_BLOCK), F32))
    out_t = jnp.concatenate([acc_scr[h] / jnp.sum(l_part[h], axis=0, keepdims=True) for h in range(B_HEADS)], axis=0)
    o_ref[0] = jnp.transpose(out_t).astype(BF16)


def _dsa_prompt(iqt, iwt, ik3, qt, kh, vt, tt, topk):
    b, _, l, _ = kh.shape
    assert l % (KEY_TILE * PHASE_UNROLL) == 0
    qblk = pl.BlockSpec((1, B_WIDTH, Q_BLOCK), lambda i, j: (i, 0, j))
    once = pl.Buffered(1)
    return pl.pallas_call(
        functools.partial(_dsa_kernel, topk=topk),
        grid=(b, l // Q_BLOCK),
        in_specs=[qblk,
                  pl.BlockSpec((1, IDX_HEADS, Q_BLOCK), lambda i, j: (i, 0, j)),
                  pl.BlockSpec((1, l, IDX_DIM), lambda i, j: (i, 0, 0), pipeline_mode=once),
                  qblk,
                  pl.BlockSpec((1, B_HEADS, l, HEAD_DIM), lambda i, j: (i, 0, 0, 0), pipeline_mode=once),
                  pl.BlockSpec((1, B_HEADS, HEAD_DIM, l), lambda i, j: (i, 0, 0, 0), pipeline_mode=once),
                  pl.BlockSpec(tt.shape, lambda i, j: (0, 0, 0, 0), pipeline_mode=once)],
        out_specs=pl.BlockSpec((1, Q_BLOCK, B_WIDTH), lambda i, j: (i, j, 0)),
        out_shape=jax.ShapeDtypeStruct((b, l, B_WIDTH), BF16),
        scratch_shapes=[pltpu.VMEM((l, Q_BLOCK), I32),
                        pltpu.VMEM((l, Q_BLOCK), I16),
                        pltpu.VMEM((l, Q_BLOCK), I16),
                        pltpu.VMEM((B_HEADS, l, Q_BLOCK), F32),
                        pltpu.VMEM((B_HEADS, HEAD_DIM, Q_BLOCK), F32),
                        pltpu.VMEM((8, Q_BLOCK), I32)],
        compiler_params=_cparams(("arbitrary", "arbitrary"), 56),
        name="dsa_prompt",
    )(iqt, iwt, ik3, qt, kh, vt, tt)


SCORE_BATCH = 2


def _sidx_kernel(pt_ref, iq_ref, iw_ref, *rest, npages):
    pages, ikn_ref, o_ref = rest[:SCORE_BATCH * npages], rest[SCORE_BATCH * npages], rest[SCORE_BATCH * npages + 1]
    nq = iq_ref.shape[1] // IDX_HEADS
    for g in range(SCORE_BATCH):
        iq, iw = iq_ref[g], iw_ref[g]
        for j in range(npages + 1):
            keys_t = (pages[g * npages + j][0, 0] if j < npages else ikn_ref[g]).astype(BF16)
            r = jnp.maximum(_dot(iq, keys_t), 0.0) * iw
            acc = r[0:nq]
            for h in range(1, IDX_HEADS):
                acc = acc + r[h * nq:(h + 1) * nq]
            o_ref[g, :, j * PAGE_SIZE:(j + 1) * PAGE_SIZE] = (acc * (IDX_DIM ** -0.5)) * (IDX_HEADS ** -0.5)


def _sample_scores(pt_flat, iqs, iws, cache_ik, layer, ikn, npages):
    bd, hq, _ = iqs.shape
    nq = hq // IDX_HEADS
    width = (npages + 1) * PAGE_SIZE
    assert bd % SCORE_BATCH == 0
    page_spec = lambda gp: pl.BlockSpec((1, 1, IDX_DIM, PAGE_SIZE),
                                        lambda i, pt, gp=gp: (layer, pt[i * SCORE_BATCH * npages + gp], 0, 0))
    per_b = lambda a: pl.BlockSpec((SCORE_BATCH,) + a.shape[1:], lambda i, pt: (i, 0, 0))
    n_ops = SCORE_BATCH * npages
    return pl.pallas_call(
        functools.partial(_sidx_kernel, npages=npages),
        grid_spec=pltpu.PrefetchScalarGridSpec(
            num_scalar_prefetch=1, grid=(bd // SCORE_BATCH,),
            in_specs=[per_b(iqs), per_b(iws)] + [page_spec(gp) for gp in range(n_ops)] + [per_b(ikn)],
            out_specs=pl.BlockSpec((SCORE_BATCH, nq, width), lambda i, pt: (i, 0, 0))),
        out_shape=jax.ShapeDtypeStruct((bd, nq, width), F32),
        compiler_params=_cparams(("arbitrary",)),
        name="sample_scores",
    )(pt_flat, iqs, iws, *([cache_ik] * n_ops), ikn)


def _sthr_kernel(s_ref, o_ref, k_scr, j_scr, *, topk, past, nq):
    tr, width = s_ref.shape
    colw = lax.broadcasted_iota(I32, (tr, width), 1)
    rowq = lax.broadcasted_iota(I32, (tr, width), 0) % nq
    vis = colw <= past + rowq
    int_min = jnp.int32(INT_MIN)
    k_scr[...] = jnp.where(vis, _sortable(s_ref[...]), int_min)

    def count(pred):
        return jnp.sum(jnp.where(pred(k_scr[...]), 1.0, 0.0), axis=1, keepdims=True)

    def bit_body(i, ans_u):
        cand_u = ans_u | lax.shift_left(jnp.int32(1), 31 - i)
        cand_s = cand_u ^ int_min
        return jnp.where(count(lambda t: t >= cand_s) >= topk, cand_u, ans_u)

    thr = lax.fori_loop(0, 32, bit_body, jnp.zeros((tr, 1), I32)) ^ int_min
    need = topk - count(lambda t: t > thr)
    n_ge = count(lambda t: t >= thr)

    j_scr[...] = jnp.full(j_scr.shape, 1 << 20, I32)
    has_tie = jnp.max(jnp.where((n_ge > topk) & (thr != int_min), 1.0, 0.0)) > 0.0

    @pl.when(has_tie)
    def _():
        def jbit(i, jc):
            cand = jc | lax.shift_left(jnp.int32(1), 12 - i)
            cnt = count(lambda t: (t == thr) & (colw < cand))
            return jnp.where(cnt <= need, cand, jc)
        jc = lax.fori_loop(0, 13, jbit, jnp.zeros((tr, 1), I32))
        j_scr[...] = jnp.broadcast_to(jc, j_scr.shape)

    j_cut = j_scr[:, 0:1]
    t = k_scr[...]
    sel = ((t > thr) | ((t == thr) & (colw < j_cut))) & vis
    o_ref[...] = jnp.where(sel, 0.0, MASK_NEG)


def _sample_mask(scores2, topk, past, nq):
    r, width = scores2.shape
    tr = min(r, 128)
    return pl.pallas_call(
        functools.partial(_sthr_kernel, topk=topk, past=past, nq=nq),
        grid=(r // tr,),
        in_specs=[pl.BlockSpec((tr, width), lambda i: (i, 0))],
        out_specs=pl.BlockSpec((tr, width), lambda i: (i, 0)),
        out_shape=jax.ShapeDtypeStruct((r, width), F32),
        scratch_shapes=[pltpu.VMEM((tr, width), I32), pltpu.VMEM((tr, LANES), I32)],
        compiler_params=_cparams(("arbitrary",)),
        name="sample_mask",
    )(scores2)


def _sattn_kernel(pt_ref, q_ref, m_ref, bfar_ref, bnear_ref, *rest, npages, nq):
    kpages, vpages = rest[:npages], rest[npages:2 * npages]
    kn_ref, vn_ref, o_ref, s_scr = rest[2 * npages:2 * npages + 4]
    q = q_ref[0]
    heads = lambda m: jnp.concatenate([m] * B_HEADS, axis=0)
    scale = HEAD_DIM ** -0.5
    mx = jnp.full((B_HEADS * nq, PAGE_SIZE), MASK_NEG, F32)
    for j in range(npages):
        cols = slice(j * PAGE_SIZE, (j + 1) * PAGE_SIZE)
        bias = bfar_ref[...] if j < npages - 1 else bnear_ref[:, :PAGE_SIZE]
        s = _dot(q, kpages[j][0, 0].astype(BF16)) * scale + bias + heads(m_ref[0, :, cols])
        s_scr[:, cols] = s
        mx = jnp.maximum(mx, s)
    t_new = kn_ref.shape[2]
    new = slice(npages * PAGE_SIZE, npages * PAGE_SIZE + t_new)
    s_new = (_dot(q, kn_ref[0].astype(BF16)) * scale + bnear_ref[:, PAGE_SIZE:PAGE_SIZE + t_new]
             + heads(m_ref[0, :, new]))
    mrow = jnp.maximum(jnp.max(mx, axis=1, keepdims=True), jnp.max(s_new, axis=1, keepdims=True))
    lsum = jnp.zeros((B_HEADS * nq, PAGE_SIZE), F32)
    acc = jnp.zeros((B_HEADS * nq, B_WIDTH), F32)
    for j in range(npages):
        p = jnp.exp(s_scr[:, j * PAGE_SIZE:(j + 1) * PAGE_SIZE] - mrow)
        lsum = lsum + p
        acc = acc + _nt_dot(p.astype(BF16), vpages[j][0, 0].astype(BF16))
    p_new = jnp.exp(s_new - mrow)
    acc = acc + _nt_dot(p_new.astype(BF16), vn_ref[0].astype(BF16))
    acc = acc / (jnp.sum(lsum, axis=1, keepdims=True) + jnp.sum(p_new, axis=1, keepdims=True))
    head = lax.broadcasted_iota(I32, (nq, B_WIDTH), 1) // HEAD_DIM
    out = jnp.zeros((nq, B_WIDTH), F32)
    for h in range(B_HEADS):
        out = jnp.where(head == h, acc[h * nq:(h + 1) * nq], out)
    o_ref[0] = out.astype(BF16)


def _sample_attend(pt_flat, qbd, madd3, bfar, bnear, cache_k, cache_v, layer, kn, vn, npages):
    bd, hq, _ = qbd.shape
    nq = hq // B_HEADS
    width = (npages + 1) * PAGE_SIZE
    page_spec = lambda p: pl.BlockSpec((1, 1, B_WIDTH, PAGE_SIZE),
                                       lambda i, pt, p=p: (layer, pt[i * npages + p], 0, 0))
    per_b = lambda a: pl.BlockSpec((1,) + a.shape[1:], lambda i, pt: (i, 0, 0))
    const2 = lambda a: pl.BlockSpec(a.shape, lambda i, pt: (0, 0))
    pages = [page_spec(p) for p in range(npages)]
    return pl.pallas_call(
        functools.partial(_sattn_kernel, npages=npages, nq=nq),
        grid_spec=pltpu.PrefetchScalarGridSpec(
            num_scalar_prefetch=1, grid=(bd,),
            in_specs=[per_b(qbd), per_b(madd3), const2(bfar), const2(bnear)] + pages + pages + [per_b(kn), per_b(vn)],
            out_specs=pl.BlockSpec((1, nq, B_WIDTH), lambda i, pt: (i, 0, 0)),
            scratch_shapes=[pltpu.VMEM((hq, width), F32)]),
        out_shape=jax.ShapeDtypeStruct((bd, nq, B_WIDTH), BF16),
        compiler_params=_cparams(("arbitrary",), 56),
        name="sample_attend",
    )(pt_flat, qbd, madd3, bfar, bnear, *([cache_k] * npages), *([cache_v] * npages), kn, vn)


def _outproj_kernel(a_ref, b_ref, c_ref, wa_ref, wb_ref, wc_ref, x_ref, g_ref, bb_ref, o_ref):
    mix = _dot(a_ref[...], wa_ref[...]) + _dot(b_ref[...], wb_ref[...]) + _dot(c_ref[...], wc_ref[...])
    o_ref[...] = _layer_norm_rows(ALPHA * x_ref[...] + mix, g_ref[...], bb_ref[...])


def _outproj_ln(a, b, c, wa, wb, wc, x2, gain, bias):
    n = x2.shape[0]
    tm = 512
    rows = lambda a_: pl.BlockSpec((tm, a_.shape[1]), lambda i: (i, 0))
    full2 = lambda a_: pl.BlockSpec(a_.shape, lambda i: (0, 0))
    return pl.pallas_call(
        _outproj_kernel,
        grid=(n // tm,),
        in_specs=[rows(a), rows(b), rows(c), full2(wa), full2(wb), full2(wc), rows(x2), full2(gain), full2(bias)],
        out_specs=pl.BlockSpec((tm, D_MODEL), lambda i: (i, 0)),
        out_shape=jax.ShapeDtypeStruct((n, D_MODEL), F32),
        compiler_params=_cparams(("arbitrary",), 40),
        name="outproj_ln1",
    )(a, b, c, wa, wb, wc, x2, gain, bias)


def _router_kernel(x_ref, wh_ref, wl_ref, b_ref, o_ref):
    x = x_ref[...]
    xh = x.astype(BF16)
    xl = (x - xh.astype(F32)).astype(BF16)
    logits = _dot(xh, wh_ref[...]) + _dot(xh, wl_ref[...]) + _dot(xl, wh_ref[...]) + b_ref[...]
    lane = lax.broadcasted_iota(I32, logits.shape, 1).astype(F32)
    neg = -jnp.inf
    first_lane = lambda m: jnp.min(jnp.where(m, lane, float(LANES)), axis=1, keepdims=True)
    is_g = (lane >= N_EXPERTS) & (lane < N_EXPERTS + N_GROUPS)
    gmax = jnp.max(jnp.where(is_g, logits, neg), axis=1, keepdims=True)
    g_sel = first_lane(is_g & (logits == gmax)) - N_EXPERTS
    p_group = 1.0 / jnp.sum(jnp.where(is_g, jnp.exp(logits - gmax), 0.0), axis=1, keepdims=True)
    in_g = jnp.floor(lane * (1.0 / EXPERTS_PER_GROUP)) == g_sel
    ev = jnp.where(in_g, logits, neg)
    v1 = jnp.max(ev, axis=1, keepdims=True)
    i1 = first_lane(in_g & (logits == v1))
    ev2 = jnp.where(lane == i1, neg, ev)
    v2 = jnp.max(ev2, axis=1, keepdims=True)
    i2 = first_lane(in_g & (lane != i1) & (logits == v2))
    e2 = jnp.exp(v2 - v1)
    den = 1.0 + e2
    g1 = (1.0 / den) * p_group
    g2 = (e2 / den) * p_group
    o_ref[...] = jnp.where(lane == i1, g1, jnp.where(lane == i2, g2, 0.0))


def _router(x2, wh, wl, bias):
    n = x2.shape[0]
    tm = 512
    full2 = lambda a_: pl.BlockSpec(a_.shape, lambda i: (0, 0))
    return pl.pallas_call(
        _router_kernel,
        grid=(n // tm,),
        in_specs=[pl.BlockSpec((tm, D_MODEL), lambda i: (i, 0)), full2(wh), full2(wl), full2(bias)],
        out_specs=pl.BlockSpec((tm, LANES), lambda i: (i, 0)),
        out_shape=jax.ShapeDtypeStruct((n, LANES), F32),
        compiler_params=_cparams(("arbitrary",)),
        name="router",
    )(x2, wh, wl, bias)


MOE_EXPERTS_PER_STEP = 4


def _moe_kernel(x_ref, cmb_ref, wg_ref, wu_ref, wd_ref, g_ref, b_ref, o_ref, xb_scr, acc_scr):
    e = pl.program_id(1)

    @pl.when(e == 0)
    def _():
        xb_scr[...] = x_ref[...].astype(BF16)
        acc_scr[...] = jnp.zeros(acc_scr.shape, F32)

    cmb = cmb_ref[...]
    lane = lax.broadcasted_iota(I32, cmb.shape, 1)
    xb = xb_scr[...]
    part = None
    for u in range(MOE_EXPERTS_PER_STEP):
        ce = jnp.sum(jnp.where(lane == e * MOE_EXPERTS_PER_STEP + u, cmb, 0.0), axis=1, keepdims=True)
        hid = _silu(_dot(xb, wg_ref[u])) * _dot(xb, wu_ref[u]) * ce
        down = _dot(hid.astype(BF16), wd_ref[u])
        part = down if part is None else part + down
    acc_scr[...] += part

    @pl.when(e == pl.num_programs(1) - 1)
    def _():
        o_ref[...] = _layer_norm_rows(ALPHA * x_ref[...] + acc_scr[...], g_ref[...], b_ref[...])


def _moe_ln(x2, cmb, wg, wu, wd, gain, bias):
    n = x2.shape[0]
    tm = 1024 if n > 1024 else 512
    full2 = lambda a_: pl.BlockSpec(a_.shape, lambda i, e: (0, 0))
    up_blk = pl.BlockSpec((MOE_EXPERTS_PER_STEP, D_MODEL, EXPERT_FF), lambda i, e: (e, 0, 0))
    return pl.pallas_call(
        _moe_kernel,
        grid=(n // tm, N_EXPERTS // MOE_EXPERTS_PER_STEP),
        in_specs=[pl.BlockSpec((tm, D_MODEL), lambda i, e: (i, 0)),
                  pl.BlockSpec((tm, LANES), lambda i, e: (i, 0)),
                  up_blk, up_blk,
                  pl.BlockSpec((MOE_EXPERTS_PER_STEP, EXPERT_FF, D_MODEL), lambda i, e: (e, 0, 0)),
                  full2(gain), full2(bias)],
        out_specs=pl.BlockSpec((tm, D_MODEL), lambda i, e: (i, 0)),
        out_shape=jax.ShapeDtypeStruct((n, D_MODEL), F32),
        scratch_shapes=[pltpu.VMEM((tm, D_MODEL), BF16), pltpu.VMEM((tm, D_MODEL), F32)],
        compiler_params=_cparams(("arbitrary", "arbitrary"), 48),
        name="moe_ln2",
    )(x2, cmb, wg, wu, wd, gain, bias)


def _layer_weights(l, w_in, w_out, gv_gain, gv_bias, ws, bs, ret_gain, ret_bias, ln1_g, ln1_b, ln2_g, ln2_b,
                   rg_w, rg_b, re_w, re_b, e_gate, e_up, e_down):
    w = w_in[l]
    z = lambda k: jnp.zeros((D_MODEL, k), w.dtype)
    ik_end = 2560 + IDX_DIM
    iw_end = ik_end + IDX_HEADS
    w_pad = jnp.concatenate([w[:, :ik_end], z(COL_IW - COL_IK - IDX_DIM), w[:, ik_end:iw_end],
                             z(COL_CQ - COL_IW - IDX_HEADS), w[:, iw_end:]], axis=1).astype(BF16)
    wo = w_out[l].astype(BF16)
    wr = jnp.concatenate([re_w[l], rg_w[l], jnp.zeros((D_MODEL, LANES - N_EXPERTS - N_GROUPS), F32)], axis=1)
    wrh = wr.astype(BF16)
    wrl = (wr - wrh.astype(F32)).astype(BF16)
    br = jnp.concatenate([re_b[l], rg_b[l], jnp.zeros((LANES - N_EXPERTS - N_GROUPS,), F32)])[None, :]
    row = lambda a: a[l].reshape(1, -1)
    return dict(
        w_pad=w_pad, wa=wo[:A_WIDTH], wb=wo[A_WIDTH:A_WIDTH + B_WIDTH], wc=wo[A_WIDTH + B_WIDTH:],
        gv_gain=row(gv_gain), gv_bias=row(gv_bias), ws=ws[l], bs=bs[l],
        ret_gain=row(ret_gain), ret_bias=row(ret_bias),
        ln1_g=row(ln1_g), ln1_b=row(ln1_b), ln2_g=row(ln2_g), ln2_b=row(ln2_b),
        wrh=wrh, wrl=wrl, br=br,
        wg=e_gate[l].astype(BF16), wu=e_up[l].astype(BF16), wd=e_down[l].astype(BF16))


def _channel_mix(x2, a_out, b_out, c_out, lw):
    x1 = _outproj_ln(a_out, b_out, c_out, lw["wa"], lw["wb"], lw["wc"], x2, lw["ln1_g"], lw["ln1_b"])
    cmb = _router(x1, lw["wrh"], lw["wrl"], lw["br"])
    return _moe_ln(x1, cmb, lw["wg"], lw["wu"], lw["wd"], lw["ln2_g"], lw["ln2_b"])


def _heads(t2, b, l, nh):
    return t2.reshape(b, l, nh, t2.shape[-1] // nh)


def _prompt_layer(x3, lw, tt, cosf, sins):
    b, l, _ = x3.shape
    x2 = x3.reshape(b * l, D_MODEL)
    ha, kt, vt, vtb, ikt, kh, qt, iqt, ikb, iwt = _proj_prompt(x3, lw["w_pad"])
    bsb = jnp.repeat(lw["bs"].T, HEAD_DIM, axis=1)
    a_out, a_vn = _gmlp(ha, PCOL_AU, PCOL_AV, lw["gv_gain"], lw["gv_bias"], lw["ws"], bsb)
    topk = min(TOPK_MAX, l // 4)
    b_out = _dsa_prompt(iqt, iwt, ikb, qt, kh, vtb.reshape(b, B_HEADS, HEAD_DIM, l), tt, topk)
    s0e = jnp.zeros((b, C_HEADS, HEAD_DIM, HEAD_DIM), F32)
    c_out, s_fin = _retention(ha.reshape(b, l, P_WIDTH), (PCOL_CQ, PCOL_CK, PCOL_CV, PCOL_CG), cosf, sins,
                              lw["ret_gain"], lw["ret_bias"], s0e, RET_CHUNK, 1)
    y = _channel_mix(x2, a_out, b_out.reshape(b * l, B_WIDTH), c_out.reshape(b * l, C_WIDTH), lw)
    last = ((l - 1) // CHUNK) * CHUNK
    to_rows = lambda t: jnp.transpose(t.reshape(b, B_HEADS, HEAD_DIM, l), (0, 3, 1, 2))
    state = (to_rows(kt), to_rows(vt), jnp.transpose(ikt, (0, 2, 1)), s_fin,
             a_vn.reshape(b, l, A_WIDTH)[:, last:])
    return y.reshape(b, l, D_MODEL), state


def _sample_layer(x3, lw, layer, cache_k, cache_v, cache_ik, state_l, pt_flat, npages, bfar, bnear, cosf, sins):
    bd, t, _ = x3.shape
    n = bd * t
    past = npages * PAGE_SIZE
    x2 = x3.reshape(n, D_MODEL)
    h2 = _proj(x2, lw["w_pad"])
    col = lambda c0, wdt: h2[:, c0:c0 + wdt]
    rep = CHUNK // t
    eye = jnp.eye(rep, dtype=F32)
    ws_t = lw["ws"][:, :t, :t]
    ws_bd = (eye[None, :, None, :, None] * ws_t[:, None, :, None, :]).reshape(A_GROUPS, CHUNK, CHUNK)
    bsb = jnp.repeat(jnp.tile(lw["bs"][:, :t], (1, rep)).T, HEAD_DIM, axis=1)
    a_out, a_vn = _gmlp(h2, COL_AU, COL_AV, lw["gv_gain"], lw["gv_bias"], ws_bd, bsb)
    k4 = _heads(col(COL_K, B_WIDTH), bd, t, B_HEADS)
    v4 = _heads(col(COL_V, B_WIDTH), bd, t, B_HEADS)
    ik3 = col(COL_IK, IDX_DIM).reshape(bd, t, IDX_DIM)
    feat_major = lambda a3: jnp.pad(jnp.transpose(a3, (0, 2, 1)), ((0, 0), (0, 0), (0, PAGE_SIZE - t)))
    iq4 = _heads(col(COL_IQ, IDX_HEADS * IDX_DIM), bd, t, IDX_HEADS)
    iqs = jnp.transpose(iq4, (0, 2, 1, 3)).reshape(bd, IDX_HEADS * t, IDX_DIM).astype(BF16)
    iw3 = jnp.transpose(col(COL_IW, IDX_HEADS).reshape(bd, t, IDX_HEADS), (0, 2, 1))
    iws = jnp.broadcast_to(iw3.reshape(bd, IDX_HEADS * t, 1), (bd, IDX_HEADS * t, LANES))
    scores = _sample_scores(pt_flat, iqs, iws, jnp.transpose(cache_ik, (0, 1, 3, 2)), layer, feat_major(ik3), npages)
    width = (npages + 1) * PAGE_SIZE
    topk = min(TOPK_MAX, (past + t) // 4)
    madd = _sample_mask(scores.reshape(n, width), topk, past, t).reshape(bd, t, width)
    q4 = _heads(col(COL_Q, B_WIDTH), bd, t, B_HEADS)
    eye_h = jnp.eye(B_HEADS, dtype=F32)
    qbd = (jnp.transpose(q4, (0, 2, 1, 3))[:, :, :, None, :] * eye_h[None, :, None, :, None])
    qbd = qbd.reshape(bd, B_HEADS * t, B_WIDTH).astype(BF16)
    page_view = lambda c: jnp.transpose(c, (0, 1, 3, 4, 2)).reshape(c.shape[0], c.shape[1], B_WIDTH, PAGE_SIZE)
    new_rows = lambda a4: jnp.transpose(a4.reshape(bd, t, B_WIDTH), (0, 2, 1))
    b_out = _sample_attend(pt_flat, qbd, madd, bfar, bnear, page_view(cache_k), page_view(cache_v), layer,
                           new_rows(k4), new_rows(v4), npages)
    c_out, s_fin = _retention(h2.reshape(bd, t, H_WIDTH), (COL_CQ, COL_CK, COL_CV, COL_CG), cosf, sins,
                              lw["ret_gain"], lw["ret_bias"], state_l, t, 8)
    y = _channel_mix(x2, a_out, b_out.reshape(n, B_WIDTH), c_out.reshape(n, C_WIDTH), lw)
    state = (k4, v4, ik3, s_fin, a_vn.reshape(bd, t, A_WIDTH))
    return y.reshape(bd, t, D_MODEL), state


def _distance_tables(rel_bias, t, past):
    r = jnp.arange(Q_BLOCK, dtype=I32)
    d_prompt = jnp.concatenate([off * Q_BLOCK + r[None, :] - r[:, None] for off in range(3)], axis=0)
    qpos = past + jnp.arange(t, dtype=I32)
    near0 = past - PAGE_SIZE
    d_near = [qpos[:, None] - (near0 + half * PAGE_SIZE + r[None, :]) for half in range(2)]
    d_far = jnp.full((t, LANES), MAX_DISTANCE * 2, I32)
    n0 = 3 * Q_BLOCK
    tables = _bias_tables(jnp.concatenate([d_prompt] + d_near + [d_far], axis=0), rel_bias, n0)
    tt = tables[:, :n0].reshape(B_HEADS, 3, Q_BLOCK, Q_BLOCK)
    bnear = jnp.concatenate([tables[:, n0:n0 + t], tables[:, n0 + t:n0 + 2 * t]], axis=2).reshape(B_HEADS * t, 2 * LANES)
    bfar = tables[:, n0 + 2 * t:n0 + 3 * t].reshape(B_HEADS * t, LANES)
    return tt, bnear, bfar


def kernel(x_prompt, x_sample, cache_k, cache_v, cache_idx_k, state_ret, page_table, w_in, w_out, gmlp_v_gain,
           gmlp_v_bias, gmlp_ws, gmlp_bs, rel_bias, ret_gn_gain, ret_gn_bias, ln1_gain, ln1_bias, ln2_gain, ln2_bias,
           router_group_w, router_group_b, router_expert_w, router_expert_b, expert_w_gate, expert_w_up,
           expert_w_down):
    depth = w_in.shape[0]
    seq = x_prompt.shape[1]
    bd, t, _ = x_sample.shape
    npages = page_table.shape[1]
    past = npages * PAGE_SIZE
    pt_flat = page_table.reshape(-1).astype(I32)
    tt, bnear, bfar = _distance_tables(rel_bias, t, past)
    cos_p, sin_p = _rope_tables(jnp.arange(seq, dtype=I32))
    cos_s, sin_s = _rope_tables(past + jnp.arange(t, dtype=I32))
    xp, xs = x_prompt, x_sample
    st_p, st_s = [], []
    for l in range(depth):
        lw = _layer_weights(l, w_in, w_out, gmlp_v_gain, gmlp_v_bias, gmlp_ws, gmlp_bs, ret_gn_gain, ret_gn_bias,
                            ln1_gain, ln1_bias, ln2_gain, ln2_bias, router_group_w, router_group_b,
                            router_expert_w, router_expert_b, expert_w_gate, expert_w_up, expert_w_down)
        xp, sp = _prompt_layer(xp, lw, tt, cos_p, sin_p)
        xs, ss = _sample_layer(xs, lw, l, cache_k, cache_v, cache_idx_k, state_ret[l], pt_flat, npages,
                               bfar, bnear, cos_s, sin_s)
        st_p.append(sp)
        st_s.append(ss)
    stk = lambda sts, i: jnp.stack([s[i] for s in sts], axis=0)
    return (xp, xs,
            stk(st_p, 0), stk(st_p, 1), stk(st_p, 2), stk(st_p, 3), stk(st_p, 4),
            stk(st_s, 0), stk(st_s, 1), stk(st_s, 2), stk(st_s, 3), stk(st_s, 4))
```
